```python
import math
import jax
import jax.numpy as jnp
from jax import lax
import numpy as np

D_MODEL = 1024
BATCH = 4
SEQ = 8192
DEPTH = 1

ATT_HEAD_DIM = 128
ATT_HEADS = 8
ATT_WIDTH = ATT_HEADS * ATT_HEAD_DIM
MOBA_BLOCK = 256
MOBA_TOPK = 3
MOBA_QCHUNK = 64
REL_BUCKETS = 32
REL_MAX_DIST = 2048
MLSTM_HEADS = 4
MLSTM_WIDTH = D_MODEL
MLSTM_HEAD_DIM = MLSTM_WIDTH // MLSTM_HEADS
MLSTM_CHUNK = 64
QKV_BLOCK = 4
CONV_WIDTH = 4
MIX_WIDTH = ATT_WIDTH + MLSTM_WIDTH
IN_COLS = 4 * ATT_WIDTH + 2 * MLSTM_WIDTH
RMS_EPS = 1e-6
LN_EPS = 1e-5

kernel_name = "hymba_moba_mlstm_sandwich"


def rmsnorm(x, g):
    xf = x.astype(jnp.float32)
    y = xf * lax.rsqrt(jnp.mean(xf * xf, axis=-1, keepdims=True) + RMS_EPS)
    return (y * g.astype(jnp.float32)).astype(x.dtype)


def split_heads(a, n_heads):
    b, s, _ = a.shape
    return a.reshape(b, s, n_heads, -1).transpose(0, 2, 1, 3)


def t5_bucket(dist):
    n = jnp.maximum(dist, 0)
    max_exact = REL_BUCKETS // 2
    nf = jnp.maximum(n, 1).astype(jnp.float32)
    large = max_exact + (jnp.log(nf / max_exact) / math.log(REL_MAX_DIST / max_exact)
                         * (REL_BUCKETS - max_exact)).astype(jnp.int32)
    large = jnp.minimum(large, REL_BUCKETS - 1)
    return jnp.where(n < max_exact, n, large)


def moba_attention(q, k, v, rel_bias):
    B, H, S, Dh = q.shape
    nb = -(-S // MOBA_BLOCK)
    pad = nb * MOBA_BLOCK - S
    kb = jnp.pad(k, ((0, 0), (0, 0), (0, pad), (0, 0))).reshape(B, H, nb, MOBA_BLOCK, Dh)
    vb = jnp.pad(v, ((0, 0), (0, 0), (0, pad), (0, 0))).reshape(B, H, nb, MOBA_BLOCK, Dh)
    kmean = jnp.mean(kb.astype(jnp.float32), axis=3)
    ksel = min(MOBA_TOPK, nb)
    scale = Dh ** -0.5
    bias_h = rel_bias.T.astype(jnp.float32)
    b_idx = jnp.arange(B)[:, None, None, None]
    h_idx = jnp.arange(H)[None, :, None, None]
    h_idx5 = h_idx[..., None]
    blk_off = jnp.arange(MOBA_BLOCK)
    nq = S // MOBA_QCHUNK
    qc = jnp.moveaxis(q.reshape(B, H, nq, MOBA_QCHUNK, Dh), 2, 0)

    def one_chunk(args):
        ci, qblk = args
        q_pos = ci * MOBA_QCHUNK + jnp.arange(MOBA_QCHUNK)
        own = (ci * MOBA_QCHUNK) // MOBA_BLOCK
        qf = qblk.astype(jnp.float32) * scale
        gate = jnp.einsum('bhqd,bhnd->bhqn', qf, kmean)
        gate = jnp.where(jnp.arange(nb) < own, gate, -jnp.inf)
        _, gidx = lax.top_k(gate, ksel)
        sel_valid = gidx < own
        k_sel = kb[b_idx, h_idx, gidx].astype(jnp.float32)
        v_sel = vb[b_idx, h_idx, gidx].astype(jnp.float32)
        s_sel = jnp.einsum('bhqd,bhqnkd->bhqnk', qf, k_sel)
        k_pos_sel = gidx[..., None] * MOBA_BLOCK + blk_off
        bucket_sel = t5_bucket(q_pos[None, None, :, None, None] - k_pos_sel)
        s_sel = s_sel + bias_h[h_idx5, bucket_sel]
        s_sel = jnp.where(sel_valid[..., None], s_sel, -jnp.inf)
        k_own = lax.dynamic_index_in_dim(kb, own, axis=2, keepdims=False).astype(jnp.float32)
        v_own = lax.dynamic_index_in_dim(vb, own, axis=2, keepdims=False).astype(jnp.float32)
        s_own = jnp.einsum('bhqd,bhkd->bhqk', qf, k_own)
        dist = q_pos[:, None] - (own * MOBA_BLOCK + blk_off)[None, :]
        s_own = s_own + bias_h[:, t5_bucket(dist)][None]
        s_own = jnp.where(dist >= 0, s_own, -jnp.inf)
        logits = jnp.concatenate(
            [s_sel.reshape(B, H, MOBA_QCHUNK, ksel * MOBA_BLOCK), s_own], axis=-1)
        p = jax.nn.softmax(logits, axis=-1)
        p_sel = p[..., :ksel * MOBA_BLOCK].reshape(B, H, MOBA_QCHUNK, ksel, MOBA_BLOCK)
        p_own = p[..., ksel * MOBA_BLOCK:]
        out = (jnp.einsum('bhqnk,bhqnkd->bhqd', p_sel, v_sel)
               + jnp.einsum('bhqk,bhkd->bhqd', p_own, v_own))
        return out.astype(q.dtype)

    out = lax.map(one_chunk, (jnp.arange(nq), qc))
    return jnp.moveaxis(out, 0, 2).reshape(B, H, S, Dh)


def causal_depthwise_conv(x, w, b):
    c = x.shape[-1]
    y = lax.conv_general_dilated(
        x, w[:, None, :], window_strides=(1,), padding=[(CONV_WIDTH - 1, 0)],
        dimension_numbers=('NWC', 'WIO', 'NWC'), feature_group_count=c)
    return y + b


def blockdiag_proj(x, w):
    b, s, width = x.shape
    xb = x.reshape(b, s, width // QKV_BLOCK, QKV_BLOCK)
    return jnp.einsum('bsni,nio->bsno', xb, w).reshape(b, s, width)


def mlstm_chunkwise(q, k, v, i_pre, f_pre):
    B, H, S, Dh = q.shape
    L = MLSTM_CHUNK
    nc = S // L
    qf = q.astype(jnp.float32)
    kf = k.astype(jnp.float32) * (Dh ** -0.5)
    vf = v.astype(jnp.float32)
    log_i = i_pre.astype(jnp.float32)
    log_f = jax.nn.log_sigmoid(f_pre.astype(jnp.float32))

    def to_chunks(a):
        return jnp.moveaxis(a.reshape((B, H, nc, L) + a.shape[3:]), 2, 0)

    causal = jnp.tril(jnp.ones((L, L), dtype=bool))

    def step(carry, inp):
        C, n, m = carry
        qc, kc, vc, li, lf = inp
        b = jnp.cumsum(lf, axis=-1)
        log_d = b[..., :, None] - b[..., None, :] + li[..., None, :]
        log_d = jnp.where(causal, log_d, -jnp.inf)
        inter = b + m[..., None]
        m_t = jnp.maximum(inter, jnp.max(log_d, axis=-1))
        d = jnp.exp(log_d - m_t[..., None])
        s = jnp.einsum('bhtd,bhsd->bhts', qc, kc) * d
        dec = jnp.exp(inter - m_t)
        num = (jnp.einsum('bhts,bhsd->bhtd', s, vc)
               + dec[..., None] * jnp.einsum('bhtd,bhde->bhte', qc, C))
        den = jnp.sum(s, axis=-1) + dec * jnp.einsum('bhtd,bhd->bht', qc, n)
        h = num / jnp.maximum(jnp.abs(den), jnp.exp(-m_t))[..., None]
        b_last = b[..., -1]
        log_w = b_last[..., None] - b + li
        m_new = jnp.maximum(b_last + m, jnp.max(log_w, axis=-1))
        w = jnp.exp(log_w - m_new[..., None])
        carry_dec = jnp.exp(b_last + m - m_new)
        C_new = carry_dec[..., None, None] * C + jnp.einsum('bhs,bhsd,bhse->bhde', w, kc, vc)
        n_new = carry_dec[..., None] * n + jnp.einsum('bhs,bhsd->bhd', w, kc)
        return (C_new, n_new, m_new), h

    init = (jnp.zeros((B, H, Dh, Dh), jnp.float32),
            jnp.zeros((B, H, Dh), jnp.float32),
            jnp.zeros((B, H), jnp.float32))
    _, hs = lax.scan(step, init, (to_chunks(qf), to_chunks(kf), to_chunks(vf),
                                  to_chunks(log_i), to_chunks(log_f)))
    return jnp.moveaxis(hs, 0, 2).reshape(B, H, S, Dh)


def headwise_layernorm(h, w):
    mu = jnp.mean(h, axis=-1, keepdims=True)
    var = jnp.mean(jnp.square(h - mu), axis=-1, keepdims=True)
    y = (h - mu) * lax.rsqrt(var + LN_EPS)
    b, s = h.shape[:2]
    return y.reshape(b, s, -1) * w.astype(jnp.float32)


def hybrid_layer(x, rel_bias, g_pre, g_post, w_in, conv_w, conv_b, wq_m, wk_m, wv_m,
                 w_if, b_if, mh_norm, skip, w_out):
    B, S, _ = x.shape
    h = rmsnorm(x, g_pre)
    proj = h @ w_in
    aw, mw = ATT_WIDTH, MLSTM_WIDTH
    q_a = proj[..., 0:aw]
    k_a = proj[..., aw:2 * aw]
    v_a = proj[..., 2 * aw:3 * aw]
    g_a = proj[..., 3 * aw:4 * aw]
    x_m = proj[..., 4 * aw:4 * aw + mw]
    z_m = proj[..., 4 * aw + mw:]
    o_a = moba_attention(split_heads(q_a, ATT_HEADS), split_heads(k_a, ATT_HEADS),
                         split_heads(v_a, ATT_HEADS), rel_bias)
    o_a = o_a.transpose(0, 2, 1, 3).reshape(B, S, aw)
    y_a = o_a * jax.nn.silu(g_a)
    x_c = jax.nn.silu(causal_depthwise_conv(x_m, conv_w, conv_b))
    q_m = blockdiag_proj(x_c, wq_m)
    k_m = blockdiag_proj(x_c, wk_m)
    v_m = blockdiag_proj(x_m, wv_m)
    gates = jnp.concatenate([q_m, k_m, v_m], axis=-1) @ w_if + b_if
    i_pre = gates[..., :MLSTM_HEADS].transpose(0, 2, 1)
    f_pre = gates[..., MLSTM_HEADS:].transpose(0, 2, 1)
    h_m = mlstm_chunkwise(split_heads(q_m, MLSTM_HEADS), split_heads(k_m, MLSTM_HEADS),
                          split_heads(v_m, MLSTM_HEADS), i_pre, f_pre)
    h_m = headwise_layernorm(h_m.transpose(0, 2, 1, 3), mh_norm)
    y_m = ((h_m + skip.astype(jnp.float32) * x_c.astype(jnp.float32))
           * jax.nn.silu(z_m.astype(jnp.float32))).astype(x.dtype)
    y = jnp.concatenate([y_a, y_m], axis=-1) @ w_out
    return x + rmsnorm(y, g_post)


def setup_inputs(seed: int = 0) -> dict:
    key = jax.random.key(seed)
    ks = jax.random.split(key, 16)
    f32 = jnp.float32
    nblk = MLSTM_WIDTH // QKV_BLOCK
    x = jax.random.normal(ks[0], (BATCH, SEQ, D_MODEL), f32)
    rel_bias = 0.5 * jax.random.normal(ks[1], (REL_BUCKETS, ATT_HEADS), f32)
    g_pre = 1.0 + 0.02 * jax.random.normal(ks[2], (DEPTH, D_MODEL), f32)
    g_post = 1.0 + 0.02 * jax.random.normal(ks[3], (DEPTH, D_MODEL), f32)
    w_in = jax.random.normal(ks[4], (DEPTH, D_MODEL, IN_COLS), f32) * D_MODEL ** -0.5
    conv_w = jax.random.normal(ks[5], (DEPTH, CONV_WIDTH, MLSTM_WIDTH), f32) * CONV_WIDTH ** -0.5
    conv_b = 0.02 * jax.random.normal(ks[6], (DEPTH, MLSTM_WIDTH), f32)
    wq_m = jax.random.normal(ks[7], (DEPTH, nblk, QKV_BLOCK, QKV_BLOCK), f32) * QKV_BLOCK ** -0.5
    wk_m = jax.random.normal(ks[8], (DEPTH, nblk, QKV_BLOCK, QKV_BLOCK), f32) * QKV_BLOCK ** -0.5
    wv_m = jax.random.normal(ks[9], (DEPTH, nblk, QKV_BLOCK, QKV_BLOCK), f32) * QKV_BLOCK ** -0.5
    w_if = jax.random.normal(ks[10], (DEPTH, 3 * MLSTM_WIDTH, 2 * MLSTM_HEADS), f32) * (3 * MLSTM_WIDTH) ** -0.5
    b_i = 0.1 * jax.random.normal(ks[11], (DEPTH, MLSTM_HEADS), f32)
    b_f = (jnp.linspace(3.0, 6.0, MLSTM_HEADS, dtype=f32)[None, :]
           + 0.1 * jax.random.normal(ks[12], (DEPTH, MLSTM_HEADS), f32))
    b_if = jnp.concatenate([b_i, b_f], axis=-1)
    mh_norm = 1.0 + 0.02 * jax.random.normal(ks[13], (DEPTH, MLSTM_WIDTH), f32)
    skip = 1.0 + 0.02 * jax.random.normal(ks[14], (DEPTH, MLSTM_WIDTH), f32)
    w_out = jax.random.normal(ks[15], (DEPTH, MIX_WIDTH, D_MODEL), f32) * MIX_WIDTH ** -0.5
    return {"x": x, "rel_bias": rel_bias, "g_pre": g_pre, "g_post": g_post, "w_in": w_in,
            "conv_w": conv_w, "conv_b": conv_b, "wq_m": wq_m, "wk_m": wk_m, "wv_m": wv_m,
            "w_if": w_if, "b_if": b_if, "mh_norm": mh_norm, "skip": skip, "w_out": w_out}


def reference(x, rel_bias, g_pre, g_post, w_in, conv_w, conv_b, wq_m, wk_m, wv_m,
              w_if, b_if, mh_norm, skip, w_out):
    for l in range(DEPTH):
        x = hybrid_layer(x, rel_bias, g_pre[l], g_post[l], w_in[l], conv_w[l], conv_b[l],
                         wq_m[l], wk_m[l], wv_m[l], w_if[l], b_if[l], mh_norm[l], skip[l],
                         w_out[l])
    return x
```

```python
import functools
import math

import jax
import jax.numpy as jnp
import numpy as np
from jax import lax
from jax.experimental import pallas as pl
from jax.experimental.pallas import tpu as pltpu

f32 = jnp.float32
bf16 = jnp.bfloat16

ATT_HEADS = 8
ATT_HEAD_DIM = 128
ATT_WIDTH = ATT_HEADS * ATT_HEAD_DIM
MOBA_BLOCK = 256
MOBA_TOPK = 3
REL_BUCKETS = 32
REL_MAX_DIST = 2048
MLSTM_HEADS = 4
MLSTM_WIDTH = 1024
MLSTM_HEAD_DIM = MLSTM_WIDTH // MLSTM_HEADS
QKV_BLOCK = 4
CONV_WIDTH = 4
RMS_EPS = 1e-6
LN_EPS = 1e-5

MLSTM_CHUNK = 256
TOKEN_TILE = 512
HALO_ROWS = 16
NEG_BIG = -1e30
VMEM_LIMIT = 56 * 1024 * 1024

_NT = (((1,), (1,)), ((), ()))
_TN = (((0,), (0,)), ((), ()))


def _t5_thresholds():
    n = np.arange(0, 2 * REL_MAX_DIST, dtype=np.int64)
    max_exact = REL_BUCKETS // 2
    nf = np.maximum(n, 1).astype(np.float32)
    large = max_exact + (np.log(nf / np.float32(max_exact))
                         / np.float32(math.log(REL_MAX_DIST / max_exact))
                         * np.float32(REL_BUCKETS - max_exact)).astype(np.int32)
    large = np.minimum(large, REL_BUCKETS - 1)
    bucket = np.where(n < max_exact, n, large)
    assert np.all(np.diff(bucket) >= 0)
    return [int(np.argmax(bucket >= k)) for k in range(1, REL_BUCKETS)]


T5_THR = _t5_thresholds()
NEAR_TILES = -(-(T5_THR[-1] + MOBA_BLOCK - 1) // MOBA_BLOCK)
assert NEAR_TILES * MOBA_BLOCK - (MOBA_BLOCK - 1) >= T5_THR[-1]
BIAS_TILES = NEAR_TILES + 1


def _dot(a, b):
    return jnp.dot(a, b, preferred_element_type=f32)


def _split3(x):
    hi = x.astype(bf16)
    r = x - hi.astype(f32)
    mid = r.astype(bf16)
    lo = (r - mid.astype(f32)).astype(bf16)
    return hi, mid, lo


def _bias_kernel(rb_ref, out_ref):
    h = pl.program_id(0)
    key = lax.broadcasted_iota(jnp.int32, (MOBA_BLOCK, MOBA_BLOCK), 0)
    qry = lax.broadcasted_iota(jnp.int32, (MOBA_BLOCK, MOBA_BLOCK), 1)
    base = qry - key
    for d in range(BIAS_TILES):
        dist = base + d * MOBA_BLOCK
        n = jnp.maximum(dist, 0)
        val = jnp.full((MOBA_BLOCK, MOBA_BLOCK), rb_ref[REL_BUCKETS - 1, h], f32)
        for k in range(REL_BUCKETS - 2, -1, -1):
            val = jnp.where(n < T5_THR[k], rb_ref[k, h], val)
        if d == 0:
            val = jnp.where(dist >= 0, val, NEG_BIG)
        out_ref[d] = val


def _bias_tiles(rel_bias):
    return pl.pallas_call(
        _bias_kernel,
        grid=(ATT_HEADS,),
        in_specs=[pl.BlockSpec(memory_space=pltpu.SMEM)],
        out_specs=pl.BlockSpec((None, BIAS_TILES, MOBA_BLOCK, MOBA_BLOCK), lambda h: (h, 0, 0, 0)),
        out_shape=jax.ShapeDtypeStruct((ATT_HEADS, BIAS_TILES, MOBA_BLOCK, MOBA_BLOCK), f32),
        name="bias_tiles",
    )(rel_bias)


def _inproj_kernel(x_ref, g_ref, wqT_ref, wvT_ref, wr_ref, qT_ref, vT_ref, pr_ref, *, tm, q_scale):
    x = x_ref[...]
    ms = jnp.mean(x * x, axis=-1, keepdims=True)
    h = (x * lax.rsqrt(ms + RMS_EPS) * g_ref[...]).astype(bf16)
    for cc in range(ATT_WIDTH // 256):
        rows = slice(cc * 256, (cc + 1) * 256)
        qt = lax.dot_general(wqT_ref[rows, :], h, _NT, preferred_element_type=f32) * q_scale
        vt = lax.dot_general(wvT_ref[rows, :], h, _NT, preferred_element_type=f32)
        for u in range(tm // MOBA_BLOCK):
            cols = slice(u * MOBA_BLOCK, (u + 1) * MOBA_BLOCK)
            qT_ref[u, rows, :] = qt[:, cols].astype(bf16)
            vT_ref[u, rows, :] = vt[:, cols].astype(bf16)
    ncols = pr_ref.shape[1]
    for cc in range(ncols // 512):
        cols = slice(cc * 512, (cc + 1) * 512)
        pr_ref[:, cols] = _dot(h, wr_ref[:, cols]).astype(bf16)


def _inproj(x2, g_pre, wqT, wvT, wr):
    T, D = x2.shape
    tm = TOKEN_TILE
    nblk = tm // MOBA_BLOCK
    ncols = wr.shape[1]
    resident = dict(pipeline_mode=pl.Buffered(1))
    return pl.pallas_call(
        functools.partial(_inproj_kernel, tm=tm, q_scale=ATT_HEAD_DIM ** -0.5),
        grid=(T // tm,),
        in_specs=[
            pl.BlockSpec((tm, D), lambda i: (i, 0)),
            pl.BlockSpec((1, D), lambda i: (0, 0)),
            pl.BlockSpec((ATT_WIDTH, D), lambda i: (0, 0), **resident),
            pl.BlockSpec((ATT_WIDTH, D), lambda i: (0, 0), **resident),
            pl.BlockSpec((D, ncols), lambda i: (0, 0), **resident),
        ],
        out_specs=[
            pl.BlockSpec((nblk, ATT_WIDTH, MOBA_BLOCK), lambda i: (i, 0, 0)),
            pl.BlockSpec((nblk, ATT_WIDTH, MOBA_BLOCK), lambda i: (i, 0, 0)),
            pl.BlockSpec((tm, ncols), lambda i: (i, 0)),
        ],
        out_shape=[
            jax.ShapeDtypeStruct((T // MOBA_BLOCK, ATT_WIDTH, MOBA_BLOCK), bf16),
            jax.ShapeDtypeStruct((T // MOBA_BLOCK, ATT_WIDTH, MOBA_BLOCK), bf16),
            jax.ShapeDtypeStruct((T, ncols), bf16),
        ],
        compiler_params=pltpu.CompilerParams(
            dimension_semantics=("arbitrary",), vmem_limit_bytes=VMEM_LIMIT),
        name="inproj",
    )(x2, g_pre, wqT, wvT, wr)


def _log_sigmoid(v):
    return jnp.minimum(v, 0.0) - jnp.log1p(jnp.exp(-jnp.abs(v)))


def _prep_kernel(xm_ref, halo_ref, cw_ref, cb_ref, wq_ref, wk_ref, wv_ref, wif_ref, wifT_ref,
                 bif_ref, bifT_ref, xc_ref, qm_ref, km_ref, vm_ref, gc_ref, gr_ref, xpad_ref,
                 *, tm, seq, k_scale):
    i = pl.program_id(0)
    W = MLSTM_WIDTH
    nh = MLSTM_HEADS
    L = MLSTM_CHUNK
    xm_bf = xm_ref[...]
    xm = xm_bf.astype(f32)
    at_seq_start = (i * tm) % seq == 0
    xpad_ref[0:HALO_ROWS, :] = jnp.where(at_seq_start, 0.0, halo_ref[...].astype(f32))
    xpad_ref[HALO_ROWS:HALO_ROWS + tm, :] = xm
    acc = cb_ref[...] + cw_ref[CONV_WIDTH - 1:CONV_WIDTH, :] * xm
    for j in range(CONV_WIDTH - 1):
        off = HALO_ROWS - (CONV_WIDTH - 1) + j
        acc = acc + cw_ref[j:j + 1, :] * xpad_ref[off:off + tm, :]
    xc_bf = (acc * jax.nn.sigmoid(acc)).astype(bf16)
    xc_ref[...] = xc_bf

    nd = W // 256
    for g in range(nd):
        sl = slice(g * 256, (g + 1) * 256)
        qm_ref[:, sl] = _dot(xc_bf[:, sl], wq_ref[g]).astype(bf16)
        km_ref[:, sl] = (_dot(xc_bf[:, sl], wk_ref[g]) * k_scale).astype(bf16)
        vm_ref[:, sl] = _dot(xm_bf[:, sl], wv_ref[g]).astype(bf16)
    qm = qm_ref[...]
    km = km_ref[...]
    vm = vm_ref[...]
    k_unscale = 1.0 / k_scale
    gates_c = (_dot(qm, wif_ref[0:W, :]) + k_unscale * _dot(km, wif_ref[W:2 * W, :])
               + _dot(vm, wif_ref[2 * W:3 * W, :]) + bif_ref[...])
    gates_r = (lax.dot_general(wifT_ref[:, 0:W], qm, _NT, preferred_element_type=f32)
               + k_unscale * lax.dot_general(wifT_ref[:, W:2 * W], km, _NT, preferred_element_type=f32)
               + lax.dot_general(wifT_ref[:, 2 * W:3 * W], vm, _NT, preferred_element_type=f32)
               + bifT_ref[...])
    li_c = gates_c[:, 0:nh]
    lf_c = _log_sigmoid(gates_c[:, nh:2 * nh])
    li_r = gates_r[0:nh, :]
    lf_r = _log_sigmoid(gates_r[nh:2 * nh, :])

    row = lax.broadcasted_iota(jnp.int32, (L, L), 0)
    col = lax.broadcasted_iota(jnp.int32, (L, L), 1)
    lower = jnp.where(col <= row, 1.0, 0.0).astype(bf16)
    upper = jnp.where(row <= col, 1.0, 0.0).astype(bf16)
    gc_ref[:, nh:2 * nh] = li_c
    for u in range(tm // L):
        rs = slice(u * L, (u + 1) * L)
        b_c = sum(_dot(lower, part) for part in _split3(lf_c[rs, :]))
        b_r = sum(_dot(part, upper) for part in _split3(lf_r[:, rs]))
        gc_ref[rs, 0:nh] = b_c
        gr_ref[0:nh, rs] = li_r[:, rs] - b_r
        gr_ref[nh:2 * nh, rs] = b_r


def _mlstm_prep(pr, conv_w, conv_b, wq_bd, wk_bd, wv_bd, wif, wifT, bif, bifT, seq):
    T = pr.shape[0]
    W = MLSTM_WIDTH
    tm = TOKEN_TILE
    xm_col = 2
    halo_per_tile = tm // HALO_ROWS
    ng = 2 * MLSTM_HEADS
    resident = dict(pipeline_mode=pl.Buffered(1))
    const2 = lambda i: (0, 0)
    const3 = lambda i: (0, 0, 0)
    tok = pl.BlockSpec((tm, W), lambda i: (i, 0))
    return pl.pallas_call(
        functools.partial(_prep_kernel, tm=tm, seq=seq, k_scale=MLSTM_HEAD_DIM ** -0.5),
        grid=(T // tm,),
        in_specs=[
            pl.BlockSpec((tm, W), lambda i: (i, xm_col)),
            pl.BlockSpec((HALO_ROWS, W), lambda i: (jnp.maximum(i * halo_per_tile - 1, 0), xm_col)),
            pl.BlockSpec((CONV_WIDTH, W), const2),
            pl.BlockSpec((1, W), const2),
            pl.BlockSpec((W // 256, 256, 256), const3, **resident),
            pl.BlockSpec((W // 256, 256, 256), const3, **resident),
            pl.BlockSpec((W // 256, 256, 256), const3, **resident),
            pl.BlockSpec((3 * W, ng), const2, **resident),
            pl.BlockSpec((ng, 3 * W), const2, **resident),
            pl.BlockSpec((1, ng), const2),
            pl.BlockSpec((ng, 1), const2),
        ],
        out_specs=[tok, tok, tok, tok,
                   pl.BlockSpec((tm, ng), lambda i: (i, 0)),
                   pl.BlockSpec((ng, tm), lambda i: (0, i))],
        out_shape=[jax.ShapeDtypeStruct((T, W), bf16)] * 4
                  + [jax.ShapeDtypeStruct((T, ng), f32), jax.ShapeDtypeStruct((ng, T), f32)],
        scratch_shapes=[pltpu.VMEM((HALO_ROWS + tm, W), f32)],
        compiler_params=pltpu.CompilerParams(
            dimension_semantics=("arbitrary",), vmem_limit_bytes=VMEM_LIMIT),
        name="mlstm_prep",
    )(pr, pr, conv_w, conv_b, wq_bd, wk_bd, wv_bd, wif, wifT, bif, bifT)


def _mlstm_kernel(qm_ref, km_ref, vm_ref, xc_ref, z_ref, gc_ref, gr_ref, nw_ref, sk_ref,
                  y_ref, c_ref, n_ref, m_ref):
    L = MLSTM_CHUNK
    nh = MLSTM_HEADS
    dh = MLSTM_HEAD_DIM

    @pl.when(pl.program_id(1) == 0)
    def _():
        c_ref[...] = jnp.zeros_like(c_ref)
        n_ref[...] = jnp.zeros_like(n_ref)
        m_ref[...] = jnp.zeros_like(m_ref)

    t_idx = lax.broadcasted_iota(jnp.int32, (L, L), 0)
    s_idx = lax.broadcasted_iota(jnp.int32, (L, L), 1)
    causal = s_idx <= t_idx
    for h in range(nh):
        sl = slice(h * dh, (h + 1) * dh)
        q = qm_ref[:, sl]
        k = km_ref[:, sl]
        v = vm_ref[:, sl]
        b_c = gc_ref[:, h:h + 1]
        li_c = gc_ref[:, nh + h:nh + h + 1]
        a_r = gr_ref[h:h + 1, :]
        b_last = b_c[L - 1:L, :]
        m_prev = m_ref[h][0:1, 0:1]
        c_prev = c_ref[h]
        n_prev = n_ref[h]

        log_d = jnp.where(causal, b_c + a_r, NEG_BIG)
        inter = b_c + m_prev
        m_t = jnp.maximum(inter, jnp.max(log_d, axis=1, keepdims=True))
        d = jnp.exp(log_d - m_t)
        s = lax.dot_general(q, k, _NT, preferred_element_type=f32) * d
        dec = jnp.exp(inter - m_t)
        qf = q.astype(f32)
        num = _dot(s.astype(bf16), v) + dec * _dot(q, c_prev.astype(bf16))
        den = (jnp.sum(s, axis=1, keepdims=True)
               + dec * jnp.sum(qf * n_prev, axis=1, keepdims=True))
        hh = num / jnp.maximum(jnp.abs(den), jnp.exp(-m_t))

        mu = jnp.mean(hh, axis=1, keepdims=True)
        cen = hh - mu
        var = jnp.mean(cen * cen, axis=1, keepdims=True)
        yn = cen * lax.rsqrt(var + LN_EPS) * nw_ref[:, sl]
        z = z_ref[:, sl].astype(f32)
        out = (yn + sk_ref[:, sl] * xc_ref[:, sl].astype(f32)) * (z * jax.nn.sigmoid(z))
        y_ref[:, sl] = out.astype(y_ref.dtype)

        log_w = b_last - b_c + li_c
        m_new = jnp.maximum(b_last + m_prev, jnp.max(log_w, axis=0, keepdims=True))
        w = jnp.exp(log_w - m_new)
        carry = jnp.exp(b_last + m_prev - m_new)
        kw = k.astype(f32) * w
        c_ref[h] = carry * c_prev + lax.dot_general(kw.astype(bf16), v, _TN,
                                                    preferred_element_type=f32)
        n_ref[h] = carry * n_prev + jnp.sum(kw, axis=0, keepdims=True)
        m_ref[h] = jnp.broadcast_to(m_new, m_ref.shape[1:])


def _mlstm(qm, km, vm, xc, pr, gc, gr, mh_norm, skip, batch, seq):
    T, W = qm.shape
    L = MLSTM_CHUNK
    nc = seq // L
    ng = 2 * MLSTM_HEADS
    z_col = 3
    tok = pl.BlockSpec((L, W), lambda b, c: (b * nc + c, 0))
    return pl.pallas_call(
        _mlstm_kernel,
        grid=(batch, nc),
        in_specs=[tok, tok, tok, tok,
                  pl.BlockSpec((L, W), lambda b, c: (b * nc + c, z_col)),
                  pl.BlockSpec((L, ng), lambda b, c: (b * nc + c, 0)),
                  pl.BlockSpec((ng, L), lambda b, c: (0, b * nc + c)),
                  pl.BlockSpec((1, W), lambda b, c: (0, 0)),
                  pl.BlockSpec((1, W), lambda b, c: (0, 0))],
        out_specs=tok,
        out_shape=jax.ShapeDtypeStruct((T, W), bf16),
        scratch_shapes=[pltpu.VMEM((MLSTM_HEADS, MLSTM_HEAD_DIM, MLSTM_HEAD_DIM), f32),
                        pltpu.VMEM((MLSTM_HEADS, 1, MLSTM_HEAD_DIM), f32),
                        pltpu.VMEM((MLSTM_HEADS, 8, 128), f32)],
        compiler_params=pltpu.CompilerParams(
            dimension_semantics=("arbitrary", "arbitrary"), vmem_limit_bytes=VMEM_LIMIT),
        name="mlstm",
    )(qm, km, vm, xc, pr, gc, gr, mh_norm, skip)


def _moba_kernel(qT_ref, k_ref, vT_ref, g_ref, bias_ref, o_ref, kmean_ref, sel_ref, *, nb):
    blk_len = MOBA_BLOCK
    own = pl.program_id(2)

    @pl.when(own == 0)
    def _():
        def mean_body(j, c):
            kb = k_ref[pl.ds(pl.multiple_of(j * blk_len, blk_len), blk_len), :].astype(f32)
            kmean_ref[pl.ds(j, 1), :] = jnp.sum(kb, axis=0, keepdims=True) * (1.0 / blk_len)
            return c
        lax.fori_loop(0, nb, mean_body, 0)

    qT = qT_ref[...]
    km = kmean_ref[...]
    km_hi = km.astype(bf16)
    km_lo = (km - km_hi.astype(f32)).astype(bf16)
    gate = _dot(km_hi, qT) + _dot(km_lo, qT)
    blk = lax.broadcasted_iota(jnp.int32, gate.shape, 0)
    past = blk < own
    g = jnp.where(past, gate, -jnp.inf)
    sel = blk == own
    for _ in range(MOBA_TOPK):
        mx = jnp.max(g, axis=0, keepdims=True)
        first = jnp.min(jnp.where(g == mx, blk, nb), axis=0, keepdims=True)
        pick = blk == first
        sel = sel | (pick & past)
        g = jnp.where(pick, -jnp.inf, g)
    sel_ref[...] = jnp.where(sel, 0.0, NEG_BIG)

    def kv_body(j, carry):
        m, l, acc = carry
        kj = k_ref[pl.ds(pl.multiple_of(j * blk_len, blk_len), blk_len), :]
        s = _dot(kj, qT) + bias_ref[jnp.minimum(own - j, BIAS_TILES - 1)]
        selrow = sel_ref[pl.ds(j, 1), :]
        m_new = jnp.maximum(m, jnp.max(s, axis=0, keepdims=True) + selrow)
        p = jnp.exp(s - jnp.where(selrow < 0.0, -NEG_BIG, m_new))
        alpha = jnp.exp(m - m_new)
        l = alpha * l + jnp.sum(p, axis=0, keepdims=True)
        acc = alpha * acc + _dot(vT_ref[j], p.astype(bf16))
        return m_new, l, acc

    nq = qT.shape[1]
    init = (jnp.full((1, nq), NEG_BIG, f32), jnp.zeros((1, nq), f32),
            jnp.zeros((ATT_HEAD_DIM, nq), f32))
    _, l, acc = lax.fori_loop(0, own + 1, kv_body, init)
    o = (acc / l).T
    gg = g_ref[...].astype(f32)
    o_ref[...] = (o * (gg * jax.nn.sigmoid(gg))).astype(o_ref.dtype)


def _moba(qT, vT, pr, bias_tiles, batch, seq):
    T = pr.shape[0]
    nb = seq // MOBA_BLOCK
    dh = ATT_HEAD_DIM
    gate_col0 = ATT_WIDTH // dh
    vT4 = vT.reshape(batch, nb, ATT_WIDTH, MOBA_BLOCK)
    return pl.pallas_call(
        functools.partial(_moba_kernel, nb=nb),
        grid=(batch, ATT_HEADS, nb),
        in_specs=[
            pl.BlockSpec((None, dh, MOBA_BLOCK), lambda b, h, i: (b * nb + i, h, 0)),
            pl.BlockSpec((seq, dh), lambda b, h, i: (b, h)),
            pl.BlockSpec((None, nb, dh, MOBA_BLOCK), lambda b, h, i: (b, 0, h, 0)),
            pl.BlockSpec((MOBA_BLOCK, dh), lambda b, h, i: (b * nb + i, gate_col0 + h)),
            pl.BlockSpec((None, BIAS_TILES, MOBA_BLOCK, MOBA_BLOCK), lambda b, h, i: (h, 0, 0, 0)),
        ],
        out_specs=pl.BlockSpec((MOBA_BLOCK, dh), lambda b, h, i: (b * nb + i, h)),
        out_shape=jax.ShapeDtypeStruct((T, ATT_WIDTH), bf16),
        scratch_shapes=[pltpu.VMEM((nb, dh), f32), pltpu.VMEM((nb, MOBA_BLOCK), f32)],
        compiler_params=pltpu.CompilerParams(
            dimension_semantics=("arbitrary", "arbitrary", "arbitrary"),
            vmem_limit_bytes=VMEM_LIMIT),
        name="moba",
    )(qT, pr, vT4, pr, bias_tiles)


def _outproj_kernel(ya_ref, ym_ref, wa_ref, wm_ref, g_ref, x_ref, o_ref):
    y = _dot(ya_ref[...], wa_ref[...]) + _dot(ym_ref[...], wm_ref[...])
    ms = jnp.mean(y * y, axis=-1, keepdims=True)
    o_ref[...] = x_ref[...] + y * lax.rsqrt(ms + RMS_EPS) * g_ref[...]


def _outproj(ya, ym, wa, wm, g_post, x2):
    T, D = x2.shape
    tm = TOKEN_TILE
    resident = dict(pipeline_mode=pl.Buffered(1))
    return pl.pallas_call(
        _outproj_kernel,
        grid=(T // tm,),
        in_specs=[
            pl.BlockSpec((tm, ya.shape[1]), lambda i: (i, 0)),
            pl.BlockSpec((tm, ym.shape[1]), lambda i: (i, 0)),
            pl.BlockSpec(wa.shape, lambda i: (0, 0), **resident),
            pl.BlockSpec(wm.shape, lambda i: (0, 0), **resident),
            pl.BlockSpec((1, D), lambda i: (0, 0)),
            pl.BlockSpec((tm, D), lambda i: (i, 0)),
        ],
        out_specs=pl.BlockSpec((tm, D), lambda i: (i, 0)),
        out_shape=jax.ShapeDtypeStruct((T, D), f32),
        compiler_params=pltpu.CompilerParams(
            dimension_semantics=("arbitrary",), vmem_limit_bytes=VMEM_LIMIT),
        name="outproj",
    )(ya, ym, wa, wm, g_post, x2)


def _block_diag_256(w):
    per = 256 // QKV_BLOCK
    wg = w.reshape(-1, per, QKV_BLOCK, QKV_BLOCK)
    eye = jnp.eye(per, dtype=w.dtype)
    dense = jnp.einsum('gnio,nm->gnimo', wg, eye)
    return dense.reshape(-1, 256, 256).astype(bf16)


def _layer(x, rel_bias, g_pre, g_post, w_in, conv_w, conv_b, wq_m, wk_m, wv_m,
           w_if, b_if, mh_norm, skip, w_out):
    batch, seq, d_model = x.shape
    assert seq % TOKEN_TILE == 0 and TOKEN_TILE % MLSTM_CHUNK == 0 and TOKEN_TILE % MOBA_BLOCK == 0
    aw, mw = ATT_WIDTH, MLSTM_WIDTH
    x2 = x.reshape(batch * seq, d_model)

    w_bf = w_in.astype(bf16)
    wqT = w_bf[:, 0:aw].T
    wvT = w_bf[:, 2 * aw:3 * aw].T
    wr = jnp.concatenate([w_bf[:, aw:2 * aw], w_bf[:, 3 * aw:]], axis=1)
    wif = w_if.astype(bf16)
    w_out_bf = w_out.astype(bf16)

    bias_tiles = _bias_tiles(rel_bias)
    qT, vT, pr = _inproj(x2, g_pre.reshape(1, -1), wqT, wvT, wr)
    xc, qm, km, vm, gc, gr = _mlstm_prep(
        pr, conv_w, conv_b.reshape(1, -1), _block_diag_256(wq_m), _block_diag_256(wk_m),
        _block_diag_256(wv_m), wif, wif.T, b_if.reshape(1, -1), b_if.reshape(-1, 1), seq)
    ym = _mlstm(qm, km, vm, xc, pr, gc, gr, mh_norm.reshape(1, -1), skip.reshape(1, -1), batch, seq)
    ya = _moba(qT, vT, pr, bias_tiles, batch, seq)
    out = _outproj(ya, ym, w_out_bf[0:aw], w_out_bf[aw:], g_post.reshape(1, -1), x2)
    return out.reshape(batch, seq, d_model)


def kernel(x, rel_bias, g_pre, g_post, w_in, conv_w, conv_b, wq_m, wk_m, wv_m, w_if, b_if,
           mh_norm, skip, w_out):
    depth = w_in.shape[0]
    for l in range(depth):
        x = _layer(x, rel_bias, g_pre[l], g_post[l], w_in[l], conv_w[l], conv_b[l], wq_m[l],
                   wk_m[l], wv_m[l], w_if[l], b_if[l], mh_norm[l], skip[l], w_out[l])
    return x
```

```python
import functools
import math

import jax
import jax.numpy as jnp
import numpy as np
from jax import lax
from jax.experimental import pallas as pl
from jax.experimental.pallas import tpu as pltpu

f32 = jnp.float32
bf16 = jnp.bfloat16

ATT_HEADS = 8
ATT_HEAD_DIM = 128
ATT_WIDTH = ATT_HEADS * ATT_HEAD_DIM
MOBA_BLOCK = 256
MOBA_TOPK = 3
REL_BUCKETS = 32
REL_MAX_DIST = 2048
MLSTM_HEADS = 4
MLSTM_WIDTH = 1024
MLSTM_HEAD_DIM = MLSTM_WIDTH // MLSTM_HEADS
QKV_BLOCK = 4
CONV_WIDTH = 4
RMS_EPS = 1e-6
LN_EPS = 1e-5

MLSTM_CHUNK = 256
TOKEN_TILE = 512
MOBA_HEADS_PER_STEP = 4
HALO_ROWS = 16
SUM_ROWS = 16
NEG_BIG = -1e30
VMEM_LIMIT = 56 * 1024 * 1024

_NT = (((1,), (1,)), ((), ()))
_TN = (((0,), (0,)), ((), ()))


def _t5_thresholds():
    n = np.arange(0, 2 * REL_MAX_DIST, dtype=np.int64)
    max_exact = REL_BUCKETS // 2
    nf = np.maximum(n, 1).astype(np.float32)
    large = max_exact + (np.log(nf / np.float32(max_exact))
                         / np.float32(math.log(REL_MAX_DIST / max_exact))
                         * np.float32(REL_BUCKETS - max_exact)).astype(np.int32)
    large = np.minimum(large, REL_BUCKETS - 1)
    bucket = np.where(n < max_exact, n, large)
    assert np.all(np.diff(bucket) >= 0)
    return [int(np.argmax(bucket >= k)) for k in range(1, REL_BUCKETS)]


T5_THR = _t5_thresholds()
NEAR_TILES = -(-(T5_THR[-1] + MOBA_BLOCK - 1) // MOBA_BLOCK)
assert NEAR_TILES * MOBA_BLOCK - (MOBA_BLOCK - 1) >= T5_THR[-1]
LOG2E = math.log2(math.e)


def _dot(a, b):
    return jnp.dot(a, b, preferred_element_type=f32)


def _split3(x):
    hi = x.astype(bf16)
    r = x - hi.astype(f32)
    mid = r.astype(bf16)
    lo = (r - mid.astype(f32)).astype(bf16)
    return hi, mid, lo


def _bias_kernel(rb_ref, out_ref):
    h = pl.program_id(0)
    key = lax.broadcasted_iota(jnp.int32, (MOBA_BLOCK, MOBA_BLOCK), 0)
    qry = lax.broadcasted_iota(jnp.int32, (MOBA_BLOCK, MOBA_BLOCK), 1)
    base = qry - key
    for d in range(NEAR_TILES):
        dist = base + d * MOBA_BLOCK
        n = jnp.maximum(dist, 0)
        val = jnp.full((MOBA_BLOCK, MOBA_BLOCK), rb_ref[REL_BUCKETS - 1, h] * LOG2E, f32)
        for k in range(REL_BUCKETS - 2, -1, -1):
            val = jnp.where(n < T5_THR[k], rb_ref[k, h] * LOG2E, val)
        if d == 0:
            val = jnp.where(dist >= 0, val, NEG_BIG)
        out_ref[d] = val


def _bias_tiles(rel_bias):
    return pl.pallas_call(
        _bias_kernel,
        grid=(ATT_HEADS,),
        in_specs=[pl.BlockSpec(memory_space=pltpu.SMEM)],
        out_specs=pl.BlockSpec((None, NEAR_TILES, MOBA_BLOCK, MOBA_BLOCK), lambda h: (h, 0, 0, 0)),
        out_shape=jax.ShapeDtypeStruct((ATT_HEADS, NEAR_TILES, MOBA_BLOCK, MOBA_BLOCK), f32),
        name="bias_tiles",
    )(rel_bias)


def _inproj_kernel(x_ref, g_ref, wqT_ref, wvT_ref, wr_ref, qT_ref, vT_ref, pr_ref, *, tm, q_scale):
    x = x_ref[...]
    ms = jnp.mean(x * x, axis=-1, keepdims=True)
    h = (x * lax.rsqrt(ms + RMS_EPS) * g_ref[...]).astype(bf16)
    for cc in range(ATT_WIDTH // 256):
        rows = slice(cc * 256, (cc + 1) * 256)
        qt = lax.dot_general(wqT_ref[rows, :], h, _NT, preferred_element_type=f32) * q_scale
        vt = lax.dot_general(wvT_ref[rows, :], h, _NT, preferred_element_type=f32)
        for u in range(tm // MOBA_BLOCK):
            cols = slice(u * MOBA_BLOCK, (u + 1) * MOBA_BLOCK)
            qT_ref[u, rows, :] = qt[:, cols].astype(bf16)
            vT_ref[u, rows, :] = vt[:, cols].astype(bf16)
    ncols = pr_ref.shape[1]
    for cc in range(ncols // 512):
        cols = slice(cc * 512, (cc + 1) * 512)
        pr_ref[:, cols] = _dot(h, wr_ref[:, cols]).astype(bf16)


def _inproj(x2, g_pre, wqT, wvT, wr):
    T, D = x2.shape
    tm = TOKEN_TILE
    nblk = tm // MOBA_BLOCK
    ncols = wr.shape[1]
    resident = dict(pipeline_mode=pl.Buffered(1))
    return pl.pallas_call(
        functools.partial(_inproj_kernel, tm=tm, q_scale=ATT_HEAD_DIM ** -0.5 * LOG2E),
        grid=(T // tm,),
        in_specs=[
            pl.BlockSpec((tm, D), lambda i: (i, 0)),
            pl.BlockSpec((1, D), lambda i: (0, 0)),
            pl.BlockSpec((ATT_WIDTH, D), lambda i: (0, 0), **resident),
            pl.BlockSpec((ATT_WIDTH, D), lambda i: (0, 0), **resident),
            pl.BlockSpec((D, ncols), lambda i: (0, 0), **resident),
        ],
        out_specs=[
            pl.BlockSpec((nblk, ATT_WIDTH, MOBA_BLOCK), lambda i: (i, 0, 0)),
            pl.BlockSpec((nblk, ATT_WIDTH, MOBA_BLOCK), lambda i: (i, 0, 0)),
            pl.BlockSpec((tm, ncols), lambda i: (i, 0)),
        ],
        out_shape=[
            jax.ShapeDtypeStruct((T // MOBA_BLOCK, ATT_WIDTH, MOBA_BLOCK), bf16),
            jax.ShapeDtypeStruct((T // MOBA_BLOCK, ATT_WIDTH, MOBA_BLOCK), bf16),
            jax.ShapeDtypeStruct((T, ncols), bf16),
        ],
        compiler_params=pltpu.CompilerParams(
            dimension_semantics=("arbitrary",), vmem_limit_bytes=VMEM_LIMIT),
        name="inproj",
    )(x2, g_pre, wqT, wvT, wr)


def _log_sigmoid(v):
    return jnp.minimum(v, 0.0) - jnp.log1p(jnp.exp(-jnp.abs(v)))


def _prep_kernel(xm_ref, halo_ref, cw_ref, cb_ref, wq_ref, wk_ref, wv_ref, wif_ref, wifT_ref,
                 bif_ref, bifT_ref, xc_ref, qm_ref, km_ref, vm_ref, gc_ref, gr_ref, xpad_ref,
                 *, tm, seq, k_scale):
    i = pl.program_id(0)
    W = MLSTM_WIDTH
    nh = MLSTM_HEADS
    L = MLSTM_CHUNK
    xm_bf = xm_ref[...]
    xm = xm_bf.astype(f32)
    at_seq_start = (i * tm) % seq == 0
    xpad_ref[0:HALO_ROWS, :] = jnp.where(at_seq_start, 0.0, halo_ref[...].astype(f32))
    xpad_ref[HALO_ROWS:HALO_ROWS + tm, :] = xm
    acc = cb_ref[...] + cw_ref[CONV_WIDTH - 1:CONV_WIDTH, :] * xm
    for j in range(CONV_WIDTH - 1):
        off = HALO_ROWS - (CONV_WIDTH - 1) + j
        acc = acc + cw_ref[j:j + 1, :] * xpad_ref[off:off + tm, :]
    xc_bf = (acc * jax.nn.sigmoid(acc)).astype(bf16)
    xc_ref[...] = xc_bf

    nd = W // 256
    for g in range(nd):
        sl = slice(g * 256, (g + 1) * 256)
        qm_ref[:, sl] = _dot(xc_bf[:, sl], wq_ref[g]).astype(bf16)
        km_ref[:, sl] = (_dot(xc_bf[:, sl], wk_ref[g]) * k_scale).astype(bf16)
        vm_ref[:, sl] = _dot(xm_bf[:, sl], wv_ref[g]).astype(bf16)
    qm = qm_ref[...]
    km = km_ref[...]
    vm = vm_ref[...]
    k_unscale = 1.0 / k_scale
    gates_c = (_dot(qm, wif_ref[0:W, :]) + k_unscale * _dot(km, wif_ref[W:2 * W, :])
               + _dot(vm, wif_ref[2 * W:3 * W, :]) + bif_ref[...])
    gates_r = (lax.dot_general(wifT_ref[:, 0:W], qm, _NT, preferred_element_type=f32)
               + k_unscale * lax.dot_general(wifT_ref[:, W:2 * W], km, _NT, preferred_element_type=f32)
               + lax.dot_general(wifT_ref[:, 2 * W:3 * W], vm, _NT, preferred_element_type=f32)
               + bifT_ref[...])
    li_c = gates_c[:, 0:nh]
    lf_c = _log_sigmoid(gates_c[:, nh:2 * nh])
    li_r = gates_r[0:nh, :]
    lf_r = _log_sigmoid(gates_r[nh:2 * nh, :])

    row = lax.broadcasted_iota(jnp.int32, (L, L), 0)
    col = lax.broadcasted_iota(jnp.int32, (L, L), 1)
    lower = jnp.where(col <= row, 1.0, 0.0).astype(bf16)
    upper = jnp.where(row <= col, 1.0, 0.0).astype(bf16)
    gc_ref[:, nh:2 * nh] = li_c
    for u in range(tm // L):
        rs = slice(u * L, (u + 1) * L)
        b_c = sum(_dot(lower, part) for part in _split3(lf_c[rs, :]))
        b_r = sum(_dot(part, upper) for part in _split3(lf_r[:, rs]))
        gc_ref[rs, 0:nh] = b_c
        gr_ref[0:nh, rs] = li_r[:, rs] - b_r
        gr_ref[nh:2 * nh, rs] = b_r


def _mlstm_prep(pr, conv_w, conv_b, wq_bd, wk_bd, wv_bd, wif, wifT, bif, bifT, seq):
    T = pr.shape[0]
    W = MLSTM_WIDTH
    tm = TOKEN_TILE
    xm_col = 2
    halo_per_tile = tm // HALO_ROWS
    ng = 2 * MLSTM_HEADS
    resident = dict(pipeline_mode=pl.Buffered(1))
    const2 = lambda i: (0, 0)
    const3 = lambda i: (0, 0, 0)
    tok = pl.BlockSpec((tm, W), lambda i: (i, 0))
    return pl.pallas_call(
        functools.partial(_prep_kernel, tm=tm, seq=seq, k_scale=MLSTM_HEAD_DIM ** -0.5),
        grid=(T // tm,),
        in_specs=[
            pl.BlockSpec((tm, W), lambda i: (i, xm_col)),
            pl.BlockSpec((HALO_ROWS, W), lambda i: (jnp.maximum(i * halo_per_tile - 1, 0), xm_col)),
            pl.BlockSpec((CONV_WIDTH, W), const2),
            pl.BlockSpec((1, W), const2),
            pl.BlockSpec((W // 256, 256, 256), const3, **resident),
            pl.BlockSpec((W // 256, 256, 256), const3, **resident),
            pl.BlockSpec((W // 256, 256, 256), const3, **resident),
            pl.BlockSpec((3 * W, ng), const2, **resident),
            pl.BlockSpec((ng, 3 * W), const2, **resident),
            pl.BlockSpec((1, ng), const2),
            pl.BlockSpec((ng, 1), const2),
        ],
        out_specs=[tok, tok, tok, tok,
                   pl.BlockSpec((tm, ng), lambda i: (i, 0)),
                   pl.BlockSpec((ng, tm), lambda i: (0, i))],
        out_shape=[jax.ShapeDtypeStruct((T, W), bf16)] * 4
                  + [jax.ShapeDtypeStruct((T, ng), f32), jax.ShapeDtypeStruct((ng, T), f32)],
        scratch_shapes=[pltpu.VMEM((HALO_ROWS + tm, W), f32)],
        compiler_params=pltpu.CompilerParams(
            dimension_semantics=("arbitrary",), vmem_limit_bytes=VMEM_LIMIT),
        name="mlstm_prep",
    )(pr, pr, conv_w, conv_b, wq_bd, wk_bd, wv_bd, wif, wifT, bif, bifT)


def _mlstm_kernel(qm_ref, km_ref, vm_ref, xc_ref, z_ref, gc_ref, gr_ref, nw_ref, sk_ref,
                  y_ref, c_ref, n_ref, m_ref):
    L = MLSTM_CHUNK
    nh = MLSTM_HEADS
    dh = MLSTM_HEAD_DIM

    @pl.when(pl.program_id(1) == 0)
    def _():
        c_ref[...] = jnp.zeros_like(c_ref)
        n_ref[...] = jnp.zeros_like(n_ref)
        m_ref[...] = jnp.zeros_like(m_ref)

    t_idx = lax.broadcasted_iota(jnp.int32, (L, L), 0)
    s_idx = lax.broadcasted_iota(jnp.int32, (L, L), 1)
    causal = s_idx <= t_idx
    for h in range(nh):
        sl = slice(h * dh, (h + 1) * dh)
        q = qm_ref[:, sl]
        k = km_ref[:, sl]
        v = vm_ref[:, sl]
        b_c = gc_ref[:, h:h + 1]
        li_c = gc_ref[:, nh + h:nh + h + 1]
        a_r = gr_ref[h:h + 1, :]
        b_last = b_c[L - 1:L, :]
        m_prev = m_ref[h][0:1, 0:1]
        c_prev = c_ref[h]
        n_prev = n_ref[h]

        log_d = jnp.where(causal, b_c + a_r, NEG_BIG)
        inter = b_c + m_prev
        m_t = jnp.maximum(inter, jnp.max(log_d, axis=1, keepdims=True))
        d = jnp.exp(log_d - m_t)
        s = lax.dot_general(q, k, _NT, preferred_element_type=f32) * d
        dec = jnp.exp(inter - m_t)
        qf = q.astype(f32)
        num = _dot(s.astype(bf16), v) + dec * _dot(q, c_prev.astype(bf16))
        den = (jnp.sum(s, axis=1, keepdims=True)
               + dec * jnp.sum(qf * n_prev, axis=1, keepdims=True))
        hh = num / jnp.maximum(jnp.abs(den), jnp.exp(-m_t))

        mu = jnp.mean(hh, axis=1, keepdims=True)
        cen = hh - mu
        var = jnp.mean(cen * cen, axis=1, keepdims=True)
        yn = cen * lax.rsqrt(var + LN_EPS) * nw_ref[:, sl]
        z = z_ref[:, sl].astype(f32)
        out = (yn + sk_ref[:, sl] * xc_ref[:, sl].astype(f32)) * (z * jax.nn.sigmoid(z))
        y_ref[:, sl] = out.astype(y_ref.dtype)

        log_w = b_last - b_c + li_c
        m_new = jnp.maximum(b_last + m_prev, jnp.max(log_w, axis=0, keepdims=True))
        w = jnp.exp(log_w - m_new)
        carry = jnp.exp(b_last + m_prev - m_new)
        kw = k.astype(f32) * w
        c_ref[h] = carry * c_prev + lax.dot_general(kw.astype(bf16), v, _TN,
                                                    preferred_element_type=f32)
        n_ref[h] = carry * n_prev + jnp.sum(kw, axis=0, keepdims=True)
        m_ref[h] = jnp.broadcast_to(m_new, m_ref.shape[1:])


def _mlstm(qm, km, vm, xc, pr, gc, gr, mh_norm, skip, batch, seq):
    T, W = qm.shape
    L = MLSTM_CHUNK
    nc = seq // L
    ng = 2 * MLSTM_HEADS
    z_col = 3
    tok = pl.BlockSpec((L, W), lambda b, c: (b * nc + c, 0))
    return pl.pallas_call(
        _mlstm_kernel,
        grid=(batch, nc),
        in_specs=[tok, tok, tok, tok,
                  pl.BlockSpec((L, W), lambda b, c: (b * nc + c, z_col)),
                  pl.BlockSpec((L, ng), lambda b, c: (b * nc + c, 0)),
                  pl.BlockSpec((ng, L), lambda b, c: (0, b * nc + c)),
                  pl.BlockSpec((1, W), lambda b, c: (0, 0)),
                  pl.BlockSpec((1, W), lambda b, c: (0, 0))],
        out_specs=tok,
        out_shape=jax.ShapeDtypeStruct((T, W), bf16),
        scratch_shapes=[pltpu.VMEM((MLSTM_HEADS, MLSTM_HEAD_DIM, MLSTM_HEAD_DIM), f32),
                        pltpu.VMEM((MLSTM_HEADS, 1, MLSTM_HEAD_DIM), f32),
                        pltpu.VMEM((MLSTM_HEADS, 8, 128), f32)],
        compiler_params=pltpu.CompilerParams(
            dimension_semantics=("arbitrary", "arbitrary"), vmem_limit_bytes=VMEM_LIMIT),
        name="mlstm",
    )(qm, km, vm, xc, pr, gc, gr, mh_norm, skip)


def _moba_kernel(rb_ref, qT_ref, k_ref, vT_ref, g_ref, bias_ref, o_ref, kmean_ref, sel_ref, *, nb, heads):
    blk_len = MOBA_BLOCK
    dh = ATT_HEAD_DIM
    own = pl.program_id(2)

    @pl.when(own == 0)
    def _():
        def mean_body(j, c):
            kb = k_ref[pl.ds(pl.multiple_of(j * blk_len, blk_len), blk_len), :].astype(f32)
            kmean_ref[pl.ds(j, 1), :] = jnp.sum(kb, axis=0, keepdims=True) * (1.0 / blk_len)
            return c
        lax.fori_loop(0, nb, mean_body, 0)

    blk = lax.broadcasted_iota(jnp.int32, (nb, blk_len), 0)
    past = blk < own
    for hh in range(heads):
        hs = slice(hh * dh, (hh + 1) * dh)
        qT = qT_ref[hs, :]
        km = kmean_ref[:, hs]
        km_hi = km.astype(bf16)
        km_lo = (km - km_hi.astype(f32)).astype(bf16)
        gate = _dot(km_hi, qT) + _dot(km_lo, qT)
        g = jnp.where(past, gate, -jnp.inf)
        sel = blk == own
        for _ in range(MOBA_TOPK):
            mx = jnp.max(g, axis=0, keepdims=True)
            first = jnp.min(jnp.where(g == mx, blk, nb), axis=0, keepdims=True)
            pick = blk == first
            sel = sel | (pick & past)
            g = jnp.where(pick, -jnp.inf, g)
        sel_ref[hh] = jnp.where(sel, 0.0, NEG_BIG)

    hsl = [slice(hh * dh, (hh + 1) * dh) for hh in range(heads)]
    head0 = pl.program_id(1) * heads
    far_bias = [rb_ref[REL_BUCKETS - 1, head0 + hh] * LOG2E for hh in range(heads)]
    ones_rows = jnp.ones((SUM_ROWS, blk_len), bf16)

    def kv_body(j, carry, near):
        rows = pl.ds(pl.multiple_of(j * blk_len, blk_len), blk_len)
        scores = [_dot(k_ref[rows, hs], qT_ref[hs, :]) for hs in hsl]
        stats = []
        for hh in range(heads):
            m = carry[hh][0]
            selrow = sel_ref[hh, pl.ds(j, 1), :]
            if near:
                s = scores[hh] + bias_ref[hh, own - j]
                m_new = jnp.maximum(m, jnp.max(s, axis=0, keepdims=True) + selrow)
                sub = m_new
            else:
                s = scores[hh]
                m_new = jnp.maximum(m, jnp.max(s, axis=0, keepdims=True) + (selrow + far_bias[hh]))
                sub = m_new - far_bias[hh]
            p = jnp.exp2(s - jnp.where(selrow < 0.0, -NEG_BIG, sub))
            stats.append((m_new, jnp.exp2(m - m_new), p.astype(bf16)))
        out = []
        for hh in range(heads):
            m_new, alpha, p = stats[hh]
            lhs = jnp.concatenate([vT_ref[j, hsl[hh], :], ones_rows], axis=0)
            out.append((m_new, alpha * carry[hh][1] + _dot(lhs, p)))
        return tuple(out)

    init = tuple((jnp.full((1, blk_len), NEG_BIG, f32), jnp.zeros((dh + SUM_ROWS, blk_len), f32))
                 for _ in range(heads))
    n_far = jnp.maximum(own - (NEAR_TILES - 1), 0)
    res = lax.fori_loop(0, n_far, functools.partial(kv_body, near=False), init)
    res = lax.fori_loop(n_far, own + 1, functools.partial(kv_body, near=True), res)
    for hh in range(heads):
        hs = hsl[hh]
        acc = res[hh][1]
        o = (acc[0:dh, :] / acc[dh:dh + 1, :]).T
        gg = g_ref[:, hs].astype(f32)
        o_ref[:, hs] = (o * (gg * jax.nn.sigmoid(gg))).astype(o_ref.dtype)


def _moba(rel_bias, qT, vT, pr, bias_tiles, batch, seq):
    T = pr.shape[0]
    nb = seq // MOBA_BLOCK
    G = MOBA_HEADS_PER_STEP
    gw = G * ATT_HEAD_DIM
    gate_col0 = ATT_WIDTH // gw
    vT4 = vT.reshape(batch, nb, ATT_WIDTH, MOBA_BLOCK)
    return pl.pallas_call(
        functools.partial(_moba_kernel, nb=nb, heads=G),
        grid=(batch, ATT_HEADS // G, nb),
        in_specs=[
            pl.BlockSpec(memory_space=pltpu.SMEM),
            pl.BlockSpec((None, gw, MOBA_BLOCK), lambda b, h, i: (b * nb + i, h, 0)),
            pl.BlockSpec((seq, gw), lambda b, h, i: (b, h)),
            pl.BlockSpec((None, nb, gw, MOBA_BLOCK), lambda b, h, i: (b, 0, h, 0)),
            pl.BlockSpec((MOBA_BLOCK, gw), lambda b, h, i: (b * nb + i, gate_col0 + h)),
            pl.BlockSpec((G, NEAR_TILES, MOBA_BLOCK, MOBA_BLOCK), lambda b, h, i: (h, 0, 0, 0)),
        ],
        out_specs=pl.BlockSpec((MOBA_BLOCK, gw), lambda b, h, i: (b * nb + i, h)),
        out_shape=jax.ShapeDtypeStruct((T, ATT_WIDTH), bf16),
        scratch_shapes=[pltpu.VMEM((nb, gw), f32), pltpu.VMEM((G, nb, MOBA_BLOCK), f32)],
        compiler_params=pltpu.CompilerParams(
            dimension_semantics=("arbitrary", "arbitrary", "arbitrary"),
            vmem_limit_bytes=VMEM_LIMIT),
        name="moba",
    )(rel_bias, qT, pr, vT4, pr, bias_tiles)


def _outproj_kernel(ya_ref, ym_ref, wa_ref, wm_ref, g_ref, x_ref, o_ref):
    y = _dot(ya_ref[...], wa_ref[...]) + _dot(ym_ref[...], wm_ref[...])
    ms = jnp.mean(y * y, axis=-1, keepdims=True)
    o_ref[...] = x_ref[...] + y * lax.rsqrt(ms + RMS_EPS) * g_ref[...]


def _outproj(ya, ym, wa, wm, g_post, x2):
    T, D = x2.shape
    tm = TOKEN_TILE
    resident = dict(pipeline_mode=pl.Buffered(1))
    return pl.pallas_call(
        _outproj_kernel,
        grid=(T // tm,),
        in_specs=[
            pl.BlockSpec((tm, ya.shape[1]), lambda i: (i, 0)),
            pl.BlockSpec((tm, ym.shape[1]), lambda i: (i, 0)),
            pl.BlockSpec(wa.shape, lambda i: (0, 0), **resident),
            pl.BlockSpec(wm.shape, lambda i: (0, 0), **resident),
            pl.BlockSpec((1, D), lambda i: (0, 0)),
            pl.BlockSpec((tm, D), lambda i: (i, 0)),
        ],
        out_specs=pl.BlockSpec((tm, D), lambda i: (i, 0)),
        out_shape=jax.ShapeDtypeStruct((T, D), f32),
        compiler_params=pltpu.CompilerParams(
            dimension_semantics=("arbitrary",), vmem_limit_bytes=VMEM_LIMIT),
        name="outproj",
    )(ya, ym, wa, wm, g_post, x2)


def _block_diag_256(w):
    per = 256 // QKV_BLOCK
    wg = w.reshape(-1, per, QKV_BLOCK, QKV_BLOCK)
    eye = jnp.eye(per, dtype=w.dtype)
    dense = jnp.einsum('gnio,nm->gnimo', wg, eye)
    return dense.reshape(-1, 256, 256).astype(bf16)


def _layer(x, rel_bias, g_pre, g_post, w_in, conv_w, conv_b, wq_m, wk_m, wv_m,
           w_if, b_if, mh_norm, skip, w_out):
    batch, seq, d_model = x.shape
    assert seq % TOKEN_TILE == 0 and TOKEN_TILE % MLSTM_CHUNK == 0 and TOKEN_TILE % MOBA_BLOCK == 0
    aw, mw = ATT_WIDTH, MLSTM_WIDTH
    x2 = x.reshape(batch * seq, d_model)

    w_bf = w_in.astype(bf16)
    wqT = w_bf[:, 0:aw].T
    wvT = w_bf[:, 2 * aw:3 * aw].T
    wr = jnp.concatenate([w_bf[:, aw:2 * aw], w_bf[:, 3 * aw:]], axis=1)
    wif = w_if.astype(bf16)
    w_out_bf = w_out.astype(bf16)

    bias_tiles = _bias_tiles(rel_bias)
    qT, vT, pr = _inproj(x2, g_pre.reshape(1, -1), wqT, wvT, wr)
    xc, qm, km, vm, gc, gr = _mlstm_prep(
        pr, conv_w, conv_b.reshape(1, -1), _block_diag_256(wq_m), _block_diag_256(wk_m),
        _block_diag_256(wv_m), wif, wif.T, b_if.reshape(1, -1), b_if.reshape(-1, 1), seq)
    ym = _mlstm(qm, km, vm, xc, pr, gc, gr, mh_norm.reshape(1, -1), skip.reshape(1, -1), batch, seq)
    ya = _moba(rel_bias, qT, vT, pr, bias_tiles, batch, seq)
    out = _outproj(ya, ym, w_out_bf[0:aw], w_out_bf[aw:], g_post.reshape(1, -1), x2)
    return out.reshape(batch, seq, d_model)


def kernel(x, rel_bias, g_pre, g_post, w_in, conv_w, conv_b, wq_m, wk_m, wv_m, w_if, b_if,
           mh_norm, skip, w_out):
    depth = w_in.shape[0]
    for l in range(depth):
        x = _layer(x, rel_bias, g_pre[l], g_post[l], w_in[l], conv_w[l], conv_b[l], wq_m[l],
                   wk_m[l], wv_m[l], w_if[l], b_if[l], mh_norm[l], skip[l], w_out[l])
    return x
```

```python
import functools
import math

import jax
import jax.numpy as jnp
import numpy as np
from jax import lax
from jax.experimental import pallas as pl
from jax.experimental.pallas import tpu as pltpu

f32 = jnp.float32
bf16 = jnp.bfloat16

ATT_HEADS = 8
ATT_HEAD_DIM = 128
ATT_WIDTH = ATT_HEADS * ATT_HEAD_DIM
MOBA_BLOCK = 256
MOBA_TOPK = 3
REL_BUCKETS = 32
REL_MAX_DIST = 2048
MLSTM_HEADS = 4
MLSTM_WIDTH = 1024
MLSTM_HEAD_DIM = MLSTM_WIDTH // MLSTM_HEADS
QKV_BLOCK = 4
CONV_WIDTH = 4
RMS_EPS = 1e-6
LN_EPS = 1e-5

MLSTM_CHUNK = 256
TOKEN_TILE = 512
MOBA_HEADS_PER_STEP = 4
HALO_ROWS = 16
SUM_ROWS = 16
NEG_BIG = -1e30
VMEM_LIMIT = 56 * 1024 * 1024

_NT = (((1,), (1,)), ((), ()))
_TN = (((0,), (0,)), ((), ()))


def _t5_thresholds():
    n = np.arange(0, 2 * REL_MAX_DIST, dtype=np.int64)
    max_exact = REL_BUCKETS // 2
    nf = np.maximum(n, 1).astype(np.float32)
    large = max_exact + (np.log(nf / np.float32(max_exact))
                         / np.float32(math.log(REL_MAX_DIST / max_exact))
                         * np.float32(REL_BUCKETS - max_exact)).astype(np.int32)
    large = np.minimum(large, REL_BUCKETS - 1)
    bucket = np.where(n < max_exact, n, large)
    assert np.all(np.diff(bucket) >= 0)
    return [int(np.argmax(bucket >= k)) for k in range(1, REL_BUCKETS)]


T5_THR = _t5_thresholds()
NEAR_TILES = -(-(T5_THR[-1] + MOBA_BLOCK - 1) // MOBA_BLOCK)
assert NEAR_TILES * MOBA_BLOCK - (MOBA_BLOCK - 1) >= T5_THR[-1]
BIAS_TILES = NEAR_TILES + 1
LOG2E = math.log2(math.e)


def _dot(a, b):
    return jnp.dot(a, b, preferred_element_type=f32)


def _split3(x):
    hi = x.astype(bf16)
    r = x - hi.astype(f32)
    mid = r.astype(bf16)
    lo = (r - mid.astype(f32)).astype(bf16)
    return hi, mid, lo


def _bias_kernel(rb_ref, out_ref):
    h = pl.program_id(0)
    key = lax.broadcasted_iota(jnp.int32, (MOBA_BLOCK, MOBA_BLOCK), 0)
    qry = lax.broadcasted_iota(jnp.int32, (MOBA_BLOCK, MOBA_BLOCK), 1)
    base = qry - key
    for d in range(BIAS_TILES):
        dist = base + d * MOBA_BLOCK
        n = jnp.maximum(dist, 0)
        val = jnp.full((MOBA_BLOCK, MOBA_BLOCK), rb_ref[REL_BUCKETS - 1, h] * LOG2E, f32)
        for k in range(REL_BUCKETS - 2, -1, -1):
            val = jnp.where(n < T5_THR[k], rb_ref[k, h] * LOG2E, val)
        if d == 0:
            val = jnp.where(dist >= 0, val, NEG_BIG)
        out_ref[d] = val


def _bias_tiles(rel_bias):
    return pl.pallas_call(
        _bias_kernel,
        grid=(ATT_HEADS,),
        in_specs=[pl.BlockSpec(memory_space=pltpu.SMEM)],
        out_specs=pl.BlockSpec((None, BIAS_TILES, MOBA_BLOCK, MOBA_BLOCK), lambda h: (h, 0, 0, 0)),
        out_shape=jax.ShapeDtypeStruct((ATT_HEADS, BIAS_TILES, MOBA_BLOCK, MOBA_BLOCK), f32),
        name="bias_tiles",
    )(rel_bias)


def _inproj_kernel(x_ref, g_ref, wqT_ref, wvT_ref, wr_ref, qT_ref, vT_ref, pr_ref, *, tm, q_scale):
    x = x_ref[...]
    ms = jnp.mean(x * x, axis=-1, keepdims=True)
    h = (x * lax.rsqrt(ms + RMS_EPS) * g_ref[...]).astype(bf16)
    for cc in range(ATT_WIDTH // 256):
        rows = slice(cc * 256, (cc + 1) * 256)
        qt = lax.dot_general(wqT_ref[rows, :], h, _NT, preferred_element_type=f32) * q_scale
        vt = lax.dot_general(wvT_ref[rows, :], h, _NT, preferred_element_type=f32)
        for u in range(tm // MOBA_BLOCK):
            cols = slice(u * MOBA_BLOCK, (u + 1) * MOBA_BLOCK)
            qT_ref[u, rows, :] = qt[:, cols].astype(bf16)
            vT_ref[u, rows, :] = vt[:, cols].astype(bf16)
    ncols = pr_ref.shape[1]
    for cc in range(ncols // 512):
        cols = slice(cc * 512, (cc + 1) * 512)
        pr_ref[:, cols] = _dot(h, wr_ref[:, cols]).astype(bf16)


def _inproj(x2, g_pre, wqT, wvT, wr):
    T, D = x2.shape
    tm = TOKEN_TILE
    nblk = tm // MOBA_BLOCK
    ncols = wr.shape[1]
    resident = dict(pipeline_mode=pl.Buffered(1))
    return pl.pallas_call(
        functools.partial(_inproj_kernel, tm=tm, q_scale=ATT_HEAD_DIM ** -0.5 * LOG2E),
        grid=(T // tm,),
        in_specs=[
            pl.BlockSpec((tm, D), lambda i: (i, 0)),
            pl.BlockSpec((1, D), lambda i: (0, 0)),
            pl.BlockSpec((ATT_WIDTH, D), lambda i: (0, 0), **resident),
            pl.BlockSpec((ATT_WIDTH, D), lambda i: (0, 0), **resident),
            pl.BlockSpec((D, ncols), lambda i: (0, 0), **resident),
        ],
        out_specs=[
            pl.BlockSpec((nblk, ATT_WIDTH, MOBA_BLOCK), lambda i: (i, 0, 0)),
            pl.BlockSpec((nblk, ATT_WIDTH, MOBA_BLOCK), lambda i: (i, 0, 0)),
            pl.BlockSpec((tm, ncols), lambda i: (i, 0)),
        ],
        out_shape=[
            jax.ShapeDtypeStruct((T // MOBA_BLOCK, ATT_WIDTH, MOBA_BLOCK), bf16),
            jax.ShapeDtypeStruct((T // MOBA_BLOCK, ATT_WIDTH, MOBA_BLOCK), bf16),
            jax.ShapeDtypeStruct((T, ncols), bf16),
        ],
        compiler_params=pltpu.CompilerParams(
            dimension_semantics=("arbitrary",), vmem_limit_bytes=VMEM_LIMIT),
        name="inproj",
    )(x2, g_pre, wqT, wvT, wr)


def _log_sigmoid(v):
    return jnp.minimum(v, 0.0) - jnp.log1p(jnp.exp(-jnp.abs(v)))


def _prep_kernel(xm_ref, halo_ref, cw_ref, cb_ref, wq_ref, wk_ref, wv_ref, wif_ref, wifT_ref,
                 bif_ref, bifT_ref, xc_ref, qm_ref, km_ref, vm_ref, gc_ref, gr_ref, xpad_ref,
                 *, tm, seq, k_scale):
    i = pl.program_id(0)
    W = MLSTM_WIDTH
    nh = MLSTM_HEADS
    L = MLSTM_CHUNK
    xm_bf = xm_ref[...]
    xm = xm_bf.astype(f32)
    at_seq_start = (i * tm) % seq == 0
    xpad_ref[0:HALO_ROWS, :] = jnp.where(at_seq_start, 0.0, halo_ref[...].astype(f32))
    xpad_ref[HALO_ROWS:HALO_ROWS + tm, :] = xm
    acc = cb_ref[...] + cw_ref[CONV_WIDTH - 1:CONV_WIDTH, :] * xm
    for j in range(CONV_WIDTH - 1):
        off = HALO_ROWS - (CONV_WIDTH - 1) + j
        acc = acc + cw_ref[j:j + 1, :] * xpad_ref[off:off + tm, :]
    xc_bf = (acc * jax.nn.sigmoid(acc)).astype(bf16)
    xc_ref[...] = xc_bf

    nd = W // 256
    for g in range(nd):
        sl = slice(g * 256, (g + 1) * 256)
        qm_ref[:, sl] = _dot(xc_bf[:, sl], wq_ref[g]).astype(bf16)
        km_ref[:, sl] = (_dot(xc_bf[:, sl], wk_ref[g]) * k_scale).astype(bf16)
        vm_ref[:, sl] = _dot(xm_bf[:, sl], wv_ref[g]).astype(bf16)
    qm = qm_ref[...]
    km = km_ref[...]
    vm = vm_ref[...]
    k_unscale = 1.0 / k_scale
    gates_c = (_dot(qm, wif_ref[0:W, :]) + k_unscale * _dot(km, wif_ref[W:2 * W, :])
               + _dot(vm, wif_ref[2 * W:3 * W, :]) + bif_ref[...])
    gates_r = (lax.dot_general(wifT_ref[:, 0:W], qm, _NT, preferred_element_type=f32)
               + k_unscale * lax.dot_general(wifT_ref[:, W:2 * W], km, _NT, preferred_element_type=f32)
               + lax.dot_general(wifT_ref[:, 2 * W:3 * W], vm, _NT, preferred_element_type=f32)
               + bifT_ref[...])
    li_c = gates_c[:, 0:nh]
    lf_c = _log_sigmoid(gates_c[:, nh:2 * nh])
    li_r = gates_r[0:nh, :]
    lf_r = _log_sigmoid(gates_r[nh:2 * nh, :])

    row = lax.broadcasted_iota(jnp.int32, (L, L), 0)
    col = lax.broadcasted_iota(jnp.int32, (L, L), 1)
    lower = jnp.where(col <= row, 1.0, 0.0).astype(bf16)
    upper = jnp.where(row <= col, 1.0, 0.0).astype(bf16)
    gc_ref[:, nh:2 * nh] = li_c
    for u in range(tm // L):
        rs = slice(u * L, (u + 1) * L)
        b_c = sum(_dot(lower, part) for part in _split3(lf_c[rs, :]))
        b_r = sum(_dot(part, upper) for part in _split3(lf_r[:, rs]))
        gc_ref[rs, 0:nh] = b_c
        gr_ref[0:nh, rs] = li_r[:, rs] - b_r
        gr_ref[nh:2 * nh, rs] = b_r


def _mlstm_prep(pr, conv_w, conv_b, wq_bd, wk_bd, wv_bd, wif, wifT, bif, bifT, seq):
    T = pr.shape[0]
    W = MLSTM_WIDTH
    tm = TOKEN_TILE
    xm_col = 2
    halo_per_tile = tm // HALO_ROWS
    ng = 2 * MLSTM_HEADS
    resident = dict(pipeline_mode=pl.Buffered(1))
    const2 = lambda i: (0, 0)
    const3 = lambda i: (0, 0, 0)
    tok = pl.BlockSpec((tm, W), lambda i: (i, 0))
    return pl.pallas_call(
        functools.partial(_prep_kernel, tm=tm, seq=seq, k_scale=MLSTM_HEAD_DIM ** -0.5),
        grid=(T // tm,),
        in_specs=[
            pl.BlockSpec((tm, W), lambda i: (i, xm_col)),
            pl.BlockSpec((HALO_ROWS, W), lambda i: (jnp.maximum(i * halo_per_tile - 1, 0), xm_col)),
            pl.BlockSpec((CONV_WIDTH, W), const2),
            pl.BlockSpec((1, W), const2),
            pl.BlockSpec((W // 256, 256, 256), const3, **resident),
            pl.BlockSpec((W // 256, 256, 256), const3, **resident),
            pl.BlockSpec((W // 256, 256, 256), const3, **resident),
            pl.BlockSpec((3 * W, ng), const2, **resident),
            pl.BlockSpec((ng, 3 * W), const2, **resident),
            pl.BlockSpec((1, ng), const2),
            pl.BlockSpec((ng, 1), const2),
        ],
        out_specs=[tok, tok, tok, tok,
                   pl.BlockSpec((tm, ng), lambda i: (i, 0)),
                   pl.BlockSpec((ng, tm), lambda i: (0, i))],
        out_shape=[jax.ShapeDtypeStruct((T, W), bf16)] * 4
                  + [jax.ShapeDtypeStruct((T, ng), f32), jax.ShapeDtypeStruct((ng, T), f32)],
        scratch_shapes=[pltpu.VMEM((HALO_ROWS + tm, W), f32)],
        compiler_params=pltpu.CompilerParams(
            dimension_semantics=("arbitrary",), vmem_limit_bytes=VMEM_LIMIT),
        name="mlstm_prep",
    )(pr, pr, conv_w, conv_b, wq_bd, wk_bd, wv_bd, wif, wifT, bif, bifT)


def _mlstm_kernel(qm_ref, km_ref, vm_ref, xc_ref, z_ref, gc_ref, gr_ref, nw_ref, sk_ref,
                  y_ref, c_ref, n_ref, m_ref):
    L = MLSTM_CHUNK
    nh = MLSTM_HEADS
    dh = MLSTM_HEAD_DIM

    @pl.when(pl.program_id(1) == 0)
    def _():
        c_ref[...] = jnp.zeros_like(c_ref)
        n_ref[...] = jnp.zeros_like(n_ref)
        m_ref[...] = jnp.zeros_like(m_ref)

    t_idx = lax.broadcasted_iota(jnp.int32, (L, L), 0)
    s_idx = lax.broadcasted_iota(jnp.int32, (L, L), 1)
    causal = s_idx <= t_idx
    for h in range(nh):
        sl = slice(h * dh, (h + 1) * dh)
        q = qm_ref[:, sl]
        k = km_ref[:, sl]
        v = vm_ref[:, sl]
        b_c = gc_ref[:, h:h + 1]
        li_c = gc_ref[:, nh + h:nh + h + 1]
        a_r = gr_ref[h:h + 1, :]
        b_last = b_c[L - 1:L, :]
        m_prev = m_ref[h][0:1, 0:1]
        c_prev = c_ref[h]
        n_prev = n_ref[h]

        log_d = jnp.where(causal, b_c + a_r, NEG_BIG)
        inter = b_c + m_prev
        m_t = jnp.maximum(inter, jnp.max(log_d, axis=1, keepdims=True))
        d = jnp.exp(log_d - m_t)
        s = lax.dot_general(q, k, _NT, preferred_element_type=f32) * d
        dec = jnp.exp(inter - m_t)
        qf = q.astype(f32)
        num = _dot(s.astype(bf16), v) + dec * _dot(q, c_prev.astype(bf16))
        den = (jnp.sum(s, axis=1, keepdims=True)
               + dec * jnp.sum(qf * n_prev, axis=1, keepdims=True))
        hh = num / jnp.maximum(jnp.abs(den), jnp.exp(-m_t))

        mu = jnp.mean(hh, axis=1, keepdims=True)
        cen = hh - mu
        var = jnp.mean(cen * cen, axis=1, keepdims=True)
        yn = cen * lax.rsqrt(var + LN_EPS) * nw_ref[:, sl]
        z = z_ref[:, sl].astype(f32)
        out = (yn + sk_ref[:, sl] * xc_ref[:, sl].astype(f32)) * (z * jax.nn.sigmoid(z))
        y_ref[:, sl] = out.astype(y_ref.dtype)

        log_w = b_last - b_c + li_c
        m_new = jnp.maximum(b_last + m_prev, jnp.max(log_w, axis=0, keepdims=True))
        w = jnp.exp(log_w - m_new)
        carry = jnp.exp(b_last + m_prev - m_new)
        kw = k.astype(f32) * w
        c_ref[h] = carry * c_prev + lax.dot_general(kw.astype(bf16), v, _TN,
                                                    preferred_element_type=f32)
        n_ref[h] = carry * n_prev + jnp.sum(kw, axis=0, keepdims=True)
        m_ref[h] = jnp.broadcast_to(m_new, m_ref.shape[1:])


def _mlstm(qm, km, vm, xc, pr, gc, gr, mh_norm, skip, batch, seq):
    T, W = qm.shape
    L = MLSTM_CHUNK
    nc = seq // L
    ng = 2 * MLSTM_HEADS
    z_col = 3
    tok = pl.BlockSpec((L, W), lambda b, c: (b * nc + c, 0))
    return pl.pallas_call(
        _mlstm_kernel,
        grid=(batch, nc),
        in_specs=[tok, tok, tok, tok,
                  pl.BlockSpec((L, W), lambda b, c: (b * nc + c, z_col)),
                  pl.BlockSpec((L, ng), lambda b, c: (b * nc + c, 0)),
                  pl.BlockSpec((ng, L), lambda b, c: (0, b * nc + c)),
                  pl.BlockSpec((1, W), lambda b, c: (0, 0)),
                  pl.BlockSpec((1, W), lambda b, c: (0, 0))],
        out_specs=tok,
        out_shape=jax.ShapeDtypeStruct((T, W), bf16),
        scratch_shapes=[pltpu.VMEM((MLSTM_HEADS, MLSTM_HEAD_DIM, MLSTM_HEAD_DIM), f32),
                        pltpu.VMEM((MLSTM_HEADS, 1, MLSTM_HEAD_DIM), f32),
                        pltpu.VMEM((MLSTM_HEADS, 8, 128), f32)],
        compiler_params=pltpu.CompilerParams(
            dimension_semantics=("arbitrary", "arbitrary"), vmem_limit_bytes=VMEM_LIMIT),
        name="mlstm",
    )(qm, km, vm, xc, pr, gc, gr, mh_norm, skip)


def _moba_kernel(qT_ref, k_ref, vT_ref, g_ref, bias_ref, o_ref, kmean_ref, sel_ref, sa_ref, sb_ref,
                 acc_ref, m_ref, *, nb, heads):
    blk_len = MOBA_BLOCK
    dh = ATT_HEAD_DIM
    own = pl.program_id(2)

    @pl.when(own == 0)
    def _():
        def mean_body(j, c):
            kb = k_ref[pl.ds(pl.multiple_of(j * blk_len, blk_len), blk_len), :].astype(f32)
            kmean_ref[pl.ds(j, 1), :] = jnp.sum(kb, axis=0, keepdims=True) * (1.0 / blk_len)
            return c
        lax.fori_loop(0, nb, mean_body, 0)

    blk = lax.broadcasted_iota(jnp.int32, (nb, blk_len), 0)
    past = blk < own
    for hh in range(heads):
        hs = slice(hh * dh, (hh + 1) * dh)
        qT = qT_ref[hs, :]
        km = kmean_ref[:, hs]
        km_hi = km.astype(bf16)
        km_lo = (km - km_hi.astype(f32)).astype(bf16)
        gate = _dot(km_hi, qT) + _dot(km_lo, qT)
        g = jnp.where(past, gate, -jnp.inf)
        sel = blk == own
        for _ in range(MOBA_TOPK):
            mx = jnp.max(g, axis=0, keepdims=True)
            first = jnp.min(jnp.where(g == mx, blk, nb), axis=0, keepdims=True)
            pick = blk == first
            sel = sel | (pick & past)
            g = jnp.where(pick, -jnp.inf, g)
        sel_ref[hh] = jnp.where(sel, 0.0, NEG_BIG)

    hsl = [slice(hh * dh, (hh + 1) * dh) for hh in range(heads)]
    ones_rows = jnp.ones((SUM_ROWS, blk_len), bf16)
    m_ref[...] = jnp.full(m_ref.shape, NEG_BIG, f32)
    acc_ref[...] = jnp.zeros(acc_ref.shape, f32)

    def score_matmuls(j, dst_ref):
        jc = jnp.minimum(j, own)
        rows = pl.ds(pl.multiple_of(jc * blk_len, blk_len), blk_len)
        tile = jnp.minimum(own - jc, NEAR_TILES)
        for hh in range(heads):
            dst_ref[hh] = _dot(k_ref[rows, hsl[hh]], qT_ref[hsl[hh], :]) + bias_ref[hh, tile]

    def softmax_update(j, src_ref):
        for hh in range(heads):
            m = m_ref[hh]
            selrow = sel_ref[hh, pl.ds(j, 1), :]
            m_new = jnp.maximum(m, jnp.max(src_ref[hh], axis=0, keepdims=True) + selrow)
            p = jnp.exp2(src_ref[hh] - jnp.where(selrow < 0.0, -NEG_BIG, m_new)).astype(bf16)
            lhs = jnp.concatenate([vT_ref[j, hsl[hh], :], ones_rows], axis=0)
            acc_ref[hh] = jnp.exp2(m - m_new) * acc_ref[hh] + _dot(lhs, p)
            m_ref[hh] = m_new

    score_matmuls(0, sa_ref)

    def pair_body(jj, c):
        j = 2 * jj
        score_matmuls(j + 1, sb_ref)
        softmax_update(j, sa_ref)
        score_matmuls(j + 2, sa_ref)
        softmax_update(j + 1, sb_ref)
        return c

    lax.fori_loop(0, (own + 2) // 2, pair_body, 0)
    for hh in range(heads):
        hs = hsl[hh]
        acc = acc_ref[hh]
        o = (acc[0:dh, :] / acc[dh:dh + 1, :]).T
        gg = g_ref[:, hs].astype(f32)
        o_ref[:, hs] = (o * (gg * jax.nn.sigmoid(gg))).astype(o_ref.dtype)


def _moba(qT, vT, pr, bias_tiles, batch, seq):
    T = pr.shape[0]
    nb = seq // MOBA_BLOCK
    G = MOBA_HEADS_PER_STEP
    gw = G * ATT_HEAD_DIM
    gate_col0 = ATT_WIDTH // gw
    vT4 = vT.reshape(batch, nb, ATT_WIDTH, MOBA_BLOCK)
    return pl.pallas_call(
        functools.partial(_moba_kernel, nb=nb, heads=G),
        grid=(batch, ATT_HEADS // G, nb),
        in_specs=[
            pl.BlockSpec((None, gw, MOBA_BLOCK), lambda b, h, i: (b * nb + i, h, 0)),
            pl.BlockSpec((seq, gw), lambda b, h, i: (b, h)),
            pl.BlockSpec((None, nb, gw, MOBA_BLOCK), lambda b, h, i: (b, 0, h, 0)),
            pl.BlockSpec((MOBA_BLOCK, gw), lambda b, h, i: (b * nb + i, gate_col0 + h)),
            pl.BlockSpec((G, BIAS_TILES, MOBA_BLOCK, MOBA_BLOCK), lambda b, h, i: (h, 0, 0, 0),
                         pipeline_mode=pl.Buffered(1)),
        ],
        out_specs=pl.BlockSpec((MOBA_BLOCK, gw), lambda b, h, i: (b * nb + i, h)),
        out_shape=jax.ShapeDtypeStruct((T, ATT_WIDTH), bf16),
        scratch_shapes=[pltpu.VMEM((nb, gw), f32),
                        pltpu.VMEM((G, nb, MOBA_BLOCK), f32),
                        pltpu.VMEM((G, MOBA_BLOCK, MOBA_BLOCK), f32),
                        pltpu.VMEM((G, MOBA_BLOCK, MOBA_BLOCK), f32),
                        pltpu.VMEM((G, ATT_HEAD_DIM + SUM_ROWS, MOBA_BLOCK), f32),
                        pltpu.VMEM((G, 1, MOBA_BLOCK), f32)],
        compiler_params=pltpu.CompilerParams(
            dimension_semantics=("arbitrary", "arbitrary", "arbitrary"),
            vmem_limit_bytes=VMEM_LIMIT),
        name="moba",
    )(qT, pr, vT4, pr, bias_tiles)


def _outproj_kernel(ya_ref, ym_ref, wa_ref, wm_ref, g_ref, x_ref, o_ref):
    y = _dot(ya_ref[...], wa_ref[...]) + _dot(ym_ref[...], wm_ref[...])
    ms = jnp.mean(y * y, axis=-1, keepdims=True)
    o_ref[...] = x_ref[...] + y * lax.rsqrt(ms + RMS_EPS) * g_ref[...]


def _outproj(ya, ym, wa, wm, g_post, x2):
    T, D = x2.shape
    tm = TOKEN_TILE
    resident = dict(pipeline_mode=pl.Buffered(1))
    return pl.pallas_call(
        _outproj_kernel,
        grid=(T // tm,),
        in_specs=[
            pl.BlockSpec((tm, ya.shape[1]), lambda i: (i, 0)),
            pl.BlockSpec((tm, ym.shape[1]), lambda i: (i, 0)),
            pl.BlockSpec(wa.shape, lambda i: (0, 0), **resident),
            pl.BlockSpec(wm.shape, lambda i: (0, 0), **resident),
            pl.BlockSpec((1, D), lambda i: (0, 0)),
            pl.BlockSpec((tm, D), lambda i: (i, 0)),
        ],
        out_specs=pl.BlockSpec((tm, D), lambda i: (i, 0)),
        out_shape=jax.ShapeDtypeStruct((T, D), f32),
        compiler_params=pltpu.CompilerParams(
            dimension_semantics=("arbitrary",), vmem_limit_bytes=VMEM_LIMIT),
        name="outproj",
    )(ya, ym, wa, wm, g_post, x2)


def _block_diag_256(w):
    width = w.shape[0] * QKV_BLOCK
    rows = w.reshape(width, QKV_BLOCK)
    col = np.arange(256)
    spread = jnp.asarray(col[None, :] % QKV_BLOCK == np.arange(QKV_BLOCK)[:, None], w.dtype)
    tiled = jnp.dot(rows, spread, precision=lax.Precision.HIGHEST)
    same_block = (np.arange(width)[:, None] % 256) // QKV_BLOCK == col[None, :] // QKV_BLOCK
    dense = jnp.where(jnp.asarray(same_block), tiled, 0.0)
    return dense.reshape(-1, 256, 256).astype(bf16)


def _layer(x, rel_bias, g_pre, g_post, w_in, conv_w, conv_b, wq_m, wk_m, wv_m,
           w_if, b_if, mh_norm, skip, w_out):
    batch, seq, d_model = x.shape
    assert seq % TOKEN_TILE == 0 and TOKEN_TILE % MLSTM_CHUNK == 0 and TOKEN_TILE % MOBA_BLOCK == 0
    aw, mw = ATT_WIDTH, MLSTM_WIDTH
    x2 = x.reshape(batch * seq, d_model)

    w_bf = w_in.astype(bf16)
    wqT = w_bf[:, 0:aw].T
    wvT = w_bf[:, 2 * aw:3 * aw].T
    wr = jnp.concatenate([w_bf[:, aw:2 * aw], w_bf[:, 3 * aw:]], axis=1)
    wif = w_if.astype(bf16)
    w_out_bf = w_out.astype(bf16)

    bias_tiles = _bias_tiles(rel_bias)
    qT, vT, pr = _inproj(x2, g_pre.reshape(1, -1), wqT, wvT, wr)
    xc, qm, km, vm, gc, gr = _mlstm_prep(
        pr, conv_w, conv_b.reshape(1, -1), _block_diag_256(wq_m), _block_diag_256(wk_m),
        _block_diag_256(wv_m), wif, wif.T, b_if.reshape(1, -1), b_if.reshape(-1, 1), seq)
    ym = _mlstm(qm, km, vm, xc, pr, gc, gr, mh_norm.reshape(1, -1), skip.reshape(1, -1), batch, seq)
    ya = _moba(qT, vT, pr, bias_tiles, batch, seq)
    out = _outproj(ya, ym, w_out_bf[0:aw], w_out_bf[aw:], g_post.reshape(1, -1), x2)
    return out.reshape(batch, seq, d_model)


def kernel(x, rel_bias, g_pre, g_post, w_in, conv_w, conv_b, wq_m, wk_m, wv_m, w_if, b_if,
           mh_norm, skip, w_out):
    depth = w_in.shape[0]
    for l in range(depth):
        x = _layer(x, rel_bias, g_pre[l], g_post[l], w_in[l], conv_w[l], conv_b[l], wq_m[l],
                   wk_m[l], wv_m[l], w_if[l], b_if[l], mh_norm[l], skip[l], w_out[l])
    return x
```

```python
import functools
import math

import jax
import jax.numpy as jnp
import numpy as np
from jax import lax
from jax.experimental import pallas as pl
from jax.experimental.pallas import tpu as pltpu

f32 = jnp.float32
bf16 = jnp.bfloat16

ATT_HEADS = 8
ATT_HEAD_DIM = 128
ATT_WIDTH = ATT_HEADS * ATT_HEAD_DIM
MOBA_BLOCK = 256
MOBA_TOPK = 3
REL_BUCKETS = 32
REL_MAX_DIST = 2048
MLSTM_HEADS = 4
MLSTM_WIDTH = 1024
MLSTM_HEAD_DIM = MLSTM_WIDTH // MLSTM_HEADS
QKV_BLOCK = 4
CONV_WIDTH = 4
RMS_EPS = 1e-6
LN_EPS = 1e-5

MLSTM_CHUNK = 256
TOKEN_TILE = 512
MOBA_HEADS_PER_STEP = 4
HALO_ROWS = 16
SUM_ROWS = 16
GATE_ROWS = 8
LANES = 128
NEG_BIG = -1e30
VMEM_LIMIT = 56 * 1024 * 1024
LOG2E = math.log2(math.e)

_NT = (((1,), (1,)), ((), ()))


def _t5_thresholds():
    n = np.arange(0, 2 * REL_MAX_DIST, dtype=np.int64)
    max_exact = REL_BUCKETS // 2
    nf = np.maximum(n, 1).astype(np.float32)
    large = max_exact + (np.log(nf / np.float32(max_exact))
                         / np.float32(math.log(REL_MAX_DIST / max_exact))
                         * np.float32(REL_BUCKETS - max_exact)).astype(np.int32)
    large = np.minimum(large, REL_BUCKETS - 1)
    bucket = np.where(n < max_exact, n, large)
    assert np.all(np.diff(bucket) >= 0)
    return [int(np.argmax(bucket >= k)) for k in range(1, REL_BUCKETS)]


T5_THR = _t5_thresholds()
NEAR_TILES = -(-(T5_THR[-1] + MOBA_BLOCK - 1) // MOBA_BLOCK)
assert NEAR_TILES * MOBA_BLOCK - (MOBA_BLOCK - 1) >= T5_THR[-1]
BIAS_TILES = NEAR_TILES + 1


def _dot(a, b):
    return jnp.dot(a, b, preferred_element_type=f32)


def _dot_nt(a, b):
    return lax.dot_general(a, b, _NT, preferred_element_type=f32)


def _split3(x):
    hi = x.astype(bf16)
    r = x - hi.astype(f32)
    mid = r.astype(bf16)
    lo = (r - mid.astype(f32)).astype(bf16)
    return hi, mid, lo


def _bias_kernel(rb_ref, out_ref):
    h = pl.program_id(0)
    key = lax.broadcasted_iota(jnp.int32, (MOBA_BLOCK, MOBA_BLOCK), 0)
    qry = lax.broadcasted_iota(jnp.int32, (MOBA_BLOCK, MOBA_BLOCK), 1)
    base = qry - key
    for d in range(BIAS_TILES):
        dist = base + d * MOBA_BLOCK
        n = jnp.maximum(dist, 0)
        val = jnp.full((MOBA_BLOCK, MOBA_BLOCK), rb_ref[REL_BUCKETS - 1, h] * LOG2E, f32)
        for k in range(REL_BUCKETS - 2, -1, -1):
            val = jnp.where(n < T5_THR[k], rb_ref[k, h] * LOG2E, val)
        if d == 0:
            val = jnp.where(dist >= 0, val, NEG_BIG)
        out_ref[d] = val


def _bias_tiles(rel_bias):
    return pl.pallas_call(
        _bias_kernel,
        grid=(ATT_HEADS,),
        in_specs=[pl.BlockSpec(memory_space=pltpu.SMEM)],
        out_specs=pl.BlockSpec((None, BIAS_TILES, MOBA_BLOCK, MOBA_BLOCK), lambda h: (h, 0, 0, 0)),
        out_shape=jax.ShapeDtypeStruct((ATT_HEADS, BIAS_TILES, MOBA_BLOCK, MOBA_BLOCK), f32),
        name="bias_tiles",
    )(rel_bias)


def _log_sigmoid(v):
    return jnp.minimum(v, 0.0) - jnp.log1p(jnp.exp(-jnp.abs(v)))


def _inproj_kernel(x_ref, g_ref, wqT_ref, wvT_ref, wr_ref, wx_ref, cw_ref, cb_ref, wmqT_ref, wmk_ref,
                   wmvT_ref, wi_ref, wf_ref, bi_ref, bf_ref,
                   qT_ref, vT_ref, pr_ref, xc_ref, mqT_ref, mk_ref, mvT_ref, ga_ref, gb_ref, gl_ref,
                   xpad_ref, *, tm, seq, q_scale, k_scale):
    i = pl.program_id(0)
    W = MLSTM_WIDTH
    L = MLSTM_CHUNK
    x = x_ref[...]
    ms = jnp.mean(x * x, axis=-1, keepdims=True)
    h = (x * lax.rsqrt(ms + RMS_EPS) * g_ref[...]).astype(bf16)

    @pl.when((i * tm) % seq == 0)
    def _():
        xpad_ref[0:HALO_ROWS, :] = jnp.zeros((HALO_ROWS, W), f32)

    for cc in range(W // 512):
        cols = slice(cc * 512, (cc + 1) * 512)
        xpad_ref[HALO_ROWS:HALO_ROWS + tm, cols] = _dot(h, wx_ref[:, cols])

    def att_qv(cc):
        rows = slice(cc * 256, (cc + 1) * 256)
        qt = _dot_nt(wqT_ref[rows, :], h) * q_scale
        vt = _dot_nt(wvT_ref[rows, :], h)
        for u in range(tm // MOBA_BLOCK):
            cols = slice(u * MOBA_BLOCK, (u + 1) * MOBA_BLOCK)
            qT_ref[u, rows, :] = qt[:, cols].astype(bf16)
            vT_ref[u, rows, :] = vt[:, cols].astype(bf16)

    def token_major(cc):
        cols = slice(cc * 512, (cc + 1) * 512)
        pr_ref[:, cols] = _dot(h, wr_ref[:, cols]).astype(bf16)

    def conv_silu(g):
        sl = slice(g * 256, (g + 1) * 256)
        acc = cb_ref[:, sl] + cw_ref[CONV_WIDTH - 1:CONV_WIDTH, sl] * xpad_ref[HALO_ROWS:HALO_ROWS + tm, sl]
        for j in range(CONV_WIDTH - 1):
            off = HALO_ROWS - (CONV_WIDTH - 1) + j
            acc = acc + cw_ref[j:j + 1, sl] * xpad_ref[off:off + tm, sl]
        xc_ref[:, sl] = (acc * jax.nn.sigmoid(acc)).astype(bf16)
        xpad_ref[0:HALO_ROWS, sl] = xpad_ref[tm:tm + HALO_ROWS, sl]

    k_unscale = 1.0 / k_scale
    gates = [jnp.zeros((GATE_ROWS, tm), f32) + bi_ref[...], jnp.zeros((GATE_ROWS, tm), f32) + bf_ref[...]]

    def block_diag_qkv(g):
        sl = slice(g * 256, (g + 1) * 256)
        xc_g = xc_ref[:, sl]
        xm_g = xpad_ref[HALO_ROWS:HALO_ROWS + tm, sl].astype(bf16)
        qT = _dot_nt(wmqT_ref[g], xc_g).astype(bf16)
        vT = _dot_nt(wmvT_ref[g], xm_g).astype(bf16)
        kk = (_dot(xc_g, wmk_ref[g]) * k_scale).astype(bf16)
        mk_ref[:, sl] = kk
        for u in range(tm // L):
            cols = slice(u * L, (u + 1) * L)
            mqT_ref[u, sl, :] = qT[:, cols]
            mvT_ref[u, sl, :] = vT[:, cols]
        ks = slice(W + g * 256, W + (g + 1) * 256)
        vs = slice(2 * W + g * 256, 2 * W + (g + 1) * 256)
        for n, w_ref in enumerate((wi_ref, wf_ref)):
            gates[n] = (gates[n] + _dot(w_ref[:, sl], qT) + k_unscale * _dot_nt(w_ref[:, ks], kk)
                        + _dot(w_ref[:, vs], vT))

    def gate_outputs():
        li = gates[0] * LOG2E
        lf = _log_sigmoid(gates[1]) * LOG2E
        row = lax.broadcasted_iota(jnp.int32, (L, L), 0)
        col = lax.broadcasted_iota(jnp.int32, (L, L), 1)
        upper = jnp.where(row <= col, 1.0, 0.0).astype(bf16)
        gl_ref[...] = li
        for u in range(tm // L):
            cols = slice(u * L, (u + 1) * L)
            b = sum(_dot(part, upper) for part in _split3(lf[:, cols]))
            gb_ref[:, cols] = b
            a_pad = jnp.concatenate([li[:, cols] - b, jnp.zeros((LANES - GATE_ROWS, L), f32)], axis=0)
            ga_ref[cols, :] = a_pad.T[:, 0:GATE_ROWS]

    big = ([functools.partial(token_major, cc) for cc in range(pr_ref.shape[1] // 512)]
           + [functools.partial(att_qv, cc) for cc in range(ATT_WIDTH // 256)])
    conv = [functools.partial(conv_silu, g) for g in range(W // 256)]
    proj = [functools.partial(block_diag_qkv, g) for g in range(W // 256)]
    prep = [conv[0], conv[1], proj[0], conv[2], proj[1], conv[3], proj[2], proj[3], gate_outputs]
    assert len(big) >= len(prep)
    for n, task in enumerate(big):
        task()
        if n < len(prep):
            prep[n]()


def _inproj(x2, g_pre, wqT, wvT, wr, wx, conv_w, conv_b, wmqT, wmk, wmvT, wi, wf, bi, bf_, seq):
    T, D = x2.shape
    W = MLSTM_WIDTH
    tm = TOKEN_TILE
    L = MLSTM_CHUNK
    nblk = tm // MOBA_BLOCK
    ncols = wr.shape[1]
    resident = dict(pipeline_mode=pl.Buffered(1))
    const2 = lambda i: (0, 0)
    const3 = lambda i: (0, 0, 0)
    tok = pl.BlockSpec((tm, W), lambda i: (i, 0))
    att_T = pl.BlockSpec((nblk, ATT_WIDTH, MOBA_BLOCK), lambda i: (i, 0, 0))
    chunk_T = pl.BlockSpec((tm // L, W, L), lambda i: (i, 0, 0))
    rows8 = pl.BlockSpec((GATE_ROWS, tm), lambda i: (0, i))
    return pl.pallas_call(
        functools.partial(_inproj_kernel, tm=tm, seq=seq, q_scale=ATT_HEAD_DIM ** -0.5 * LOG2E,
                          k_scale=MLSTM_HEAD_DIM ** -0.5),
        grid=(T // tm,),
        in_specs=[
            pl.BlockSpec((tm, D), lambda i: (i, 0)),
            pl.BlockSpec((1, D), const2),
            pl.BlockSpec((ATT_WIDTH, D), const2, **resident),
            pl.BlockSpec((ATT_WIDTH, D), const2, **resident),
            pl.BlockSpec((D, ncols), const2, **resident),
            pl.BlockSpec((D, W), const2, **resident),
            pl.BlockSpec((CONV_WIDTH, W), const2),
            pl.BlockSpec((1, W), const2),
            pl.BlockSpec((W // 256, 256, 256), const3, **resident),
            pl.BlockSpec((W // 256, 256, 256), const3, **resident),
            pl.BlockSpec((W // 256, 256, 256), const3, **resident),
            pl.BlockSpec((GATE_ROWS, 3 * W), const2, **resident),
            pl.BlockSpec((GATE_ROWS, 3 * W), const2, **resident),
            pl.BlockSpec((GATE_ROWS, 1), const2),
            pl.BlockSpec((GATE_ROWS, 1), const2),
        ],
        out_specs=[att_T, att_T, pl.BlockSpec((tm, ncols), lambda i: (i, 0)),
                   tok, chunk_T, tok, chunk_T,
                   pl.BlockSpec((tm, GATE_ROWS), lambda i: (i, 0)), rows8, rows8],
        out_shape=[
            jax.ShapeDtypeStruct((T // MOBA_BLOCK, ATT_WIDTH, MOBA_BLOCK), bf16),
            jax.ShapeDtypeStruct((T // MOBA_BLOCK, ATT_WIDTH, MOBA_BLOCK), bf16),
            jax.ShapeDtypeStruct((T, ncols), bf16),
            jax.ShapeDtypeStruct((T, W), bf16),
            jax.ShapeDtypeStruct((T // L, W, L), bf16),
            jax.ShapeDtypeStruct((T, W), bf16),
            jax.ShapeDtypeStruct((T // L, W, L), bf16),
            jax.ShapeDtypeStruct((T, GATE_ROWS), f32),
            jax.ShapeDtypeStruct((GATE_ROWS, T), f32),
            jax.ShapeDtypeStruct((GATE_ROWS, T), f32),
        ],
        scratch_shapes=[pltpu.VMEM((HALO_ROWS + tm, W), f32)],
        compiler_params=pltpu.CompilerParams(
            dimension_semantics=("arbitrary",), vmem_limit_bytes=VMEM_LIMIT),
        name="inproj",
    )(x2, g_pre, wqT, wvT, wr, wx, conv_w, conv_b, wmqT, wmk, wmvT, wi, wf, bi, bf_)


def _mlstm_kernel(qT_ref, k_ref, vT_ref, xc_ref, z_ref, ga_ref, gb_ref, gl_ref, nw_ref, sk_ref,
                  y_ref, ct_ref, m_ref):
    L = MLSTM_CHUNK
    dh = MLSTM_HEAD_DIM

    @pl.when(pl.program_id(1) == 0)
    def _():
        ct_ref[...] = jnp.zeros_like(ct_ref)
        m_ref[...] = jnp.zeros_like(m_ref)

    s_idx = lax.broadcasted_iota(jnp.int32, (L, L), 0)
    t_idx = lax.broadcasted_iota(jnp.int32, (L, L), 1)
    causal = s_idx <= t_idx
    ones_rows = jnp.ones((SUM_ROWS, L), bf16)
    for h in range(MLSTM_HEADS):
        sl = slice(h * dh, (h + 1) * dh)
        qT = qT_ref[sl, :]
        k = k_ref[:, sl]
        vT_ext = jnp.concatenate([vT_ref[sl, :], ones_rows], axis=0)
        a_c = ga_ref[:, h:h + 1]
        b_r = gb_ref[h:h + 1, :]
        li_r = gl_ref[h:h + 1, :]
        b_last = b_r[:, L - 1:L]
        m_prev = m_ref[h][0:1, 0:1]
        ct = ct_ref[h]

        log_d = jnp.where(causal, a_c + b_r, NEG_BIG)
        inter = b_r + m_prev
        m_t = jnp.maximum(inter, jnp.max(log_d, axis=0, keepdims=True))
        sT = _dot(k, qT) * jnp.exp2(log_d - m_t)
        dec = jnp.exp2(inter - m_t)
        num = _dot(vT_ext, sT.astype(bf16)) + dec * _dot(ct.astype(bf16), qT)
        den = num[dh:dh + 1, :]
        hT = num[0:dh, :] / jnp.maximum(jnp.abs(den), jnp.exp2(-m_t))

        mu = jnp.mean(hT, axis=0, keepdims=True)
        cen = hT - mu
        var = jnp.mean(cen * cen, axis=0, keepdims=True)
        yn = (cen * lax.rsqrt(var + LN_EPS)).T
        z = z_ref[:, sl].astype(f32)
        out = (yn * nw_ref[:, sl] + sk_ref[:, sl] * xc_ref[:, sl].astype(f32)) * (z * jax.nn.sigmoid(z))
        y_ref[:, sl] = out.astype(y_ref.dtype)

        log_w = b_last - b_r + li_r
        m_new = jnp.maximum(b_last + m_prev, jnp.max(log_w, axis=1, keepdims=True))
        vw = vT_ext * jnp.exp2(log_w - m_new).astype(bf16)
        ct_ref[h] = jnp.exp2(b_last + m_prev - m_new) * ct + _dot(vw, k)
        m_ref[h] = jnp.broadcast_to(m_new, m_ref.shape[1:])


def _mlstm(qT, k, vT, xc, pr, ga, gb, gl, mh_norm, skip, batch, seq):
    T, W = k.shape
    L = MLSTM_CHUNK
    nc = seq // L
    z_col = 2
    tok = pl.BlockSpec((L, W), lambda b, c: (b * nc + c, 0))
    chunkT = pl.BlockSpec((None, W, L), lambda b, c: (b * nc + c, 0, 0))
    rows8 = pl.BlockSpec((GATE_ROWS, L), lambda b, c: (0, b * nc + c))
    return pl.pallas_call(
        _mlstm_kernel,
        grid=(batch, nc),
        in_specs=[chunkT, tok, chunkT, tok,
                  pl.BlockSpec((L, W), lambda b, c: (b * nc + c, z_col)),
                  pl.BlockSpec((L, GATE_ROWS), lambda b, c: (b * nc + c, 0)),
                  rows8, rows8,
                  pl.BlockSpec((1, W), lambda b, c: (0, 0)),
                  pl.BlockSpec((1, W), lambda b, c: (0, 0))],
        out_specs=tok,
        out_shape=jax.ShapeDtypeStruct((T, W), bf16),
        scratch_shapes=[pltpu.VMEM((MLSTM_HEADS, MLSTM_HEAD_DIM + SUM_ROWS, MLSTM_HEAD_DIM), f32),
                        pltpu.VMEM((MLSTM_HEADS, 8, LANES), f32)],
        compiler_params=pltpu.CompilerParams(
            dimension_semantics=("arbitrary", "arbitrary"), vmem_limit_bytes=VMEM_LIMIT),
        name="mlstm",
    )(qT, k, vT, xc, pr, ga, gb, gl, mh_norm, skip)


def _moba_kernel(qT_ref, k_ref, vT_ref, g_ref, bias_ref, o_ref, kmean_ref, sel_ref, sa_ref, sb_ref,
                 acc_ref, m_ref, *, nb, heads):
    blk_len = MOBA_BLOCK
    dh = ATT_HEAD_DIM
    own = pl.program_id(2)

    @pl.when(own == 0)
    def _():
        def mean_body(j, c):
            kb = k_ref[pl.ds(pl.multiple_of(j * blk_len, blk_len), blk_len), :].astype(f32)
            kmean_ref[pl.ds(j, 1), :] = jnp.sum(kb, axis=0, keepdims=True) * (1.0 / blk_len)
            return c
        lax.fori_loop(0, nb, mean_body, 0)

    hsl = [slice(hh * dh, (hh + 1) * dh) for hh in range(heads)]
    blk = lax.broadcasted_iota(jnp.int32, (nb, blk_len), 0)
    past = blk < own
    for hh in range(heads):
        qT = qT_ref[hsl[hh], :]
        km = kmean_ref[:, hsl[hh]]
        km_hi = km.astype(bf16)
        km_lo = (km - km_hi.astype(f32)).astype(bf16)
        gate = _dot(km_hi, qT) + _dot(km_lo, qT)
        g = jnp.where(past, gate, -jnp.inf)
        sel = blk == own
        for _ in range(MOBA_TOPK):
            mx = jnp.max(g, axis=0, keepdims=True)
            first = jnp.min(jnp.where(g == mx, blk, nb), axis=0, keepdims=True)
            pick = blk == first
            sel = sel | (pick & past)
            g = jnp.where(pick, -jnp.inf, g)
        sel_ref[hh] = jnp.where(sel, 0.0, NEG_BIG)

    ones_rows = jnp.ones((SUM_ROWS, blk_len), bf16)
    m_ref[...] = jnp.full(m_ref.shape, NEG_BIG, f32)
    acc_ref[...] = jnp.zeros(acc_ref.shape, f32)

    def score_matmuls(j, dst_ref):
        jc = jnp.minimum(j, own)
        rows = pl.ds(pl.multiple_of(jc * blk_len, blk_len), blk_len)
        tile = jnp.minimum(own - jc, NEAR_TILES)
        for hh in range(heads):
            dst_ref[hh] = _dot(k_ref[rows, hsl[hh]], qT_ref[hsl[hh], :]) + bias_ref[hh, tile]

    def softmax_update(j, src_ref):
        for hh in range(heads):
            m = m_ref[hh]
            selrow = sel_ref[hh, pl.ds(j, 1), :]
            m_new = jnp.maximum(m, jnp.max(src_ref[hh], axis=0, keepdims=True) + selrow)
            p = jnp.exp2(src_ref[hh] - jnp.where(selrow < 0.0, -NEG_BIG, m_new)).astype(bf16)
            lhs = jnp.concatenate([vT_ref[j, hsl[hh], :], ones_rows], axis=0)
            acc_ref[hh] = jnp.exp2(m - m_new) * acc_ref[hh] + _dot(lhs, p)
            m_ref[hh] = m_new

    score_matmuls(0, sa_ref)

    def pair_body(jj, c):
        j = 2 * jj
        score_matmuls(j + 1, sb_ref)
        softmax_update(j, sa_ref)
        score_matmuls(j + 2, sa_ref)
        softmax_update(j + 1, sb_ref)
        return c

    lax.fori_loop(0, (own + 2) // 2, pair_body, 0)
    for hh in range(heads):
        acc = acc_ref[hh]
        o = (acc[0:dh, :] / acc[dh:dh + 1, :]).T
        gg = g_ref[:, hsl[hh]].astype(f32)
        o_ref[:, hsl[hh]] = (o * (gg * jax.nn.sigmoid(gg))).astype(o_ref.dtype)


def _moba(qT, vT, pr, bias_tiles, batch, seq):
    T = pr.shape[0]
    nb = seq // MOBA_BLOCK
    G = MOBA_HEADS_PER_STEP
    gw = G * ATT_HEAD_DIM
    gate_col0 = ATT_WIDTH // gw
    vT4 = vT.reshape(batch, nb, ATT_WIDTH, MOBA_BLOCK)
    return pl.pallas_call(
        functools.partial(_moba_kernel, nb=nb, heads=G),
        grid=(ATT_HEADS // G, batch, nb),
        in_specs=[
            pl.BlockSpec((None, gw, MOBA_BLOCK), lambda h, b, i: (b * nb + i, h, 0)),
            pl.BlockSpec((seq, gw), lambda h, b, i: (b, h)),
            pl.BlockSpec((None, nb, gw, MOBA_BLOCK), lambda h, b, i: (b, 0, h, 0)),
            pl.BlockSpec((MOBA_BLOCK, gw), lambda h, b, i: (b * nb + i, gate_col0 + h)),
            pl.BlockSpec((G, BIAS_TILES, MOBA_BLOCK, MOBA_BLOCK), lambda h, b, i: (h, 0, 0, 0),
                         pipeline_mode=pl.Buffered(1)),
        ],
        out_specs=pl.BlockSpec((MOBA_BLOCK, gw), lambda h, b, i: (b * nb + i, h)),
        out_shape=jax.ShapeDtypeStruct((T, ATT_WIDTH), bf16),
        scratch_shapes=[pltpu.VMEM((nb, gw), f32),
                        pltpu.VMEM((G, nb, MOBA_BLOCK), f32),
                        pltpu.VMEM((G, MOBA_BLOCK, MOBA_BLOCK), f32),
                        pltpu.VMEM((G, MOBA_BLOCK, MOBA_BLOCK), f32),
                        pltpu.VMEM((G, ATT_HEAD_DIM + SUM_ROWS, MOBA_BLOCK), f32),
                        pltpu.VMEM((G, 1, MOBA_BLOCK), f32)],
        compiler_params=pltpu.CompilerParams(
            dimension_semantics=("arbitrary", "arbitrary", "arbitrary"),
            vmem_limit_bytes=VMEM_LIMIT),
        name="moba",
    )(qT, pr, vT4, pr, bias_tiles)


def _outproj_kernel(ya_ref, ym_ref, wa_ref, wm_ref, g_ref, x_ref, o_ref):
    y = _dot(ya_ref[...], wa_ref[...]) + _dot(ym_ref[...], wm_ref[...])
    ms = jnp.mean(y * y, axis=-1, keepdims=True)
    o_ref[...] = x_ref[...] + y * lax.rsqrt(ms + RMS_EPS) * g_ref[...]


def _outproj(ya, ym, wa, wm, g_post, x2):
    T, D = x2.shape
    tm = TOKEN_TILE
    resident = dict(pipeline_mode=pl.Buffered(1))
    return pl.pallas_call(
        _outproj_kernel,
        grid=(T // tm,),
        in_specs=[
            pl.BlockSpec((tm, ya.shape[1]), lambda i: (i, 0)),
            pl.BlockSpec((tm, ym.shape[1]), lambda i: (i, 0)),
            pl.BlockSpec(wa.shape, lambda i: (0, 0), **resident),
            pl.BlockSpec(wm.shape, lambda i: (0, 0), **resident),
            pl.BlockSpec((1, D), lambda i: (0, 0)),
            pl.BlockSpec((tm, D), lambda i: (i, 0)),
        ],
        out_specs=pl.BlockSpec((tm, D), lambda i: (i, 0)),
        out_shape=jax.ShapeDtypeStruct((T, D), f32),
        compiler_params=pltpu.CompilerParams(
            dimension_semantics=("arbitrary",), vmem_limit_bytes=VMEM_LIMIT),
        name="outproj",
    )(ya, ym, wa, wm, g_post, x2)


def _block_diag_256(w):
    width = w.shape[0] * QKV_BLOCK
    rows = w.reshape(width, QKV_BLOCK)
    col = np.arange(256)
    spread = jnp.asarray(col[None, :] % QKV_BLOCK == np.arange(QKV_BLOCK)[:, None], w.dtype)
    tiled = jnp.dot(rows, spread, precision=lax.Precision.HIGHEST)
    same_block = (np.arange(width)[:, None] % 256) // QKV_BLOCK == col[None, :] // QKV_BLOCK
    dense = jnp.where(jnp.asarray(same_block), tiled, 0.0)
    return dense.reshape(-1, 256, 256)


def _gate_rows(w_cols, b_vals):
    pad = GATE_ROWS - MLSTM_HEADS
    w_rows = jnp.pad(w_cols.T, ((0, pad), (0, 0))).astype(bf16)
    b_rows = jnp.pad(b_vals, (0, pad)).reshape(GATE_ROWS, 1)
    return w_rows, b_rows


def _layer(x, rel_bias, g_pre, g_post, w_in, conv_w, conv_b, wq_m, wk_m, wv_m,
           w_if, b_if, mh_norm, skip, w_out):
    batch, seq, d_model = x.shape
    assert seq % TOKEN_TILE == 0 and TOKEN_TILE % MLSTM_CHUNK == 0 and TOKEN_TILE % MOBA_BLOCK == 0
    aw, nh = ATT_WIDTH, MLSTM_HEADS
    x2 = x.reshape(batch * seq, d_model)

    w_bf = w_in.astype(bf16)
    wqT = w_bf[:, 0:aw].T
    wvT = w_bf[:, 2 * aw:3 * aw].T
    mw = MLSTM_WIDTH
    wr = jnp.concatenate([w_bf[:, aw:2 * aw], w_bf[:, 3 * aw:4 * aw], w_bf[:, 4 * aw + mw:]], axis=1)
    wx = w_bf[:, 4 * aw:4 * aw + mw]
    wqT_bd = jnp.swapaxes(_block_diag_256(wq_m), 1, 2).astype(bf16)
    wk_bd = _block_diag_256(wk_m).astype(bf16)
    wvT_bd = jnp.swapaxes(_block_diag_256(wv_m), 1, 2).astype(bf16)
    wi, bi = _gate_rows(w_if[:, 0:nh], b_if[0:nh])
    wf, bf_ = _gate_rows(w_if[:, nh:2 * nh], b_if[nh:2 * nh])
    w_out_bf = w_out.astype(bf16)

    bias_tiles = _bias_tiles(rel_bias)
    qT, vT, pr, xc, qmT, km, vmT, ga, gb, gl = _inproj(
        x2, g_pre.reshape(1, -1), wqT, wvT, wr, wx, conv_w, conv_b.reshape(1, -1),
        wqT_bd, wk_bd, wvT_bd, wi, wf, bi, bf_, seq)
    ym = _mlstm(qmT, km, vmT, xc, pr, ga, gb, gl, mh_norm.reshape(1, -1), skip.reshape(1, -1),
                batch, seq)
    ya = _moba(qT, vT, pr, bias_tiles, batch, seq)
    out = _outproj(ya, ym, w_out_bf[0:aw], w_out_bf[aw:], g_post.reshape(1, -1), x2)
    return out.reshape(batch, seq, d_model)


def kernel(x, rel_bias, g_pre, g_post, w_in, conv_w, conv_b, wq_m, wk_m, wv_m, w_if, b_if,
           mh_norm, skip, w_out):
    depth = w_in.shape[0]
    for l in range(depth):
        x = _layer(x, rel_bias, g_pre[l], g_post[l], w_in[l], conv_w[l], conv_b[l], wq_m[l],
                   wk_m[l], wv_m[l], w_if[l], b_if[l], mh_norm[l], skip[l], w_out[l])
    return x
```

```python
import functools
import math

import jax
import jax.numpy as jnp
import numpy as np
from jax import lax
from jax.experimental import pallas as pl
from jax.experimental.pallas import tpu as pltpu

f32 = jnp.float32
bf16 = jnp.bfloat16

ATT_HEADS = 8
ATT_HEAD_DIM = 128
ATT_WIDTH = ATT_HEADS * ATT_HEAD_DIM
MOBA_BLOCK = 256
MOBA_TOPK = 3
REL_BUCKETS = 32
REL_MAX_DIST = 2048
MLSTM_HEADS = 4
MLSTM_WIDTH = 1024
MLSTM_HEAD_DIM = MLSTM_WIDTH // MLSTM_HEADS
QKV_BLOCK = 4
CONV_WIDTH = 4
RMS_EPS = 1e-6
LN_EPS = 1e-5

MLSTM_CHUNK = 256
TOKEN_TILE = 512
MOBA_HEADS_PER_STEP = 4
LONG_TRIP = 8
HALO_ROWS = 16
SUM_ROWS = 16
GATE_ROWS = 8
SEL_PAD = 8
LANES = 128
NEG_BIG = -1e30
VMEM_LIMIT = 56 * 1024 * 1024
LOG2E = math.log2(math.e)

_NT = (((1,), (1,)), ((), ()))


def _t5_thresholds():
    n = np.arange(0, 2 * REL_MAX_DIST, dtype=np.int64)
    max_exact = REL_BUCKETS // 2
    nf = np.maximum(n, 1).astype(np.float32)
    large = max_exact + (np.log(nf / np.float32(max_exact))
                         / np.float32(math.log(REL_MAX_DIST / max_exact))
                         * np.float32(REL_BUCKETS - max_exact)).astype(np.int32)
    large = np.minimum(large, REL_BUCKETS - 1)
    bucket = np.where(n < max_exact, n, large)
    assert np.all(np.diff(bucket) >= 0)
    return [int(np.argmax(bucket >= k)) for k in range(1, REL_BUCKETS)]


T5_THR = _t5_thresholds()
NEAR_TILES = -(-(T5_THR[-1] + MOBA_BLOCK - 1) // MOBA_BLOCK)
assert NEAR_TILES * MOBA_BLOCK - (MOBA_BLOCK - 1) >= T5_THR[-1]
BIAS_TILES = NEAR_TILES + 1


def _dot(a, b):
    return jnp.dot(a, b, preferred_element_type=f32)


def _dot_nt(a, b):
    return lax.dot_general(a, b, _NT, preferred_element_type=f32)


def _split3(x):
    hi = x.astype(bf16)
    r = x - hi.astype(f32)
    mid = r.astype(bf16)
    lo = (r - mid.astype(f32)).astype(bf16)
    return hi, mid, lo


def _bias_kernel(rb_ref, out_ref):
    h = pl.program_id(0)
    key = lax.broadcasted_iota(jnp.int32, (MOBA_BLOCK, MOBA_BLOCK), 0)
    qry = lax.broadcasted_iota(jnp.int32, (MOBA_BLOCK, MOBA_BLOCK), 1)
    base = qry - key
    for d in range(BIAS_TILES):
        dist = base + d * MOBA_BLOCK
        n = jnp.maximum(dist, 0)
        val = jnp.full((MOBA_BLOCK, MOBA_BLOCK), rb_ref[REL_BUCKETS - 1, h] * LOG2E, f32)
        for k in range(REL_BUCKETS - 2, -1, -1):
            val = jnp.where(n < T5_THR[k], rb_ref[k, h] * LOG2E, val)
        if d == 0:
            val = jnp.where(dist >= 0, val, NEG_BIG)
        out_ref[d] = val


def _bias_tiles(rel_bias):
    return pl.pallas_call(
        _bias_kernel,
        grid=(ATT_HEADS,),
        in_specs=[pl.BlockSpec(memory_space=pltpu.SMEM)],
        out_specs=pl.BlockSpec((None, BIAS_TILES, MOBA_BLOCK, MOBA_BLOCK), lambda h: (h, 0, 0, 0)),
        out_shape=jax.ShapeDtypeStruct((ATT_HEADS, BIAS_TILES, MOBA_BLOCK, MOBA_BLOCK), f32),
        name="bias_tiles",
    )(rel_bias)


def _log_sigmoid(v):
    return jnp.minimum(v, 0.0) - jnp.log1p(jnp.exp(-jnp.abs(v)))


def _inproj_kernel(x_ref, g_ref, wqT_ref, wvT_ref, wr_ref, wx_ref, cw_ref, cb_ref, wmqT_ref, wmk_ref,
                   wmvT_ref, wi_ref, wf_ref, bi_ref, bf_ref,
                   qT_ref, vT_ref, pr_ref, xc_ref, mqT_ref, mk_ref, mvT_ref, ga_ref, gb_ref, gl_ref,
                   xpad_ref, *, tm, seq, q_scale, k_scale):
    i = pl.program_id(0)
    W = MLSTM_WIDTH
    L = MLSTM_CHUNK
    x = x_ref[...]
    ms = jnp.mean(x * x, axis=-1, keepdims=True)
    h = (x * lax.rsqrt(ms + RMS_EPS) * g_ref[...]).astype(bf16)

    @pl.when((i * tm) % seq == 0)
    def _():
        xpad_ref[0:HALO_ROWS, :] = jnp.zeros((HALO_ROWS, W), f32)

    for cc in range(W // 512):
        cols = slice(cc * 512, (cc + 1) * 512)
        xpad_ref[HALO_ROWS:HALO_ROWS + tm, cols] = _dot(h, wx_ref[:, cols])

    def att_qv(cc):
        rows = slice(cc * 256, (cc + 1) * 256)
        qt = _dot_nt(wqT_ref[rows, :], h) * q_scale
        vt = _dot_nt(wvT_ref[rows, :], h)
        for u in range(tm // MOBA_BLOCK):
            cols = slice(u * MOBA_BLOCK, (u + 1) * MOBA_BLOCK)
            qT_ref[u, rows, :] = qt[:, cols].astype(bf16)
            vT_ref[u, rows, :] = vt[:, cols].astype(bf16)

    def token_major(cc):
        cols = slice(cc * 512, (cc + 1) * 512)
        pr_ref[:, cols] = _dot(h, wr_ref[:, cols]).astype(bf16)

    def conv_silu(g):
        sl = slice(g * 256, (g + 1) * 256)
        acc = cb_ref[:, sl] + cw_ref[CONV_WIDTH - 1:CONV_WIDTH, sl] * xpad_ref[HALO_ROWS:HALO_ROWS + tm, sl]
        for j in range(CONV_WIDTH - 1):
            off = HALO_ROWS - (CONV_WIDTH - 1) + j
            acc = acc + cw_ref[j:j + 1, sl] * xpad_ref[off:off + tm, sl]
        xc_ref[:, sl] = (acc * jax.nn.sigmoid(acc)).astype(bf16)
        xpad_ref[0:HALO_ROWS, sl] = xpad_ref[tm:tm + HALO_ROWS, sl]

    k_unscale = 1.0 / k_scale
    gates = [jnp.zeros((GATE_ROWS, tm), f32) + bi_ref[...], jnp.zeros((GATE_ROWS, tm), f32) + bf_ref[...]]

    def block_diag_qkv(g):
        sl = slice(g * 256, (g + 1) * 256)
        xc_g = xc_ref[:, sl]
        xm_g = xpad_ref[HALO_ROWS:HALO_ROWS + tm, sl].astype(bf16)
        qT = _dot_nt(wmqT_ref[g], xc_g).astype(bf16)
        vT = _dot_nt(wmvT_ref[g], xm_g).astype(bf16)
        kk = (_dot(xc_g, wmk_ref[g]) * k_scale).astype(bf16)
        mk_ref[:, sl] = kk
        for u in range(tm // L):
            cols = slice(u * L, (u + 1) * L)
            mqT_ref[u, sl, :] = qT[:, cols]
            mvT_ref[u, sl, :] = vT[:, cols]
        ks = slice(W + g * 256, W + (g + 1) * 256)
        vs = slice(2 * W + g * 256, 2 * W + (g + 1) * 256)
        for n, w_ref in enumerate((wi_ref, wf_ref)):
            gates[n] = (gates[n] + _dot(w_ref[:, sl], qT) + k_unscale * _dot_nt(w_ref[:, ks], kk)
                        + _dot(w_ref[:, vs], vT))

    def gate_outputs():
        li = gates[0] * LOG2E
        lf = _log_sigmoid(gates[1]) * LOG2E
        row = lax.broadcasted_iota(jnp.int32, (L, L), 0)
        col = lax.broadcasted_iota(jnp.int32, (L, L), 1)
        upper = jnp.where(row <= col, 1.0, 0.0).astype(bf16)
        gl_ref[...] = li
        for u in range(tm // L):
            cols = slice(u * L, (u + 1) * L)
            b = sum(_dot(part, upper) for part in _split3(lf[:, cols]))
            gb_ref[:, cols] = b
            a_pad = jnp.concatenate([li[:, cols] - b, jnp.zeros((LANES - GATE_ROWS, L), f32)], axis=0)
            ga_ref[cols, :] = a_pad.T[:, 0:GATE_ROWS]

    big = ([functools.partial(token_major, cc) for cc in range(pr_ref.shape[1] // 512)]
           + [functools.partial(att_qv, cc) for cc in range(ATT_WIDTH // 256)])
    conv = [functools.partial(conv_silu, g) for g in range(W // 256)]
    proj = [functools.partial(block_diag_qkv, g) for g in range(W // 256)]
    prep = [conv[0], conv[1], proj[0], conv[2], proj[1], conv[3], proj[2], proj[3], gate_outputs]
    assert len(big) >= len(prep)
    for n, task in enumerate(big):
        task()
        if n < len(prep):
            prep[n]()


def _inproj(x2, g_pre, wqT, wvT, wr, wx, conv_w, conv_b, wmqT, wmk, wmvT, wi, wf, bi, bf_, seq):
    T, D = x2.shape
    W = MLSTM_WIDTH
    tm = TOKEN_TILE
    L = MLSTM_CHUNK
    nblk = tm // MOBA_BLOCK
    ncols = wr.shape[1]
    resident = dict(pipeline_mode=pl.Buffered(1))
    const2 = lambda i: (0, 0)
    const3 = lambda i: (0, 0, 0)
    tok = pl.BlockSpec((tm, W), lambda i: (i, 0))
    att_T = pl.BlockSpec((nblk, ATT_WIDTH, MOBA_BLOCK), lambda i: (i, 0, 0))
    chunk_T = pl.BlockSpec((tm // L, W, L), lambda i: (i, 0, 0))
    rows8 = pl.BlockSpec((GATE_ROWS, tm), lambda i: (0, i))
    return pl.pallas_call(
        functools.partial(_inproj_kernel, tm=tm, seq=seq, q_scale=ATT_HEAD_DIM ** -0.5 * LOG2E,
                          k_scale=MLSTM_HEAD_DIM ** -0.5),
        grid=(T // tm,),
        in_specs=[
            pl.BlockSpec((tm, D), lambda i: (i, 0)),
            pl.BlockSpec((1, D), const2),
            pl.BlockSpec((ATT_WIDTH, D), const2, **resident),
            pl.BlockSpec((ATT_WIDTH, D), const2, **resident),
            pl.BlockSpec((D, ncols), const2, **resident),
            pl.BlockSpec((D, W), const2, **resident),
            pl.BlockSpec((CONV_WIDTH, W), const2),
            pl.BlockSpec((1, W), const2),
            pl.BlockSpec((W // 256, 256, 256), const3, **resident),
            pl.BlockSpec((W // 256, 256, 256), const3, **resident),
            pl.BlockSpec((W // 256, 256, 256), const3, **resident),
            pl.BlockSpec((GATE_ROWS, 3 * W), const2, **resident),
            pl.BlockSpec((GATE_ROWS, 3 * W), const2, **resident),
            pl.BlockSpec((GATE_ROWS, 1), const2),
            pl.BlockSpec((GATE_ROWS, 1), const2),
        ],
        out_specs=[att_T, att_T, pl.BlockSpec((tm, ncols), lambda i: (i, 0)),
                   tok, chunk_T, tok, chunk_T,
                   pl.BlockSpec((tm, GATE_ROWS), lambda i: (i, 0)), rows8, rows8],
        out_shape=[
            jax.ShapeDtypeStruct((T // MOBA_BLOCK, ATT_WIDTH, MOBA_BLOCK), bf16),
            jax.ShapeDtypeStruct((T // MOBA_BLOCK, ATT_WIDTH, MOBA_BLOCK), bf16),
            jax.ShapeDtypeStruct((T, ncols), bf16),
            jax.ShapeDtypeStruct((T, W), bf16),
            jax.ShapeDtypeStruct((T // L, W, L), bf16),
            jax.ShapeDtypeStruct((T, W), bf16),
            jax.ShapeDtypeStruct((T // L, W, L), bf16),
            jax.ShapeDtypeStruct((T, GATE_ROWS), f32),
            jax.ShapeDtypeStruct((GATE_ROWS, T), f32),
            jax.ShapeDtypeStruct((GATE_ROWS, T), f32),
        ],
        scratch_shapes=[pltpu.VMEM((HALO_ROWS + tm, W), f32)],
        compiler_params=pltpu.CompilerParams(
            dimension_semantics=("arbitrary",), vmem_limit_bytes=VMEM_LIMIT),
        name="inproj",
    )(x2, g_pre, wqT, wvT, wr, wx, conv_w, conv_b, wmqT, wmk, wmvT, wi, wf, bi, bf_)


def _mlstm_kernel(qT_ref, k_ref, vT_ref, xc_ref, z_ref, ga_ref, gb_ref, gl_ref, nw_ref, sk_ref,
                  y_ref, ct_ref, m_ref):
    L = MLSTM_CHUNK
    dh = MLSTM_HEAD_DIM

    @pl.when(pl.program_id(1) == 0)
    def _():
        ct_ref[...] = jnp.zeros_like(ct_ref)
        m_ref[...] = jnp.zeros_like(m_ref)

    s_idx = lax.broadcasted_iota(jnp.int32, (L, L), 0)
    t_idx = lax.broadcasted_iota(jnp.int32, (L, L), 1)
    causal = s_idx <= t_idx
    ones_rows = jnp.ones((SUM_ROWS, L), bf16)
    for h in range(MLSTM_HEADS):
        sl = slice(h * dh, (h + 1) * dh)
        qT = qT_ref[sl, :]
        k = k_ref[:, sl]
        vT_ext = jnp.concatenate([vT_ref[sl, :], ones_rows], axis=0)
        a_c = ga_ref[:, h:h + 1]
        b_r = gb_ref[h:h + 1, :]
        li_r = gl_ref[h:h + 1, :]
        b_last = b_r[:, L - 1:L]
        m_prev = m_ref[h][0:1, 0:1]
        ct = ct_ref[h]

        log_d = jnp.where(causal, a_c + b_r, NEG_BIG)
        inter = b_r + m_prev
        m_t = jnp.maximum(inter, jnp.max(log_d, axis=0, keepdims=True))
        sT = _dot(k, qT) * jnp.exp2(log_d - m_t)
        dec = jnp.exp2(inter - m_t)
        num = _dot(vT_ext, sT.astype(bf16)) + dec * _dot(ct.astype(bf16), qT)
        den = num[dh:dh + 1, :]
        hT = num[0:dh, :] / jnp.maximum(jnp.abs(den), jnp.exp2(-m_t))

        mu = jnp.mean(hT, axis=0, keepdims=True)
        cen = hT - mu
        var = jnp.mean(cen * cen, axis=0, keepdims=True)
        yn = (cen * lax.rsqrt(var + LN_EPS)).T
        z = z_ref[:, sl].astype(f32)
        out = (yn * nw_ref[:, sl] + sk_ref[:, sl] * xc_ref[:, sl].astype(f32)) * (z * jax.nn.sigmoid(z))
        y_ref[:, sl] = out.astype(y_ref.dtype)

        log_w = b_last - b_r + li_r
        m_new = jnp.maximum(b_last + m_prev, jnp.max(log_w, axis=1, keepdims=True))
        vw = vT_ext * jnp.exp2(log_w - m_new).astype(bf16)
        ct_ref[h] = jnp.exp2(b_last + m_prev - m_new) * ct + _dot(vw, k)
        m_ref[h] = jnp.broadcast_to(m_new, m_ref.shape[1:])


def _mlstm(qT, k, vT, xc, pr, ga, gb, gl, mh_norm, skip, batch, seq):
    T, W = k.shape
    L = MLSTM_CHUNK
    nc = seq // L
    z_col = 2
    tok = pl.BlockSpec((L, W), lambda b, c: (b * nc + c, 0))
    chunkT = pl.BlockSpec((None, W, L), lambda b, c: (b * nc + c, 0, 0))
    rows8 = pl.BlockSpec((GATE_ROWS, L), lambda b, c: (0, b * nc + c))
    return pl.pallas_call(
        _mlstm_kernel,
        grid=(batch, nc),
        in_specs=[chunkT, tok, chunkT, tok,
                  pl.BlockSpec((L, W), lambda b, c: (b * nc + c, z_col)),
                  pl.BlockSpec((L, GATE_ROWS), lambda b, c: (b * nc + c, 0)),
                  rows8, rows8,
                  pl.BlockSpec((1, W), lambda b, c: (0, 0)),
                  pl.BlockSpec((1, W), lambda b, c: (0, 0))],
        out_specs=tok,
        out_shape=jax.ShapeDtypeStruct((T, W), bf16),
        scratch_shapes=[pltpu.VMEM((MLSTM_HEADS, MLSTM_HEAD_DIM + SUM_ROWS, MLSTM_HEAD_DIM), f32),
                        pltpu.VMEM((MLSTM_HEADS, 8, LANES), f32)],
        compiler_params=pltpu.CompilerParams(
            dimension_semantics=("arbitrary", "arbitrary"), vmem_limit_bytes=VMEM_LIMIT),
        name="mlstm",
    )(qT, k, vT, xc, pr, ga, gb, gl, mh_norm, skip)


def _moba_kernel(qT_ref, k_ref, vT_ref, g_ref, bias_ref, o_ref, kmean_ref, sel_ref, sa_ref, sb_ref,
                 acc_ref, m_ref, alpha_ref, p_ref, *, nb, heads):
    blk_len = MOBA_BLOCK
    dh = ATT_HEAD_DIM
    own = pl.program_id(2)

    @pl.when(own == 0)
    def _():
        def mean_body(j, c):
            kb = k_ref[pl.ds(pl.multiple_of(j * blk_len, blk_len), blk_len), :].astype(f32)
            kmean_ref[pl.ds(j, 1), :] = jnp.sum(kb, axis=0, keepdims=True) * (1.0 / blk_len)
            return c
        lax.fori_loop(0, nb, mean_body, 0)

    hsl = [slice(hh * dh, (hh + 1) * dh) for hh in range(heads)]

    def score_matmuls(j, dst_ref):
        jc = jnp.minimum(j, own)
        rows = pl.ds(pl.multiple_of(jc * blk_len, blk_len), blk_len)
        tile = jnp.minimum(own - jc, NEAR_TILES)
        for hh in range(heads):
            dst_ref[hh] = _dot(k_ref[rows, hsl[hh]], qT_ref[hsl[hh], :]) + bias_ref[hh, tile]

    blk = lax.broadcasted_iota(jnp.int32, (nb, blk_len), 0)
    past = blk < own
    for hh in range(heads):
        qT = qT_ref[hsl[hh], :]
        km = kmean_ref[:, hsl[hh]]
        km_hi = km.astype(bf16)
        km_lo = (km - km_hi.astype(f32)).astype(bf16)
        gate = _dot(km_hi, qT) + _dot(km_lo, qT)
        g = jnp.where(past, gate, -jnp.inf)
        sel = blk == own
        for _ in range(MOBA_TOPK):
            mx = jnp.max(g, axis=0, keepdims=True)
            first = jnp.min(jnp.where(g == mx, blk, nb), axis=0, keepdims=True)
            pick = blk == first
            sel = sel | (pick & past)
            g = jnp.where(pick, -jnp.inf, g)
        sel_ref[hh, 0:nb, :] = jnp.where(sel, 0.0, NEG_BIG)
        sel_ref[hh, nb:nb + SEL_PAD, :] = jnp.full((SEL_PAD, blk_len), NEG_BIG, f32)

    ones_rows = jnp.ones((SUM_ROWS, blk_len), bf16)
    m_ref[...] = jnp.full(m_ref.shape, NEG_BIG, f32)
    acc_ref[...] = jnp.zeros(acc_ref.shape, f32)

    def softmax(j, src_ref):
        stats = []
        for hh in range(heads):
            m = m_ref[hh]
            selrow = sel_ref[hh, pl.ds(j, 1), :]
            m_new = jnp.maximum(m, jnp.max(src_ref[hh], axis=0, keepdims=True) + selrow)
            p = jnp.exp2(src_ref[hh] - jnp.where(selrow < 0.0, -NEG_BIG, m_new)).astype(bf16)
            m_ref[hh] = m_new
            stats.append((jnp.exp2(m - m_new), p))
        return stats

    def value_update(j, stats):
        for hh in range(heads):
            alpha, p = stats[hh]
            lhs = jnp.concatenate([vT_ref[j, hsl[hh], :], ones_rows], axis=0)
            acc_ref[hh] = alpha * acc_ref[hh] + _dot(lhs, p)

    def keep_pending(stats):
        for hh in range(heads):
            alpha_ref[hh], p_ref[hh] = stats[hh]

    score_buf = (sa_ref, sb_ref)

    def trip(j, n):
        value_update(j, [(alpha_ref[hh], p_ref[hh]) for hh in range(heads)])
        score_matmuls(j + 2, sa_ref)
        for t in range(1, n):
            stats = softmax(j + t, score_buf[t % 2])
            if n > 2:
                score_matmuls(j + 2 + t, score_buf[t % 2])
                value_update(j + t, stats)
            else:
                value_update(j + t, stats)
                score_matmuls(j + 2 + t, score_buf[t % 2])
        keep_pending(softmax(j + n, sa_ref))

    score_matmuls(0, sa_ref)
    score_matmuls(1, sb_ref)
    keep_pending(softmax(0, sa_ref))

    n_blocks = own + 1
    n_long = n_blocks // LONG_TRIP

    def long_body(t, c):
        trip(LONG_TRIP * t, LONG_TRIP)
        return c

    def short_body(t, c):
        trip(LONG_TRIP * n_long + 2 * t, 2)
        return c

    lax.fori_loop(0, n_long, long_body, 0)
    lax.fori_loop(0, (n_blocks - LONG_TRIP * n_long + 1) // 2, short_body, 0)
    for hh in range(heads):
        acc = acc_ref[hh]
        o = (acc[0:dh, :] / acc[dh:dh + 1, :]).T
        gg = g_ref[:, hsl[hh]].astype(f32)
        o_ref[:, hsl[hh]] = (o * (gg * jax.nn.sigmoid(gg))).astype(o_ref.dtype)


def _moba(qT, vT, pr, bias_tiles, batch, seq):
    T = pr.shape[0]
    nb = seq // MOBA_BLOCK
    G = MOBA_HEADS_PER_STEP
    gw = G * ATT_HEAD_DIM
    gate_col0 = ATT_WIDTH // gw
    vT4 = vT.reshape(batch, nb, ATT_WIDTH, MOBA_BLOCK)
    return pl.pallas_call(
        functools.partial(_moba_kernel, nb=nb, heads=G),
        grid=(ATT_HEADS // G, batch, nb),
        in_specs=[
            pl.BlockSpec((None, gw, MOBA_BLOCK), lambda h, b, i: (b * nb + i, h, 0)),
            pl.BlockSpec((seq, gw), lambda h, b, i: (b, h)),
            pl.BlockSpec((None, nb, gw, MOBA_BLOCK), lambda h, b, i: (b, 0, h, 0)),
            pl.BlockSpec((MOBA_BLOCK, gw), lambda h, b, i: (b * nb + i, gate_col0 + h)),
            pl.BlockSpec((G, BIAS_TILES, MOBA_BLOCK, MOBA_BLOCK), lambda h, b, i: (h, 0, 0, 0),
                         pipeline_mode=pl.Buffered(1)),
        ],
        out_specs=pl.BlockSpec((MOBA_BLOCK, gw), lambda h, b, i: (b * nb + i, h)),
        out_shape=jax.ShapeDtypeStruct((T, ATT_WIDTH), bf16),
        scratch_shapes=[pltpu.VMEM((nb, gw), f32),
                        pltpu.VMEM((G, nb + SEL_PAD, MOBA_BLOCK), f32),
                        pltpu.VMEM((G, MOBA_BLOCK, MOBA_BLOCK), f32),
                        pltpu.VMEM((G, MOBA_BLOCK, MOBA_BLOCK), f32),
                        pltpu.VMEM((G, ATT_HEAD_DIM + SUM_ROWS, MOBA_BLOCK), f32),
                        pltpu.VMEM((G, 1, MOBA_BLOCK), f32),
                        pltpu.VMEM((G, 1, MOBA_BLOCK), f32),
                        pltpu.VMEM((G, MOBA_BLOCK, MOBA_BLOCK), bf16)],
        compiler_params=pltpu.CompilerParams(
            dimension_semantics=("arbitrary", "arbitrary", "arbitrary"),
            vmem_limit_bytes=VMEM_LIMIT),
        name="moba",
    )(qT, pr, vT4, pr, bias_tiles)


def _outproj_kernel(ya_ref, ym_ref, wa_ref, wm_ref, g_ref, x_ref, o_ref):
    y = _dot(ya_ref[...], wa_ref[...]) + _dot(ym_ref[...], wm_ref[...])
    ms = jnp.mean(y * y, axis=-1, keepdims=True)
    o_ref[...] = x_ref[...] + y * lax.rsqrt(ms + RMS_EPS) * g_ref[...]


def _outproj(ya, ym, wa, wm, g_post, x2):
    T, D = x2.shape
    tm = TOKEN_TILE
    resident = dict(pipeline_mode=pl.Buffered(1))
    return pl.pallas_call(
        _outproj_kernel,
        grid=(T // tm,),
        in_specs=[
            pl.BlockSpec((tm, ya.shape[1]), lambda i: (i, 0)),
            pl.BlockSpec((tm, ym.shape[1]), lambda i: (i, 0)),
            pl.BlockSpec(wa.shape, lambda i: (0, 0), **resident),
            pl.BlockSpec(wm.shape, lambda i: (0, 0), **resident),
            pl.BlockSpec((1, D), lambda i: (0, 0)),
            pl.BlockSpec((tm, D), lambda i: (i, 0)),
        ],
        out_specs=pl.BlockSpec((tm, D), lambda i: (i, 0)),
        out_shape=jax.ShapeDtypeStruct((T, D), f32),
        compiler_params=pltpu.CompilerParams(
            dimension_semantics=("arbitrary",), vmem_limit_bytes=VMEM_LIMIT),
        name="outproj",
    )(ya, ym, wa, wm, g_post, x2)


def _block_diag_256(w):
    width = w.shape[0] * QKV_BLOCK
    rows = w.reshape(width, QKV_BLOCK)
    col = np.arange(256)
    spread = jnp.asarray(col[None, :] % QKV_BLOCK == np.arange(QKV_BLOCK)[:, None], w.dtype)
    tiled = jnp.dot(rows, spread, precision=lax.Precision.HIGHEST)
    same_block = (np.arange(width)[:, None] % 256) // QKV_BLOCK == col[None, :] // QKV_BLOCK
    dense = jnp.where(jnp.asarray(same_block), tiled, 0.0)
    return dense.reshape(-1, 256, 256)


def _gate_rows(w_cols, b_vals):
    pad = GATE_ROWS - MLSTM_HEADS
    w_rows = jnp.pad(w_cols.T, ((0, pad), (0, 0))).astype(bf16)
    b_rows = jnp.pad(b_vals, (0, pad)).reshape(GATE_ROWS, 1)
    return w_rows, b_rows


def _layer(x, rel_bias, g_pre, g_post, w_in, conv_w, conv_b, wq_m, wk_m, wv_m,
           w_if, b_if, mh_norm, skip, w_out):
    batch, seq, d_model = x.shape
    assert seq % TOKEN_TILE == 0 and TOKEN_TILE % MLSTM_CHUNK == 0 and TOKEN_TILE % MOBA_BLOCK == 0
    aw, nh = ATT_WIDTH, MLSTM_HEADS
    x2 = x.reshape(batch * seq, d_model)

    w_bf = w_in.astype(bf16)
    wqT = w_bf[:, 0:aw].T
    wvT = w_bf[:, 2 * aw:3 * aw].T
    mw = MLSTM_WIDTH
    wr = jnp.concatenate([w_bf[:, aw:2 * aw], w_bf[:, 3 * aw:4 * aw], w_bf[:, 4 * aw + mw:]], axis=1)
    wx = w_bf[:, 4 * aw:4 * aw + mw]
    wqT_bd = jnp.swapaxes(_block_diag_256(wq_m), 1, 2).astype(bf16)
    wk_bd = _block_diag_256(wk_m).astype(bf16)
    wvT_bd = jnp.swapaxes(_block_diag_256(wv_m), 1, 2).astype(bf16)
    wi, bi = _gate_rows(w_if[:, 0:nh], b_if[0:nh])
    wf, bf_ = _gate_rows(w_if[:, nh:2 * nh], b_if[nh:2 * nh])
    w_out_bf = w_out.astype(bf16)

    bias_tiles = _bias_tiles(rel_bias)
    qT, vT, pr, xc, qmT, km, vmT, ga, gb, gl = _inproj(
        x2, g_pre.reshape(1, -1), wqT, wvT, wr, wx, conv_w, conv_b.reshape(1, -1),
        wqT_bd, wk_bd, wvT_bd, wi, wf, bi, bf_, seq)
    ym = _mlstm(qmT, km, vmT, xc, pr, ga, gb, gl, mh_norm.reshape(1, -1), skip.reshape(1, -1),
                batch, seq)
    ya = _moba(qT, vT, pr, bias_tiles, batch, seq)
    out = _outproj(ya, ym, w_out_bf[0:aw], w_out_bf[aw:], g_post.reshape(1, -1), x2)
    return out.reshape(batch, seq, d_model)


def kernel(x, rel_bias, g_pre, g_post, w_in, conv_w, conv_b, wq_m, wk_m, wv_m, w_if, b_if,
           mh_norm, skip, w_out):
    depth = w_in.shape[0]
    for l in range(depth):
        x = _layer(x, rel_bias, g_pre[l], g_post[l], w_in[l], conv_w[l], conv_b[l], wq_m[l],
                   wk_m[l], wv_m[l], w_if[l], b_if[l], mh_norm[l], skip[l], w_out[l])
    return x
```

```python
import functools
import math

import jax
import jax.numpy as jnp
import numpy as np
from jax import lax
from jax.experimental import pallas as pl
from jax.experimental.pallas import tpu as pltpu

f32 = jnp.float32
bf16 = jnp.bfloat16

ATT_HEADS = 8
ATT_HEAD_DIM = 128
ATT_WIDTH = ATT_HEADS * ATT_HEAD_DIM
MOBA_BLOCK = 256
MOBA_TOPK = 3
REL_BUCKETS = 32
REL_MAX_DIST = 2048
MLSTM_HEADS = 4
MLSTM_WIDTH = 1024
MLSTM_HEAD_DIM = MLSTM_WIDTH // MLSTM_HEADS
QKV_BLOCK = 4
CONV_WIDTH = 4
RMS_EPS = 1e-6
LN_EPS = 1e-5

MLSTM_CHUNK = 256
TOKEN_TILE = 512
MOBA_HEADS_PER_STEP = 4
LONG_TRIP = 8
HALO_ROWS = 16
SUM_ROWS = 16
GATE_ROWS = 8
SEL_PAD = 8
LANES = 128
NEG_BIG = -1e30
VMEM_LIMIT = 56 * 1024 * 1024
LOG2E = math.log2(math.e)

_NT = (((1,), (1,)), ((), ()))


def _t5_thresholds():
    n = np.arange(0, 2 * REL_MAX_DIST, dtype=np.int64)
    max_exact = REL_BUCKETS // 2
    nf = np.maximum(n, 1).astype(np.float32)
    large = max_exact + (np.log(nf / np.float32(max_exact))
                         / np.float32(math.log(REL_MAX_DIST / max_exact))
                         * np.float32(REL_BUCKETS - max_exact)).astype(np.int32)
    large = np.minimum(large, REL_BUCKETS - 1)
    bucket = np.where(n < max_exact, n, large)
    assert np.all(np.diff(bucket) >= 0)
    return [int(np.argmax(bucket >= k)) for k in range(1, REL_BUCKETS)]


T5_THR = _t5_thresholds()
NEAR_TILES = -(-(T5_THR[-1] + MOBA_BLOCK - 1) // MOBA_BLOCK)
assert NEAR_TILES * MOBA_BLOCK - (MOBA_BLOCK - 1) >= T5_THR[-1]
BIAS_TILES = NEAR_TILES + 1


def _dot(a, b):
    return jnp.dot(a, b, preferred_element_type=f32)


def _dot_nt(a, b):
    return lax.dot_general(a, b, _NT, preferred_element_type=f32)


def _split3(x):
    hi = x.astype(bf16)
    r = x - hi.astype(f32)
    mid = r.astype(bf16)
    lo = (r - mid.astype(f32)).astype(bf16)
    return hi, mid, lo


def _bias_kernel(rb_ref, out_ref):
    h = pl.program_id(0)
    key = lax.broadcasted_iota(jnp.int32, (MOBA_BLOCK, MOBA_BLOCK), 0)
    qry = lax.broadcasted_iota(jnp.int32, (MOBA_BLOCK, MOBA_BLOCK), 1)
    base = qry - key
    for d in range(BIAS_TILES):
        dist = base + d * MOBA_BLOCK
        n = jnp.maximum(dist, 0)
        val = jnp.full((MOBA_BLOCK, MOBA_BLOCK), rb_ref[REL_BUCKETS - 1, h] * LOG2E, f32)
        for k in range(REL_BUCKETS - 2, -1, -1):
            val = jnp.where(n < T5_THR[k], rb_ref[k, h] * LOG2E, val)
        if d == 0:
            val = jnp.where(dist >= 0, val, NEG_BIG)
        out_ref[d] = val


def _bias_tiles(rel_bias):
    return pl.pallas_call(
        _bias_kernel,
        grid=(ATT_HEADS,),
        in_specs=[pl.BlockSpec(memory_space=pltpu.SMEM)],
        out_specs=pl.BlockSpec((None, BIAS_TILES, MOBA_BLOCK, MOBA_BLOCK), lambda h: (h, 0, 0, 0)),
        out_shape=jax.ShapeDtypeStruct((ATT_HEADS, BIAS_TILES, MOBA_BLOCK, MOBA_BLOCK), f32),
        name="bias_tiles",
    )(rel_bias)


def _log_sigmoid(v):
    return jnp.minimum(v, 0.0) - jnp.log1p(jnp.exp(-jnp.abs(v)))


def _inproj_kernel(x_ref, g_ref, wq_ref, wv_ref, wr_ref, wx_ref, cw_ref, cb_ref, wmq_ref, wmk_ref,
                   wmv_ref, wg_ref, bg_ref,
                   qT_ref, vT_ref, pr_ref, xc_ref, mqT_ref, mk_ref, mvT_ref, ga_ref, gb_ref, gl_ref,
                   xpad_ref, *, tm, seq, q_scale, k_scale):
    i = pl.program_id(0)
    W = MLSTM_WIDTH
    L = MLSTM_CHUNK
    x = x_ref[...]
    ms = jnp.mean(x * x, axis=-1, keepdims=True)
    h = (x * lax.rsqrt(ms + RMS_EPS) * g_ref[...]).astype(bf16)

    @pl.when((i * tm) % seq == 0)
    def _():
        xpad_ref[0:HALO_ROWS, :] = jnp.zeros((HALO_ROWS, W), f32)

    for cc in range(W // 512):
        cols = slice(cc * 512, (cc + 1) * 512)
        xpad_ref[HALO_ROWS:HALO_ROWS + tm, cols] = _dot(h, wx_ref[:, cols])

    def att_qv(cc):
        rows = slice(cc * 256, (cc + 1) * 256)
        qt = (_dot(h, wq_ref[:, rows]) * q_scale).T
        vt = _dot(h, wv_ref[:, rows]).T
        for u in range(tm // MOBA_BLOCK):
            cols = slice(u * MOBA_BLOCK, (u + 1) * MOBA_BLOCK)
            qT_ref[u, rows, :] = qt[:, cols].astype(bf16)
            vT_ref[u, rows, :] = vt[:, cols].astype(bf16)

    def token_major(cc):
        cols = slice(cc * 512, (cc + 1) * 512)
        pr_ref[:, cols] = _dot(h, wr_ref[:, cols]).astype(bf16)

    def conv_silu(g):
        sl = slice(g * 256, (g + 1) * 256)
        acc = cb_ref[:, sl] + cw_ref[CONV_WIDTH - 1:CONV_WIDTH, sl] * xpad_ref[HALO_ROWS:HALO_ROWS + tm, sl]
        for j in range(CONV_WIDTH - 1):
            off = HALO_ROWS - (CONV_WIDTH - 1) + j
            acc = acc + cw_ref[j:j + 1, sl] * xpad_ref[off:off + tm, sl]
        xc_ref[:, sl] = (acc * jax.nn.sigmoid(acc)).astype(bf16)
        xpad_ref[0:HALO_ROWS, sl] = xpad_ref[tm:tm + HALO_ROWS, sl]

    k_unscale = 1.0 / k_scale
    gates = [jnp.zeros((tm, LANES), f32)]

    def block_diag_qkv(g):
        sl = slice(g * 256, (g + 1) * 256)
        xc_g = xc_ref[:, sl]
        xm_g = xpad_ref[HALO_ROWS:HALO_ROWS + tm, sl].astype(bf16)
        q = _dot(xc_g, wmq_ref[g])
        v = _dot(xm_g, wmv_ref[g])
        q_bf = q.astype(bf16)
        v_bf = v.astype(bf16)
        kk = (_dot(xc_g, wmk_ref[g]) * k_scale).astype(bf16)
        mk_ref[:, sl] = kk
        qT = q.T.astype(bf16)
        vT = v.T.astype(bf16)
        for u in range(tm // L):
            cols = slice(u * L, (u + 1) * L)
            mqT_ref[u, sl, :] = qT[:, cols]
            mvT_ref[u, sl, :] = vT[:, cols]
        ks = slice(W + g * 256, W + (g + 1) * 256)
        vs = slice(2 * W + g * 256, 2 * W + (g + 1) * 256)
        gates[0] = (gates[0] + _dot(q_bf, wg_ref[sl, :]) + k_unscale * _dot(kk, wg_ref[ks, :])
                    + _dot(v_bf, wg_ref[vs, :]))

    def gate_outputs():
        gates_t = (gates[0] + bg_ref[...]).T
        li = gates_t[0:GATE_ROWS, :] * LOG2E
        lf = _log_sigmoid(gates_t[GATE_ROWS:2 * GATE_ROWS, :]) * LOG2E
        row = lax.broadcasted_iota(jnp.int32, (L, L), 0)
        col = lax.broadcasted_iota(jnp.int32, (L, L), 1)
        upper = jnp.where(row <= col, 1.0, 0.0).astype(bf16)
        gl_ref[...] = li
        for u in range(tm // L):
            cols = slice(u * L, (u + 1) * L)
            b = sum(_dot(part, upper) for part in _split3(lf[:, cols]))
            gb_ref[:, cols] = b
            a_pad = jnp.concatenate([li[:, cols] - b, jnp.zeros((LANES - GATE_ROWS, L), f32)], axis=0)
            ga_ref[cols, :] = a_pad.T[:, 0:GATE_ROWS]

    big = ([functools.partial(token_major, cc) for cc in range(pr_ref.shape[1] // 512)]
           + [functools.partial(att_qv, cc) for cc in range(ATT_WIDTH // 256)])
    conv = [functools.partial(conv_silu, g) for g in range(W // 256)]
    proj = [functools.partial(block_diag_qkv, g) for g in range(W // 256)]
    prep = conv + proj + [gate_outputs]
    assert len(big) >= len(prep)
    for n, task in enumerate(big):
        task()
        if n < len(prep):
            prep[n]()


def _inproj(x2, g_pre, wq, wv, wr, wx, conv_w, conv_b, wmq, wmk, wmv, wg, bg, seq):
    T, D = x2.shape
    W = MLSTM_WIDTH
    tm = TOKEN_TILE
    L = MLSTM_CHUNK
    nblk = tm // MOBA_BLOCK
    ncols = wr.shape[1]
    resident = dict(pipeline_mode=pl.Buffered(1))
    const2 = lambda i: (0, 0)
    const3 = lambda i: (0, 0, 0)
    tok = pl.BlockSpec((tm, W), lambda i: (i, 0))
    att_T = pl.BlockSpec((nblk, ATT_WIDTH, MOBA_BLOCK), lambda i: (i, 0, 0))
    chunk_T = pl.BlockSpec((tm // L, W, L), lambda i: (i, 0, 0))
    rows8 = pl.BlockSpec((GATE_ROWS, tm), lambda i: (0, i))
    return pl.pallas_call(
        functools.partial(_inproj_kernel, tm=tm, seq=seq, q_scale=ATT_HEAD_DIM ** -0.5 * LOG2E,
                          k_scale=MLSTM_HEAD_DIM ** -0.5),
        grid=(T // tm,),
        in_specs=[
            pl.BlockSpec((tm, D), lambda i: (i, 0)),
            pl.BlockSpec((1, D), const2),
            pl.BlockSpec((D, ATT_WIDTH), const2, **resident),
            pl.BlockSpec((D, ATT_WIDTH), const2, **resident),
            pl.BlockSpec((D, ncols), const2, **resident),
            pl.BlockSpec((D, W), const2, **resident),
            pl.BlockSpec((CONV_WIDTH, W), const2),
            pl.BlockSpec((1, W), const2),
            pl.BlockSpec((W // 256, 256, 256), const3, **resident),
            pl.BlockSpec((W // 256, 256, 256), const3, **resident),
            pl.BlockSpec((W // 256, 256, 256), const3, **resident),
            pl.BlockSpec((3 * W, LANES), const2, **resident),
            pl.BlockSpec((1, LANES), const2),
        ],
        out_specs=[att_T, att_T, pl.BlockSpec((tm, ncols), lambda i: (i, 0)),
                   tok, chunk_T, tok, chunk_T,
                   pl.BlockSpec((tm, GATE_ROWS), lambda i: (i, 0)), rows8, rows8],
        out_shape=[
            jax.ShapeDtypeStruct((T // MOBA_BLOCK, ATT_WIDTH, MOBA_BLOCK), bf16),
            jax.ShapeDtypeStruct((T // MOBA_BLOCK, ATT_WIDTH, MOBA_BLOCK), bf16),
            jax.ShapeDtypeStruct((T, ncols), bf16),
            jax.ShapeDtypeStruct((T, W), bf16),
            jax.ShapeDtypeStruct((T // L, W, L), bf16),
            jax.ShapeDtypeStruct((T, W), bf16),
            jax.ShapeDtypeStruct((T // L, W, L), bf16),
            jax.ShapeDtypeStruct((T, GATE_ROWS), f32),
            jax.ShapeDtypeStruct((GATE_ROWS, T), f32),
            jax.ShapeDtypeStruct((GATE_ROWS, T), f32),
        ],
        scratch_shapes=[pltpu.VMEM((HALO_ROWS + tm, W), f32)],
        compiler_params=pltpu.CompilerParams(
            dimension_semantics=("arbitrary",), vmem_limit_bytes=VMEM_LIMIT),
        name="inproj",
    )(x2, g_pre, wq, wv, wr, wx, conv_w, conv_b, wmq, wmk, wmv, wg, bg)


def _mlstm_kernel(qT_ref, k_ref, vT_ref, xc_ref, z_ref, ga_ref, gb_ref, gl_ref, nw_ref, sk_ref,
                  y_ref, ct_ref, m_ref):
    L = MLSTM_CHUNK
    dh = MLSTM_HEAD_DIM

    @pl.when(pl.program_id(1) == 0)
    def _():
        ct_ref[...] = jnp.zeros_like(ct_ref)
        m_ref[...] = jnp.zeros_like(m_ref)

    s_idx = lax.broadcasted_iota(jnp.int32, (L, L), 0)
    t_idx = lax.broadcasted_iota(jnp.int32, (L, L), 1)
    causal = s_idx <= t_idx
    ones_rows = jnp.ones((SUM_ROWS, L), bf16)
    for h in range(MLSTM_HEADS):
        sl = slice(h * dh, (h + 1) * dh)
        qT = qT_ref[sl, :]
        k = k_ref[:, sl]
        vT_ext = jnp.concatenate([vT_ref[sl, :], ones_rows], axis=0)
        a_c = ga_ref[:, h:h + 1]
        b_r = gb_ref[h:h + 1, :]
        li_r = gl_ref[h:h + 1, :]
        b_last = b_r[:, L - 1:L]
        m_prev = m_ref[h][0:1, 0:1]
        ct = ct_ref[h]

        log_d = jnp.where(causal, a_c + b_r, NEG_BIG)
        inter = b_r + m_prev
        m_t = jnp.maximum(inter, jnp.max(log_d, axis=0, keepdims=True))
        sT = _dot(k, qT) * jnp.exp2(log_d - m_t)
        dec = jnp.exp2(inter - m_t)
        num = _dot(vT_ext, sT.astype(bf16)) + dec * _dot(ct.astype(bf16), qT)
        den = num[dh:dh + 1, :]
        hT = num[0:dh, :] * (1.0 / jnp.maximum(jnp.abs(den), jnp.exp2(-m_t)))

        mu = jnp.mean(hT, axis=0, keepdims=True)
        cen = hT - mu
        var = jnp.mean(cen * cen, axis=0, keepdims=True)
        yn = (cen * lax.rsqrt(var + LN_EPS)).T
        z = z_ref[:, sl].astype(f32)
        out = (yn * nw_ref[:, sl] + sk_ref[:, sl] * xc_ref[:, sl].astype(f32)) * (z * jax.nn.sigmoid(z))
        y_ref[:, sl] = out.astype(y_ref.dtype)

        log_w = b_last - b_r + li_r
        m_new = jnp.maximum(b_last + m_prev, jnp.max(log_w, axis=1, keepdims=True))
        vw = vT_ext * jnp.exp2(log_w - m_new).astype(bf16)
        ct_ref[h] = jnp.exp2(b_last + m_prev - m_new) * ct + _dot(vw, k)
        m_ref[h] = jnp.broadcast_to(m_new, m_ref.shape[1:])


def _mlstm(qT, k, vT, xc, pr, ga, gb, gl, mh_norm, skip, batch, seq):
    T, W = k.shape
    L = MLSTM_CHUNK
    nc = seq // L
    z_col = 2
    tok = pl.BlockSpec((L, W), lambda b, c: (b * nc + c, 0))
    chunkT = pl.BlockSpec((None, W, L), lambda b, c: (b * nc + c, 0, 0))
    rows8 = pl.BlockSpec((GATE_ROWS, L), lambda b, c: (0, b * nc + c))
    return pl.pallas_call(
        _mlstm_kernel,
        grid=(batch, nc),
        in_specs=[chunkT, tok, chunkT, tok,
                  pl.BlockSpec((L, W), lambda b, c: (b * nc + c, z_col)),
                  pl.BlockSpec((L, GATE_ROWS), lambda b, c: (b * nc + c, 0)),
                  rows8, rows8,
                  pl.BlockSpec((1, W), lambda b, c: (0, 0)),
                  pl.BlockSpec((1, W), lambda b, c: (0, 0))],
        out_specs=tok,
        out_shape=jax.ShapeDtypeStruct((T, W), bf16),
        scratch_shapes=[pltpu.VMEM((MLSTM_HEADS, MLSTM_HEAD_DIM + SUM_ROWS, MLSTM_HEAD_DIM), f32),
                        pltpu.VMEM((MLSTM_HEADS, 8, LANES), f32)],
        compiler_params=pltpu.CompilerParams(
            dimension_semantics=("arbitrary", "arbitrary"), vmem_limit_bytes=VMEM_LIMIT),
        name="mlstm",
    )(qT, k, vT, xc, pr, ga, gb, gl, mh_norm, skip)


def _moba_kernel(qT_ref, k_ref, vT_ref, g_ref, bias_ref, o_ref, kmean_ref, sel_ref, sa_ref, sb_ref,
                 acc_ref, m_ref, alpha_ref, p_ref, *, nb, heads):
    blk_len = MOBA_BLOCK
    dh = ATT_HEAD_DIM
    own = pl.program_id(2)

    @pl.when(own == 0)
    def _():
        def mean_body(j, c):
            kb = k_ref[pl.ds(pl.multiple_of(j * blk_len, blk_len), blk_len), :].astype(f32)
            kmean_ref[pl.ds(j, 1), :] = jnp.sum(kb, axis=0, keepdims=True) * (1.0 / blk_len)
            return c
        lax.fori_loop(0, nb, mean_body, 0)

    hsl = [slice(hh * dh, (hh + 1) * dh) for hh in range(heads)]

    def score_matmuls(j, dst_ref):
        jc = jnp.minimum(j, own)
        rows = pl.ds(pl.multiple_of(jc * blk_len, blk_len), blk_len)
        tile = jnp.minimum(own - jc, NEAR_TILES)
        for hh in range(heads):
            dst_ref[hh] = _dot(k_ref[rows, hsl[hh]], qT_ref[hsl[hh], :]) + bias_ref[hh, tile]

    blk = lax.broadcasted_iota(jnp.int32, (nb, blk_len), 0)
    past = blk < own
    for hh in range(heads):
        qT = qT_ref[hsl[hh], :]
        km = kmean_ref[:, hsl[hh]]
        km_hi = km.astype(bf16)
        km_lo = (km - km_hi.astype(f32)).astype(bf16)
        gate = _dot(km_hi, qT) + _dot(km_lo, qT)
        g = jnp.where(past, gate, -jnp.inf)
        sel = blk == own
        for _ in range(MOBA_TOPK):
            mx = jnp.max(g, axis=0, keepdims=True)
            first = jnp.min(jnp.where(g == mx, blk, nb), axis=0, keepdims=True)
            pick = blk == first
            sel = sel | (pick & past)
            g = jnp.where(pick, -jnp.inf, g)
        sel_ref[hh, 0:nb, :] = jnp.where(sel, 0.0, NEG_BIG)
        sel_ref[hh, nb:nb + SEL_PAD, :] = jnp.full((SEL_PAD, blk_len), NEG_BIG, f32)

    ones_rows = jnp.ones((SUM_ROWS, blk_len), bf16)
    m_ref[...] = jnp.full(m_ref.shape, NEG_BIG, f32)
    acc_ref[...] = jnp.zeros(acc_ref.shape, f32)

    def softmax(j, src_ref):
        stats = []
        for hh in range(heads):
            m = m_ref[hh]
            selrow = sel_ref[hh, pl.ds(j, 1), :]
            m_new = jnp.maximum(m, jnp.max(src_ref[hh], axis=0, keepdims=True) + selrow)
            p = jnp.exp2(src_ref[hh] - jnp.where(selrow < 0.0, -NEG_BIG, m_new)).astype(bf16)
            m_ref[hh] = m_new
            stats.append((jnp.exp2(m - m_new), p))
        return stats

    def value_update(j, stats):
        for hh in range(heads):
            alpha, p = stats[hh]
            lhs = jnp.concatenate([vT_ref[j, hsl[hh], :], ones_rows], axis=0)
            acc_ref[hh] = alpha * acc_ref[hh] + _dot(lhs, p)

    def keep_pending(stats):
        for hh in range(heads):
            alpha_ref[hh], p_ref[hh] = stats[hh]

    score_buf = (sa_ref, sb_ref)

    def trip(j, n):
        value_update(j, [(alpha_ref[hh], p_ref[hh]) for hh in range(heads)])
        score_matmuls(j + 2, sa_ref)
        for t in range(1, n):
            stats = softmax(j + t, score_buf[t % 2])
            if n > 2:
                score_matmuls(j + 2 + t, score_buf[t % 2])
                value_update(j + t, stats)
            else:
                value_update(j + t, stats)
                score_matmuls(j + 2 + t, score_buf[t % 2])
        keep_pending(softmax(j + n, sa_ref))

    score_matmuls(0, sa_ref)
    score_matmuls(1, sb_ref)
    keep_pending(softmax(0, sa_ref))

    n_blocks = own + 1
    n_long = n_blocks // LONG_TRIP

    def long_body(t, c):
        trip(LONG_TRIP * t, LONG_TRIP)
        return c

    def short_body(t, c):
        trip(LONG_TRIP * n_long + 2 * t, 2)
        return c

    lax.fori_loop(0, n_long, long_body, 0)
    lax.fori_loop(0, (n_blocks - LONG_TRIP * n_long + 1) // 2, short_body, 0)
    for hh in range(heads):
        acc = acc_ref[hh]
        o = (acc[0:dh, :] * (1.0 / acc[dh:dh + 1, :])).T
        gg = g_ref[:, hsl[hh]].astype(f32)
        o_ref[:, hsl[hh]] = (o * (gg * jax.nn.sigmoid(gg))).astype(o_ref.dtype)


def _moba(qT, vT, pr, bias_tiles, batch, seq):
    T = pr.shape[0]
    nb = seq // MOBA_BLOCK
    G = MOBA_HEADS_PER_STEP
    gw = G * ATT_HEAD_DIM
    gate_col0 = ATT_WIDTH // gw
    vT4 = vT.reshape(batch, nb, ATT_WIDTH, MOBA_BLOCK)
    return pl.pallas_call(
        functools.partial(_moba_kernel, nb=nb, heads=G),
        grid=(ATT_HEADS // G, batch, nb),
        in_specs=[
            pl.BlockSpec((None, gw, MOBA_BLOCK), lambda h, b, i: (b * nb + i, h, 0)),
            pl.BlockSpec((seq, gw), lambda h, b, i: (b, h)),
            pl.BlockSpec((None, nb, gw, MOBA_BLOCK), lambda h, b, i: (b, 0, h, 0)),
            pl.BlockSpec((MOBA_BLOCK, gw), lambda h, b, i: (b * nb + i, gate_col0 + h)),
            pl.BlockSpec((G, BIAS_TILES, MOBA_BLOCK, MOBA_BLOCK), lambda h, b, i: (h, 0, 0, 0),
                         pipeline_mode=pl.Buffered(1)),
        ],
        out_specs=pl.BlockSpec((MOBA_BLOCK, gw), lambda h, b, i: (b * nb + i, h)),
        out_shape=jax.ShapeDtypeStruct((T, ATT_WIDTH), bf16),
        scratch_shapes=[pltpu.VMEM((nb, gw), f32),
                        pltpu.VMEM((G, nb + SEL_PAD, MOBA_BLOCK), f32),
                        pltpu.VMEM((G, MOBA_BLOCK, MOBA_BLOCK), f32),
                        pltpu.VMEM((G, MOBA_BLOCK, MOBA_BLOCK), f32),
                        pltpu.VMEM((G, ATT_HEAD_DIM + SUM_ROWS, MOBA_BLOCK), f32),
                        pltpu.VMEM((G, 1, MOBA_BLOCK), f32),
                        pltpu.VMEM((G, 1, MOBA_BLOCK), f32),
                        pltpu.VMEM((G, MOBA_BLOCK, MOBA_BLOCK), bf16)],
        compiler_params=pltpu.CompilerParams(
            dimension_semantics=("arbitrary", "arbitrary", "arbitrary"),
            vmem_limit_bytes=VMEM_LIMIT),
        name="moba",
    )(qT, pr, vT4, pr, bias_tiles)


def _outproj_kernel(ya_ref, ym_ref, wa_ref, wm_ref, g_ref, x_ref, o_ref):
    y = _dot(ya_ref[...], wa_ref[...]) + _dot(ym_ref[...], wm_ref[...])
    ms = jnp.mean(y * y, axis=-1, keepdims=True)
    o_ref[...] = x_ref[...] + y * lax.rsqrt(ms + RMS_EPS) * g_ref[...]


def _outproj(ya, ym, wa, wm, g_post, x2):
    T, D = x2.shape
    tm = TOKEN_TILE
    resident = dict(pipeline_mode=pl.Buffered(1))
    return pl.pallas_call(
        _outproj_kernel,
        grid=(T // tm,),
        in_specs=[
            pl.BlockSpec((tm, ya.shape[1]), lambda i: (i, 0)),
            pl.BlockSpec((tm, ym.shape[1]), lambda i: (i, 0)),
            pl.BlockSpec(wa.shape, lambda i: (0, 0), **resident),
            pl.BlockSpec(wm.shape, lambda i: (0, 0), **resident),
            pl.BlockSpec((1, D), lambda i: (0, 0)),
            pl.BlockSpec((tm, D), lambda i: (i, 0)),
        ],
        out_specs=pl.BlockSpec((tm, D), lambda i: (i, 0)),
        out_shape=jax.ShapeDtypeStruct((T, D), f32),
        compiler_params=pltpu.CompilerParams(
            dimension_semantics=("arbitrary",), vmem_limit_bytes=VMEM_LIMIT),
        name="outproj",
    )(ya, ym, wa, wm, g_post, x2)


def _block_diag_256(w):
    width = w.shape[0] * QKV_BLOCK
    rows = w.reshape(width, QKV_BLOCK)
    col = np.arange(256)
    spread = jnp.asarray(col[None, :] % QKV_BLOCK == np.arange(QKV_BLOCK)[:, None], w.dtype)
    tiled = jnp.dot(rows, spread, precision=lax.Precision.HIGHEST)
    same_block = (np.arange(width)[:, None] % 256) // QKV_BLOCK == col[None, :] // QKV_BLOCK
    dense = jnp.where(jnp.asarray(same_block), tiled, 0.0)
    return dense.reshape(-1, 256, 256)


def _gate_lanes(a):
    nh = MLSTM_HEADS
    out = jnp.zeros(a.shape[:-1] + (LANES,), a.dtype)
    return out.at[..., 0:nh].set(a[..., 0:nh]).at[..., GATE_ROWS:GATE_ROWS + nh].set(a[..., nh:2 * nh])


def _layer(x, rel_bias, g_pre, g_post, w_in, conv_w, conv_b, wq_m, wk_m, wv_m,
           w_if, b_if, mh_norm, skip, w_out):
    batch, seq, d_model = x.shape
    assert seq % TOKEN_TILE == 0 and TOKEN_TILE % MLSTM_CHUNK == 0 and TOKEN_TILE % MOBA_BLOCK == 0
    aw, nh = ATT_WIDTH, MLSTM_HEADS
    x2 = x.reshape(batch * seq, d_model)

    w_bf = w_in.astype(bf16)
    wq = w_bf[:, 0:aw]
    wv = w_bf[:, 2 * aw:3 * aw]
    mw = MLSTM_WIDTH
    wr = jnp.concatenate([w_bf[:, aw:2 * aw], w_bf[:, 3 * aw:4 * aw], w_bf[:, 4 * aw + mw:]], axis=1)
    wx = w_bf[:, 4 * aw:4 * aw + mw]
    wq_bd = _block_diag_256(wq_m).astype(bf16)
    wk_bd = _block_diag_256(wk_m).astype(bf16)
    wv_bd = _block_diag_256(wv_m).astype(bf16)
    wg = _gate_lanes(w_if).astype(bf16)
    bg = _gate_lanes(b_if).reshape(1, LANES)
    w_out_bf = w_out.astype(bf16)

    bias_tiles = _bias_tiles(rel_bias)
    qT, vT, pr, xc, qmT, km, vmT, ga, gb, gl = _inproj(
        x2, g_pre.reshape(1, -1), wq, wv, wr, wx, conv_w, conv_b.reshape(1, -1),
        wq_bd, wk_bd, wv_bd, wg, bg, seq)
    ym = _mlstm(qmT, km, vmT, xc, pr, ga, gb, gl, mh_norm.reshape(1, -1), skip.reshape(1, -1),
                batch, seq)
    ya = _moba(qT, vT, pr, bias_tiles, batch, seq)
    out = _outproj(ya, ym, w_out_bf[0:aw], w_out_bf[aw:], g_post.reshape(1, -1), x2)
    return out.reshape(batch, seq, d_model)


def kernel(x, rel_bias, g_pre, g_post, w_in, conv_w, conv_b, wq_m, wk_m, wv_m, w_if, b_if,
           mh_norm, skip, w_out):
    depth = w_in.shape[0]
    for l in range(depth):
        x = _layer(x, rel_bias, g_pre[l], g_post[l], w_in[l], conv_w[l], conv_b[l], wq_m[l],
                   wk_m[l], wv_m[l], w_if[l], b_if[l], mh_norm[l], skip[l], w_out[l])
    return x
```

```python
import functools
import math

import jax
import jax.numpy as jnp
import numpy as np
from jax import lax
from jax.experimental import pallas as pl
from jax.experimental.pallas import tpu as pltpu

f32 = jnp.float32
bf16 = jnp.bfloat16

ATT_HEADS = 8
ATT_HEAD_DIM = 128
ATT_WIDTH = ATT_HEADS * ATT_HEAD_DIM
MOBA_BLOCK = 256
MOBA_TOPK = 3
REL_BUCKETS = 32
REL_MAX_DIST = 2048
MLSTM_HEADS = 4
MLSTM_WIDTH = 1024
MLSTM_HEAD_DIM = MLSTM_WIDTH // MLSTM_HEADS
QKV_BLOCK = 4
CONV_WIDTH = 4
RMS_EPS = 1e-6
LN_EPS = 1e-5

MLSTM_CHUNK = 256
TOKEN_TILE = 512
MOBA_HEADS_PER_STEP = 4
LONG_TRIP = 8
HALO_ROWS = 16
SUM_ROWS = 16
GATE_ROWS = 8
SEL_PAD = 8
LANES = 128
NEG_BIG = -1e30
VMEM_LIMIT = 56 * 1024 * 1024
LOG2E = math.log2(math.e)

_NT = (((1,), (1,)), ((), ()))


def _t5_thresholds():
    n = np.arange(0, 2 * REL_MAX_DIST, dtype=np.int64)
    max_exact = REL_BUCKETS // 2
    nf = np.maximum(n, 1).astype(np.float32)
    large = max_exact + (np.log(nf / np.float32(max_exact))
                         / np.float32(math.log(REL_MAX_DIST / max_exact))
                         * np.float32(REL_BUCKETS - max_exact)).astype(np.int32)
    large = np.minimum(large, REL_BUCKETS - 1)
    bucket = np.where(n < max_exact, n, large)
    assert np.all(np.diff(bucket) >= 0)
    return [int(np.argmax(bucket >= k)) for k in range(1, REL_BUCKETS)]


T5_THR = _t5_thresholds()
NEAR_TILES = -(-(T5_THR[-1] + MOBA_BLOCK - 1) // MOBA_BLOCK)
assert NEAR_TILES * MOBA_BLOCK - (MOBA_BLOCK - 1) >= T5_THR[-1]
BIAS_TILES = NEAR_TILES + 1


def _dot(a, b):
    return jnp.dot(a, b, preferred_element_type=f32)


def _dot_nt(a, b):
    return lax.dot_general(a, b, _NT, preferred_element_type=f32)


def _split3(x):
    hi = x.astype(bf16)
    r = x - hi.astype(f32)
    mid = r.astype(bf16)
    lo = (r - mid.astype(f32)).astype(bf16)
    return hi, mid, lo


def _bias_kernel(rb_ref, out_ref):
    h = pl.program_id(0)
    key = lax.broadcasted_iota(jnp.int32, (MOBA_BLOCK, MOBA_BLOCK), 0)
    qry = lax.broadcasted_iota(jnp.int32, (MOBA_BLOCK, MOBA_BLOCK), 1)
    base = qry - key
    for d in range(BIAS_TILES):
        dist = base + d * MOBA_BLOCK
        n = jnp.maximum(dist, 0)
        val = jnp.full((MOBA_BLOCK, MOBA_BLOCK), rb_ref[REL_BUCKETS - 1, h] * LOG2E, f32)
        for k in range(REL_BUCKETS - 2, -1, -1):
            val = jnp.where(n < T5_THR[k], rb_ref[k, h] * LOG2E, val)
        if d == 0:
            val = jnp.where(dist >= 0, val, NEG_BIG)
        out_ref[d] = val


def _bias_tiles(rel_bias):
    return pl.pallas_call(
        _bias_kernel,
        grid=(ATT_HEADS,),
        in_specs=[pl.BlockSpec(memory_space=pltpu.SMEM)],
        out_specs=pl.BlockSpec((None, BIAS_TILES, MOBA_BLOCK, MOBA_BLOCK), lambda h: (h, 0, 0, 0)),
        out_shape=jax.ShapeDtypeStruct((ATT_HEADS, BIAS_TILES, MOBA_BLOCK, MOBA_BLOCK), f32),
        name="bias_tiles",
    )(rel_bias)


def _log_sigmoid(v):
    return jnp.minimum(v, 0.0) - jnp.log1p(jnp.exp(-jnp.abs(v)))


def _inproj_kernel(x_ref, g_ref, wq_ref, wv_ref, wr_ref, wx_ref, cw_ref, cb_ref, wmq_ref, wmk_ref,
                   wmv_ref, wg_ref, bg_ref,
                   qT_ref, vT_ref, pr_ref, xc_ref, mqT_ref, mk_ref, mvT_ref, ga_ref, gb_ref, gl_ref,
                   xpad_ref, *, tm, seq, q_scale, k_scale):
    i = pl.program_id(0)
    W = MLSTM_WIDTH
    L = MLSTM_CHUNK
    x = x_ref[...]
    ms = jnp.mean(x * x, axis=-1, keepdims=True)
    h = (x * lax.rsqrt(ms + RMS_EPS) * g_ref[...]).astype(bf16)

    @pl.when((i * tm) % seq == 0)
    def _():
        xpad_ref[0:HALO_ROWS, :] = jnp.zeros((HALO_ROWS, W), f32)

    for cc in range(W // 512):
        cols = slice(cc * 512, (cc + 1) * 512)
        xpad_ref[HALO_ROWS:HALO_ROWS + tm, cols] = _dot(h, wx_ref[:, cols])

    def att_qv(cc):
        rows = slice(cc * 256, (cc + 1) * 256)
        qt = (_dot(h, wq_ref[:, rows]) * q_scale).T
        vt = _dot(h, wv_ref[:, rows]).T
        for u in range(tm // MOBA_BLOCK):
            cols = slice(u * MOBA_BLOCK, (u + 1) * MOBA_BLOCK)
            qT_ref[u, rows, :] = qt[:, cols].astype(bf16)
            vT_ref[u, rows, :] = vt[:, cols].astype(bf16)

    def token_major(cc):
        cols = slice(cc * 512, (cc + 1) * 512)
        pr_ref[:, cols] = _dot(h, wr_ref[:, cols]).astype(bf16)

    def conv_silu(g):
        sl = slice(g * 256, (g + 1) * 256)
        acc = cb_ref[:, sl] + cw_ref[CONV_WIDTH - 1:CONV_WIDTH, sl] * xpad_ref[HALO_ROWS:HALO_ROWS + tm, sl]
        for j in range(CONV_WIDTH - 1):
            off = HALO_ROWS - (CONV_WIDTH - 1) + j
            acc = acc + cw_ref[j:j + 1, sl] * xpad_ref[off:off + tm, sl]
        xc_ref[:, sl] = (acc * jax.nn.sigmoid(acc)).astype(bf16)
        xpad_ref[0:HALO_ROWS, sl] = xpad_ref[tm:tm + HALO_ROWS, sl]

    k_unscale = 1.0 / k_scale
    gates = [jnp.zeros((tm, LANES), f32)]

    def block_diag_qkv(g):
        sl = slice(g * 256, (g + 1) * 256)
        xc_g = xc_ref[:, sl]
        xm_g = xpad_ref[HALO_ROWS:HALO_ROWS + tm, sl].astype(bf16)
        q = _dot(xc_g, wmq_ref[g])
        v = _dot(xm_g, wmv_ref[g])
        q_bf = q.astype(bf16)
        v_bf = v.astype(bf16)
        kk = (_dot(xc_g, wmk_ref[g]) * k_scale).astype(bf16)
        mk_ref[:, sl] = kk
        qT = q.T.astype(bf16)
        vT = v.T.astype(bf16)
        for u in range(tm // L):
            cols = slice(u * L, (u + 1) * L)
            mqT_ref[u, sl, :] = qT[:, cols]
            mvT_ref[u, sl, :] = vT[:, cols]
        ks = slice(W + g * 256, W + (g + 1) * 256)
        vs = slice(2 * W + g * 256, 2 * W + (g + 1) * 256)
        gates[0] = (gates[0] + _dot(q_bf, wg_ref[sl, :]) + k_unscale * _dot(kk, wg_ref[ks, :])
                    + _dot(v_bf, wg_ref[vs, :]))

    def gate_outputs():
        gates_t = (gates[0] + bg_ref[...]).T
        li = gates_t[0:GATE_ROWS, :] * LOG2E
        lf = _log_sigmoid(gates_t[GATE_ROWS:2 * GATE_ROWS, :]) * LOG2E
        row = lax.broadcasted_iota(jnp.int32, (L, L), 0)
        col = lax.broadcasted_iota(jnp.int32, (L, L), 1)
        upper = jnp.where(row <= col, 1.0, 0.0).astype(bf16)
        gl_ref[...] = li
        for u in range(tm // L):
            cols = slice(u * L, (u + 1) * L)
            b = sum(_dot(part, upper) for part in _split3(lf[:, cols]))
            gb_ref[:, cols] = b
            a_pad = jnp.concatenate([li[:, cols] - b, jnp.zeros((LANES - GATE_ROWS, L), f32)], axis=0)
            ga_ref[cols, :] = a_pad.T[:, 0:GATE_ROWS]

    big = ([functools.partial(token_major, cc) for cc in range(pr_ref.shape[1] // 512)]
           + [functools.partial(att_qv, cc) for cc in range(ATT_WIDTH // 256)])
    conv = [functools.partial(conv_silu, g) for g in range(W // 256)]
    proj = [functools.partial(block_diag_qkv, g) for g in range(W // 256)]
    prep = conv + proj + [gate_outputs]
    assert len(big) >= len(prep)
    for n, task in enumerate(big):
        task()
        if n < len(prep):
            prep[n]()


def _inproj(x2, g_pre, wq, wv, wr, wx, conv_w, conv_b, wmq, wmk, wmv, wg, bg, seq):
    T, D = x2.shape
    W = MLSTM_WIDTH
    tm = TOKEN_TILE
    L = MLSTM_CHUNK
    nblk = tm // MOBA_BLOCK
    ncols = wr.shape[1]
    resident = dict(pipeline_mode=pl.Buffered(1))
    const2 = lambda i: (0, 0)
    const3 = lambda i: (0, 0, 0)
    tok = pl.BlockSpec((tm, W), lambda i: (i, 0))
    att_T = pl.BlockSpec((nblk, ATT_WIDTH, MOBA_BLOCK), lambda i: (i, 0, 0))
    chunk_T = pl.BlockSpec((tm // L, W, L), lambda i: (i, 0, 0))
    rows8 = pl.BlockSpec((GATE_ROWS, tm), lambda i: (0, i))
    return pl.pallas_call(
        functools.partial(_inproj_kernel, tm=tm, seq=seq, q_scale=ATT_HEAD_DIM ** -0.5 * LOG2E,
                          k_scale=MLSTM_HEAD_DIM ** -0.5),
        grid=(T // tm,),
        in_specs=[
            pl.BlockSpec((tm, D), lambda i: (i, 0)),
            pl.BlockSpec((1, D), const2),
            pl.BlockSpec((D, ATT_WIDTH), const2, **resident),
            pl.BlockSpec((D, ATT_WIDTH), const2, **resident),
            pl.BlockSpec((D, ncols), const2, **resident),
            pl.BlockSpec((D, W), const2, **resident),
            pl.BlockSpec((CONV_WIDTH, W), const2),
            pl.BlockSpec((1, W), const2),
            pl.BlockSpec((W // 256, 256, 256), const3, **resident),
            pl.BlockSpec((W // 256, 256, 256), const3, **resident),
            pl.BlockSpec((W // 256, 256, 256), const3, **resident),
            pl.BlockSpec((3 * W, LANES), const2, **resident),
            pl.BlockSpec((1, LANES), const2),
        ],
        out_specs=[att_T, att_T, pl.BlockSpec((tm, ncols), lambda i: (i, 0)),
                   tok, chunk_T, tok, chunk_T,
                   pl.BlockSpec((tm, GATE_ROWS), lambda i: (i, 0)), rows8, rows8],
        out_shape=[
            jax.ShapeDtypeStruct((T // MOBA_BLOCK, ATT_WIDTH, MOBA_BLOCK), bf16),
            jax.ShapeDtypeStruct((T // MOBA_BLOCK, ATT_WIDTH, MOBA_BLOCK), bf16),
            jax.ShapeDtypeStruct((T, ncols), bf16),
            jax.ShapeDtypeStruct((T, W), bf16),
            jax.ShapeDtypeStruct((T // L, W, L), bf16),
            jax.ShapeDtypeStruct((T, W), bf16),
            jax.ShapeDtypeStruct((T // L, W, L), bf16),
            jax.ShapeDtypeStruct((T, GATE_ROWS), f32),
            jax.ShapeDtypeStruct((GATE_ROWS, T), f32),
            jax.ShapeDtypeStruct((GATE_ROWS, T), f32),
        ],
        scratch_shapes=[pltpu.VMEM((HALO_ROWS + tm, W), f32)],
        compiler_params=pltpu.CompilerParams(
            dimension_semantics=("arbitrary",), vmem_limit_bytes=VMEM_LIMIT),
        name="inproj",
    )(x2, g_pre, wq, wv, wr, wx, conv_w, conv_b, wmq, wmk, wmv, wg, bg)


def _mlstm_kernel(qT_ref, k_ref, vT_ref, xc_ref, z_ref, ga_ref, gb_ref, gl_ref, nw_ref, sk_ref,
                  y_ref, ct_ref, m_ref):
    L = MLSTM_CHUNK
    dh = MLSTM_HEAD_DIM

    @pl.when(pl.program_id(1) == 0)
    def _():
        ct_ref[...] = jnp.zeros_like(ct_ref)
        m_ref[...] = jnp.zeros_like(m_ref)

    s_idx = lax.broadcasted_iota(jnp.int32, (L, L), 0)
    t_idx = lax.broadcasted_iota(jnp.int32, (L, L), 1)
    causal = s_idx <= t_idx
    ones_rows = jnp.ones((SUM_ROWS, L), bf16)
    for h in range(MLSTM_HEADS):
        sl = slice(h * dh, (h + 1) * dh)
        qT = qT_ref[sl, :]
        k = k_ref[:, sl]
        vT_ext = jnp.concatenate([vT_ref[sl, :], ones_rows], axis=0)
        a_c = ga_ref[:, h:h + 1]
        b_r = gb_ref[h:h + 1, :]
        li_r = gl_ref[h:h + 1, :]
        b_last = b_r[:, L - 1:L]
        m_prev = m_ref[h][0:1, 0:1]
        ct = ct_ref[h]

        log_d = jnp.where(causal, a_c + b_r, NEG_BIG)
        inter = b_r + m_prev
        m_t = jnp.maximum(inter, jnp.max(log_d, axis=0, keepdims=True))
        sT = _dot(k, qT) * jnp.exp2(log_d - m_t)
        dec = jnp.exp2(inter - m_t)
        num = _dot(vT_ext, sT.astype(bf16)) + dec * _dot(ct.astype(bf16), qT)
        den = num[dh:dh + 1, :]
        hT = num[0:dh, :] * (1.0 / jnp.maximum(jnp.abs(den), jnp.exp2(-m_t)))

        mu = jnp.mean(hT, axis=0, keepdims=True)
        cen = hT - mu
        var = jnp.mean(cen * cen, axis=0, keepdims=True)
        yn = (cen * lax.rsqrt(var + LN_EPS)).T
        z = z_ref[:, sl].astype(f32)
        out = (yn * nw_ref[:, sl] + sk_ref[:, sl] * xc_ref[:, sl].astype(f32)) * (z * jax.nn.sigmoid(z))
        y_ref[:, sl] = out.astype(y_ref.dtype)

        log_w = b_last - b_r + li_r
        m_new = jnp.maximum(b_last + m_prev, jnp.max(log_w, axis=1, keepdims=True))
        vw = vT_ext * jnp.exp2(log_w - m_new).astype(bf16)
        ct_ref[h] = jnp.exp2(b_last + m_prev - m_new) * ct + _dot(vw, k)
        m_ref[h] = jnp.broadcast_to(m_new, m_ref.shape[1:])


def _mlstm(qT, k, vT, xc, pr, ga, gb, gl, mh_norm, skip, batch, seq):
    T, W = k.shape
    L = MLSTM_CHUNK
    nc = seq // L
    z_col = 2
    tok = pl.BlockSpec((L, W), lambda b, c: (b * nc + c, 0))
    chunkT = pl.BlockSpec((None, W, L), lambda b, c: (b * nc + c, 0, 0))
    rows8 = pl.BlockSpec((GATE_ROWS, L), lambda b, c: (0, b * nc + c))
    return pl.pallas_call(
        _mlstm_kernel,
        grid=(batch, nc),
        in_specs=[chunkT, tok, chunkT, tok,
                  pl.BlockSpec((L, W), lambda b, c: (b * nc + c, z_col)),
                  pl.BlockSpec((L, GATE_ROWS), lambda b, c: (b * nc + c, 0)),
                  rows8, rows8,
                  pl.BlockSpec((1, W), lambda b, c: (0, 0)),
                  pl.BlockSpec((1, W), lambda b, c: (0, 0))],
        out_specs=tok,
        out_shape=jax.ShapeDtypeStruct((T, W), bf16),
        scratch_shapes=[pltpu.VMEM((MLSTM_HEADS, MLSTM_HEAD_DIM + SUM_ROWS, MLSTM_HEAD_DIM), f32),
                        pltpu.VMEM((MLSTM_HEADS, 8, LANES), f32)],
        compiler_params=pltpu.CompilerParams(
            dimension_semantics=("arbitrary", "arbitrary"), vmem_limit_bytes=VMEM_LIMIT),
        name="mlstm",
    )(qT, k, vT, xc, pr, ga, gb, gl, mh_norm, skip)


def _moba_kernel(qT_ref, k_ref, vT_ref, g_ref, bias_ref, o_ref, kmean_ref, sel_ref, sa_ref, sb_ref,
                 ca_ref, cb_ref, acc_ref, m_ref, alpha_ref, p_ref, *, nb, heads):
    blk_len = MOBA_BLOCK
    dh = ATT_HEAD_DIM
    own = pl.program_id(2)

    @pl.when(own == 0)
    def _():
        def mean_body(j, c):
            kb = k_ref[pl.ds(pl.multiple_of(j * blk_len, blk_len), blk_len), :].astype(f32)
            kmean_ref[pl.ds(j, 1), :] = jnp.sum(kb, axis=0, keepdims=True) * (1.0 / blk_len)
            return c
        lax.fori_loop(0, nb, mean_body, 0)

    hsl = [slice(hh * dh, (hh + 1) * dh) for hh in range(heads)]

    def score_matmuls(j, buf):
        dst_ref, cmax_ref = buf
        jc = jnp.minimum(j, own)
        rows = pl.ds(pl.multiple_of(jc * blk_len, blk_len), blk_len)
        tile = jnp.minimum(own - jc, NEAR_TILES)
        for hh in range(heads):
            s = _dot(k_ref[rows, hsl[hh]], qT_ref[hsl[hh], :]) + bias_ref[hh, tile]
            dst_ref[hh] = s
            cmax_ref[hh] = jnp.max(s, axis=0, keepdims=True)

    blk = lax.broadcasted_iota(jnp.int32, (nb, blk_len), 0)
    past = blk < own
    for hh in range(heads):
        qT = qT_ref[hsl[hh], :]
        km = kmean_ref[:, hsl[hh]]
        km_hi = km.astype(bf16)
        km_lo = (km - km_hi.astype(f32)).astype(bf16)
        gate = _dot(km_hi, qT) + _dot(km_lo, qT)
        g = jnp.where(past, gate, -jnp.inf)
        sel = blk == own
        for _ in range(MOBA_TOPK):
            mx = jnp.max(g, axis=0, keepdims=True)
            first = jnp.min(jnp.where(g == mx, blk, nb), axis=0, keepdims=True)
            pick = blk == first
            sel = sel | (pick & past)
            g = jnp.where(pick, -jnp.inf, g)
        sel_ref[hh, 0:nb, :] = jnp.where(sel, 0.0, NEG_BIG)
        sel_ref[hh, nb:nb + SEL_PAD, :] = jnp.full((SEL_PAD, blk_len), NEG_BIG, f32)

    ones_rows = jnp.ones((SUM_ROWS, blk_len), bf16)
    m_ref[...] = jnp.full(m_ref.shape, NEG_BIG, f32)
    acc_ref[...] = jnp.zeros(acc_ref.shape, f32)

    def softmax(j, buf):
        src_ref, cmax_ref = buf
        stats = []
        for hh in range(heads):
            m = m_ref[hh]
            selrow = sel_ref[hh, pl.ds(j, 1), :]
            m_new = jnp.maximum(m, cmax_ref[hh] + selrow)
            p = jnp.exp2(src_ref[hh] - jnp.where(selrow < 0.0, -NEG_BIG, m_new)).astype(bf16)
            m_ref[hh] = m_new
            stats.append((jnp.exp2(m - m_new), p))
        return stats

    def value_update(j, stats):
        for hh in range(heads):
            alpha, p = stats[hh]
            lhs = jnp.concatenate([vT_ref[j, hsl[hh], :], ones_rows], axis=0)
            acc_ref[hh] = alpha * acc_ref[hh] + _dot(lhs, p)

    def keep_pending(stats):
        for hh in range(heads):
            alpha_ref[hh], p_ref[hh] = stats[hh]

    even_buf, odd_buf = (sa_ref, ca_ref), (sb_ref, cb_ref)
    score_buf = (even_buf, odd_buf)

    def trip(j, n):
        value_update(j, [(alpha_ref[hh], p_ref[hh]) for hh in range(heads)])
        score_matmuls(j + 2, even_buf)
        for t in range(1, n):
            stats = softmax(j + t, score_buf[t % 2])
            if n > 2:
                score_matmuls(j + 2 + t, score_buf[t % 2])
                value_update(j + t, stats)
            else:
                value_update(j + t, stats)
                score_matmuls(j + 2 + t, score_buf[t % 2])
        keep_pending(softmax(j + n, even_buf))

    score_matmuls(0, even_buf)
    score_matmuls(1, odd_buf)
    keep_pending(softmax(0, even_buf))

    n_blocks = own + 1
    n_long = n_blocks // LONG_TRIP

    def long_body(t, c):
        trip(LONG_TRIP * t, LONG_TRIP)
        return c

    def short_body(t, c):
        trip(LONG_TRIP * n_long + 2 * t, 2)
        return c

    lax.fori_loop(0, n_long, long_body, 0)
    lax.fori_loop(0, (n_blocks - LONG_TRIP * n_long + 1) // 2, short_body, 0)
    for hh in range(heads):
        acc = acc_ref[hh]
        o = (acc[0:dh, :] * (1.0 / acc[dh:dh + 1, :])).T
        gg = g_ref[:, hsl[hh]].astype(f32)
        o_ref[:, hsl[hh]] = (o * (gg * jax.nn.sigmoid(gg))).astype(o_ref.dtype)


def _moba(qT, vT, pr, bias_tiles, batch, seq):
    T = pr.shape[0]
    nb = seq // MOBA_BLOCK
    G = MOBA_HEADS_PER_STEP
    gw = G * ATT_HEAD_DIM
    gate_col0 = ATT_WIDTH // gw
    vT4 = vT.reshape(batch, nb, ATT_WIDTH, MOBA_BLOCK)
    return pl.pallas_call(
        functools.partial(_moba_kernel, nb=nb, heads=G),
        grid=(ATT_HEADS // G, batch, nb),
        in_specs=[
            pl.BlockSpec((None, gw, MOBA_BLOCK), lambda h, b, i: (b * nb + i, h, 0)),
            pl.BlockSpec((seq, gw), lambda h, b, i: (b, h)),
            pl.BlockSpec((None, nb, gw, MOBA_BLOCK), lambda h, b, i: (b, 0, h, 0)),
            pl.BlockSpec((MOBA_BLOCK, gw), lambda h, b, i: (b * nb + i, gate_col0 + h)),
            pl.BlockSpec((G, BIAS_TILES, MOBA_BLOCK, MOBA_BLOCK), lambda h, b, i: (h, 0, 0, 0),
                         pipeline_mode=pl.Buffered(1)),
        ],
        out_specs=pl.BlockSpec((MOBA_BLOCK, gw), lambda h, b, i: (b * nb + i, h)),
        out_shape=jax.ShapeDtypeStruct((T, ATT_WIDTH), bf16),
        scratch_shapes=[pltpu.VMEM((nb, gw), f32),
                        pltpu.VMEM((G, nb + SEL_PAD, MOBA_BLOCK), f32),
                        pltpu.VMEM((G, MOBA_BLOCK, MOBA_BLOCK), f32),
                        pltpu.VMEM((G, MOBA_BLOCK, MOBA_BLOCK), f32),
                        pltpu.VMEM((G, 1, MOBA_BLOCK), f32),
                        pltpu.VMEM((G, 1, MOBA_BLOCK), f32),
                        pltpu.VMEM((G, ATT_HEAD_DIM + SUM_ROWS, MOBA_BLOCK), f32),
                        pltpu.VMEM((G, 1, MOBA_BLOCK), f32),
                        pltpu.VMEM((G, 1, MOBA_BLOCK), f32),
                        pltpu.VMEM((G, MOBA_BLOCK, MOBA_BLOCK), bf16)],
        compiler_params=pltpu.CompilerParams(
            dimension_semantics=("arbitrary", "arbitrary", "arbitrary"),
            vmem_limit_bytes=VMEM_LIMIT),
        name="moba",
    )(qT, pr, vT4, pr, bias_tiles)


def _outproj_kernel(ya_ref, ym_ref, wa_ref, wm_ref, g_ref, x_ref, o_ref):
    y = _dot(ya_ref[...], wa_ref[...]) + _dot(ym_ref[...], wm_ref[...])
    ms = jnp.mean(y * y, axis=-1, keepdims=True)
    o_ref[...] = x_ref[...] + y * lax.rsqrt(ms + RMS_EPS) * g_ref[...]


def _outproj(ya, ym, wa, wm, g_post, x2):
    T, D = x2.shape
    tm = TOKEN_TILE
    resident = dict(pipeline_mode=pl.Buffered(1))
    return pl.pallas_call(
        _outproj_kernel,
        grid=(T // tm,),
        in_specs=[
            pl.BlockSpec((tm, ya.shape[1]), lambda i: (i, 0)),
            pl.BlockSpec((tm, ym.shape[1]), lambda i: (i, 0)),
            pl.BlockSpec(wa.shape, lambda i: (0, 0), **resident),
            pl.BlockSpec(wm.shape, lambda i: (0, 0), **resident),
            pl.BlockSpec((1, D), lambda i: (0, 0)),
            pl.BlockSpec((tm, D), lambda i: (i, 0)),
        ],
        out_specs=pl.BlockSpec((tm, D), lambda i: (i, 0)),
        out_shape=jax.ShapeDtypeStruct((T, D), f32),
        compiler_params=pltpu.CompilerParams(
            dimension_semantics=("arbitrary",), vmem_limit_bytes=VMEM_LIMIT),
        name="outproj",
    )(ya, ym, wa, wm, g_post, x2)


def _block_diag_256(w):
    width = w.shape[0] * QKV_BLOCK
    rows = w.reshape(width, QKV_BLOCK)
    col = np.arange(256)
    spread = jnp.asarray(col[None, :] % QKV_BLOCK == np.arange(QKV_BLOCK)[:, None], w.dtype)
    tiled = jnp.dot(rows, spread, precision=lax.Precision.HIGHEST)
    same_block = (np.arange(width)[:, None] % 256) // QKV_BLOCK == col[None, :] // QKV_BLOCK
    dense = jnp.where(jnp.asarray(same_block), tiled, 0.0)
    return dense.reshape(-1, 256, 256)


def _gate_lanes(a):
    nh = MLSTM_HEADS
    out = jnp.zeros(a.shape[:-1] + (LANES,), a.dtype)
    return out.at[..., 0:nh].set(a[..., 0:nh]).at[..., GATE_ROWS:GATE_ROWS + nh].set(a[..., nh:2 * nh])


def _layer(x, rel_bias, g_pre, g_post, w_in, conv_w, conv_b, wq_m, wk_m, wv_m,
           w_if, b_if, mh_norm, skip, w_out):
    batch, seq, d_model = x.shape
    assert seq % TOKEN_TILE == 0 and TOKEN_TILE % MLSTM_CHUNK == 0 and TOKEN_TILE % MOBA_BLOCK == 0
    aw, nh = ATT_WIDTH, MLSTM_HEADS
    x2 = x.reshape(batch * seq, d_model)

    w_bf = w_in.astype(bf16)
    wq = w_bf[:, 0:aw]
    wv = w_bf[:, 2 * aw:3 * aw]
    mw = MLSTM_WIDTH
    wr = jnp.concatenate([w_bf[:, aw:2 * aw], w_bf[:, 3 * aw:4 * aw], w_bf[:, 4 * aw + mw:]], axis=1)
    wx = w_bf[:, 4 * aw:4 * aw + mw]
    wq_bd = _block_diag_256(wq_m).astype(bf16)
    wk_bd = _block_diag_256(wk_m).astype(bf16)
    wv_bd = _block_diag_256(wv_m).astype(bf16)
    wg = _gate_lanes(w_if).astype(bf16)
    bg = _gate_lanes(b_if).reshape(1, LANES)
    w_out_bf = w_out.astype(bf16)

    bias_tiles = _bias_tiles(rel_bias)
    qT, vT, pr, xc, qmT, km, vmT, ga, gb, gl = _inproj(
        x2, g_pre.reshape(1, -1), wq, wv, wr, wx, conv_w, conv_b.reshape(1, -1),
        wq_bd, wk_bd, wv_bd, wg, bg, seq)
    ym = _mlstm(qmT, km, vmT, xc, pr, ga, gb, gl, mh_norm.reshape(1, -1), skip.reshape(1, -1),
                batch, seq)
    ya = _moba(qT, vT, pr, bias_tiles, batch, seq)
    out = _outproj(ya, ym, w_out_bf[0:aw], w_out_bf[aw:], g_post.reshape(1, -1), x2)
    return out.reshape(batch, seq, d_model)


def kernel(x, rel_bias, g_pre, g_post, w_in, conv_w, conv_b, wq_m, wk_m, wv_m, w_if, b_if,
           mh_norm, skip, w_out):
    depth = w_in.shape[0]
    for l in range(depth):
        x = _layer(x, rel_bias, g_pre[l], g_post[l], w_in[l], conv_w[l], conv_b[l], wq_m[l],
                   wk_m[l], wv_m[l], w_if[l], b_if[l], mh_norm[l], skip[l], w_out[l])
    return x
```

```python
import functools
import math

import jax
import jax.numpy as jnp
import numpy as np
from jax import lax
from jax.experimental import pallas as pl
from jax.experimental.pallas import tpu as pltpu

f32 = jnp.float32
bf16 = jnp.bfloat16

ATT_HEADS = 8
ATT_HEAD_DIM = 128
ATT_WIDTH = ATT_HEADS * ATT_HEAD_DIM
MOBA_BLOCK = 256
MOBA_TOPK = 3
REL_BUCKETS = 32
REL_MAX_DIST = 2048
MLSTM_HEADS = 4
MLSTM_WIDTH = 1024
MLSTM_HEAD_DIM = MLSTM_WIDTH // MLSTM_HEADS
QKV_BLOCK = 4
CONV_WIDTH = 4
RMS_EPS = 1e-6
LN_EPS = 1e-5

MLSTM_CHUNK = 256
TOKEN_TILE = 512
MOBA_HEADS_PER_STEP = 4
MLSTM_CHUNKS_PER_STEP = 4
LONG_TRIP = 8
HALO_ROWS = 16
SUM_ROWS = 16
GATE_ROWS = 8
SEL_PAD = 8
LANES = 128
NEG_BIG = -1e30
VMEM_LIMIT = 56 * 1024 * 1024
LOG2E = math.log2(math.e)

_NT = (((1,), (1,)), ((), ()))


def _t5_thresholds():
    n = np.arange(0, 2 * REL_MAX_DIST, dtype=np.int64)
    max_exact = REL_BUCKETS // 2
    nf = np.maximum(n, 1).astype(np.float32)
    large = max_exact + (np.log(nf / np.float32(max_exact))
                         / np.float32(math.log(REL_MAX_DIST / max_exact))
                         * np.float32(REL_BUCKETS - max_exact)).astype(np.int32)
    large = np.minimum(large, REL_BUCKETS - 1)
    bucket = np.where(n < max_exact, n, large)
    assert np.all(np.diff(bucket) >= 0)
    return [int(np.argmax(bucket >= k)) for k in range(1, REL_BUCKETS)]


T5_THR = _t5_thresholds()
NEAR_TILES = -(-(T5_THR[-1] + MOBA_BLOCK - 1) // MOBA_BLOCK)
assert NEAR_TILES * MOBA_BLOCK - (MOBA_BLOCK - 1) >= T5_THR[-1]
BIAS_TILES = NEAR_TILES + 1
TILED_STEPS = 2 * (-(-NEAR_TILES // 2))
FIRST_BARE_STEP = TILED_STEPS + 2
assert FIRST_BARE_STEP >= NEAR_TILES


def _dot(a, b):
    return jnp.dot(a, b, preferred_element_type=f32)


def _dot_nt(a, b):
    return lax.dot_general(a, b, _NT, preferred_element_type=f32)


def _split3(x):
    hi = x.astype(bf16)
    r = x - hi.astype(f32)
    mid = r.astype(bf16)
    lo = (r - mid.astype(f32)).astype(bf16)
    return hi, mid, lo


def _bias_kernel(rb_ref, out_ref):
    h = pl.program_id(0)
    key = lax.broadcasted_iota(jnp.int32, (MOBA_BLOCK, MOBA_BLOCK), 0)
    qry = lax.broadcasted_iota(jnp.int32, (MOBA_BLOCK, MOBA_BLOCK), 1)
    base = qry - key
    for d in range(BIAS_TILES):
        dist = base + d * MOBA_BLOCK
        n = jnp.maximum(dist, 0)
        val = jnp.full((MOBA_BLOCK, MOBA_BLOCK), rb_ref[REL_BUCKETS - 1, h] * LOG2E, f32)
        for k in range(REL_BUCKETS - 2, -1, -1):
            val = jnp.where(n < T5_THR[k], rb_ref[k, h] * LOG2E, val)
        if d == 0:
            val = jnp.where(dist >= 0, val, NEG_BIG)
        out_ref[d] = val


def _bias_tiles(rel_bias):
    return pl.pallas_call(
        _bias_kernel,
        grid=(ATT_HEADS,),
        in_specs=[pl.BlockSpec(memory_space=pltpu.SMEM)],
        out_specs=pl.BlockSpec((None, BIAS_TILES, MOBA_BLOCK, MOBA_BLOCK), lambda h: (h, 0, 0, 0)),
        out_shape=jax.ShapeDtypeStruct((ATT_HEADS, BIAS_TILES, MOBA_BLOCK, MOBA_BLOCK), f32),
        name="bias_tiles",
    )(rel_bias)


def _log_sigmoid(v):
    return jnp.minimum(v, 0.0) - jnp.log1p(jnp.exp(-jnp.abs(v)))


def _inproj_kernel(x_ref, g_ref, wq_ref, wv_ref, wr_ref, wx_ref, cw_ref, cb_ref, wmq_ref, wmk_ref,
                   wmv_ref, wg_ref, bg_ref,
                   qT_ref, vT_ref, pr_ref, xc_ref, mqT_ref, mk_ref, mvT_ref, ga_ref, gb_ref, gl_ref,
                   xpad_ref, *, tm, seq, q_scale, k_scale):
    i = pl.program_id(0)
    W = MLSTM_WIDTH
    L = MLSTM_CHUNK
    x = x_ref[...]
    ms = jnp.mean(x * x, axis=-1, keepdims=True)
    h = (x * lax.rsqrt(ms + RMS_EPS) * g_ref[...]).astype(bf16)

    @pl.when((i * tm) % seq == 0)
    def _():
        xpad_ref[0:HALO_ROWS, :] = jnp.zeros((HALO_ROWS, W), f32)

    for cc in range(W // 512):
        cols = slice(cc * 512, (cc + 1) * 512)
        xpad_ref[HALO_ROWS:HALO_ROWS + tm, cols] = _dot(h, wx_ref[:, cols])

    def att_qv(cc):
        rows = slice(cc * 256, (cc + 1) * 256)
        qt = (_dot(h, wq_ref[:, rows]) * q_scale).T
        vt = _dot(h, wv_ref[:, rows]).T
        for u in range(tm // MOBA_BLOCK):
            cols = slice(u * MOBA_BLOCK, (u + 1) * MOBA_BLOCK)
            qT_ref[u, rows, :] = qt[:, cols].astype(bf16)
            vT_ref[u, rows, :] = vt[:, cols].astype(bf16)

    def token_major(cc):
        cols = slice(cc * 512, (cc + 1) * 512)
        pr_ref[:, cols] = _dot(h, wr_ref[:, cols]).astype(bf16)

    def conv_silu(g):
        sl = slice(g * 256, (g + 1) * 256)
        acc = cb_ref[:, sl] + cw_ref[CONV_WIDTH - 1:CONV_WIDTH, sl] * xpad_ref[HALO_ROWS:HALO_ROWS + tm, sl]
        for j in range(CONV_WIDTH - 1):
            off = HALO_ROWS - (CONV_WIDTH - 1) + j
            acc = acc + cw_ref[j:j + 1, sl] * xpad_ref[off:off + tm, sl]
        xc_ref[:, sl] = (acc * jax.nn.sigmoid(acc)).astype(bf16)
        xpad_ref[0:HALO_ROWS, sl] = xpad_ref[tm:tm + HALO_ROWS, sl]

    k_unscale = 1.0 / k_scale
    gates = [jnp.zeros((tm, LANES), f32)]

    def block_diag_qkv(g):
        sl = slice(g * 256, (g + 1) * 256)
        xc_g = xc_ref[:, sl]
        xm_g = xpad_ref[HALO_ROWS:HALO_ROWS + tm, sl].astype(bf16)
        q = _dot(xc_g, wmq_ref[g])
        v = _dot(xm_g, wmv_ref[g])
        q_bf = q.astype(bf16)
        v_bf = v.astype(bf16)
        kk = (_dot(xc_g, wmk_ref[g]) * k_scale).astype(bf16)
        mk_ref[:, sl] = kk
        qT = q.T.astype(bf16)
        vT = v.T.astype(bf16)
        for u in range(tm // L):
            cols = slice(u * L, (u + 1) * L)
            mqT_ref[u, sl, :] = qT[:, cols]
            mvT_ref[u, sl, :] = vT[:, cols]
        ks = slice(W + g * 256, W + (g + 1) * 256)
        vs = slice(2 * W + g * 256, 2 * W + (g + 1) * 256)
        gates[0] = (gates[0] + _dot(q_bf, wg_ref[sl, :]) + k_unscale * _dot(kk, wg_ref[ks, :])
                    + _dot(v_bf, wg_ref[vs, :]))

    def gate_outputs():
        gates_t = (gates[0] + bg_ref[...]).T
        li = gates_t[0:GATE_ROWS, :] * LOG2E
        lf = _log_sigmoid(gates_t[GATE_ROWS:2 * GATE_ROWS, :]) * LOG2E
        row = lax.broadcasted_iota(jnp.int32, (L, L), 0)
        col = lax.broadcasted_iota(jnp.int32, (L, L), 1)
        upper = jnp.where(row <= col, 1.0, 0.0).astype(bf16)
        gl_ref[...] = li
        for u in range(tm // L):
            cols = slice(u * L, (u + 1) * L)
            b = sum(_dot(part, upper) for part in _split3(lf[:, cols]))
            gb_ref[:, cols] = b
            a_pad = jnp.concatenate([li[:, cols] - b, jnp.zeros((LANES - GATE_ROWS, L), f32)], axis=0)
            ga_ref[cols, :] = a_pad.T[:, 0:GATE_ROWS]

    big = ([functools.partial(token_major, cc) for cc in range(pr_ref.shape[1] // 512)]
           + [functools.partial(att_qv, cc) for cc in range(ATT_WIDTH // 256)])
    conv = [functools.partial(conv_silu, g) for g in range(W // 256)]
    proj = [functools.partial(block_diag_qkv, g) for g in range(W // 256)]
    prep = conv + proj + [gate_outputs]
    assert len(big) >= len(prep)
    for n, task in enumerate(big):
        task()
        if n < len(prep):
            prep[n]()


def _inproj(x2, g_pre, wq, wv, wr, wx, conv_w, conv_b, wmq, wmk, wmv, wg, bg, seq):
    T, D = x2.shape
    W = MLSTM_WIDTH
    tm = TOKEN_TILE
    L = MLSTM_CHUNK
    nblk = tm // MOBA_BLOCK
    ncols = wr.shape[1]
    resident = dict(pipeline_mode=pl.Buffered(1))
    const2 = lambda i: (0, 0)
    const3 = lambda i: (0, 0, 0)
    tok = pl.BlockSpec((tm, W), lambda i: (i, 0))
    att_T = pl.BlockSpec((nblk, ATT_WIDTH, MOBA_BLOCK), lambda i: (i, 0, 0))
    chunk_T = pl.BlockSpec((tm // L, W, L), lambda i: (i, 0, 0))
    rows8 = pl.BlockSpec((GATE_ROWS, tm), lambda i: (0, i))
    return pl.pallas_call(
        functools.partial(_inproj_kernel, tm=tm, seq=seq, q_scale=ATT_HEAD_DIM ** -0.5 * LOG2E,
                          k_scale=MLSTM_HEAD_DIM ** -0.5),
        grid=(T // tm,),
        in_specs=[
            pl.BlockSpec((tm, D), lambda i: (i, 0)),
            pl.BlockSpec((1, D), const2),
            pl.BlockSpec((D, ATT_WIDTH), const2, **resident),
            pl.BlockSpec((D, ATT_WIDTH), const2, **resident),
            pl.BlockSpec((D, ncols), const2, **resident),
            pl.BlockSpec((D, W), const2, **resident),
            pl.BlockSpec((CONV_WIDTH, W), const2),
            pl.BlockSpec((1, W), const2),
            pl.BlockSpec((W // 256, 256, 256), const3, **resident),
            pl.BlockSpec((W // 256, 256, 256), const3, **resident),
            pl.BlockSpec((W // 256, 256, 256), const3, **resident),
            pl.BlockSpec((3 * W, LANES), const2, **resident),
            pl.BlockSpec((1, LANES), const2),
        ],
        out_specs=[att_T, att_T, pl.BlockSpec((tm, ncols), lambda i: (i, 0)),
                   tok, chunk_T, tok, chunk_T,
                   pl.BlockSpec((tm, GATE_ROWS), lambda i: (i, 0)), rows8, rows8],
        out_shape=[
            jax.ShapeDtypeStruct((T // MOBA_BLOCK, ATT_WIDTH, MOBA_BLOCK), bf16),
            jax.ShapeDtypeStruct((T // MOBA_BLOCK, ATT_WIDTH, MOBA_BLOCK), bf16),
            jax.ShapeDtypeStruct((T, ncols), bf16),
            jax.ShapeDtypeStruct((T, W), bf16),
            jax.ShapeDtypeStruct((T // L, W, L), bf16),
            jax.ShapeDtypeStruct((T, W), bf16),
            jax.ShapeDtypeStruct((T // L, W, L), bf16),
            jax.ShapeDtypeStruct((T, GATE_ROWS), f32),
            jax.ShapeDtypeStruct((GATE_ROWS, T), f32),
            jax.ShapeDtypeStruct((GATE_ROWS, T), f32),
        ],
        scratch_shapes=[pltpu.VMEM((HALO_ROWS + tm, W), f32)],
        compiler_params=pltpu.CompilerParams(
            dimension_semantics=("arbitrary",), vmem_limit_bytes=VMEM_LIMIT),
        name="inproj",
    )(x2, g_pre, wq, wv, wr, wx, conv_w, conv_b, wmq, wmk, wmv, wg, bg)


def _mlstm_kernel(qT_ref, k_ref, vT_ref, xc_ref, z_ref, ga_ref, gb_ref, gl_ref, nw_ref, sk_ref,
                  y_ref, ct_ref, m_ref):
    L = MLSTM_CHUNK
    dh = MLSTM_HEAD_DIM

    @pl.when(pl.program_id(1) == 0)
    def _():
        ct_ref[...] = jnp.zeros_like(ct_ref)
        m_ref[...] = jnp.zeros_like(m_ref)

    s_idx = lax.broadcasted_iota(jnp.int32, (L, L), 0)
    t_idx = lax.broadcasted_iota(jnp.int32, (L, L), 1)
    causal = s_idx <= t_idx
    ones_rows = jnp.ones((SUM_ROWS, L), bf16)
    for u, h in ((u, h) for u in range(MLSTM_CHUNKS_PER_STEP) for h in range(MLSTM_HEADS)):
        sl = slice(h * dh, (h + 1) * dh)
        tok = slice(u * L, (u + 1) * L)
        qT = qT_ref[u, sl, :]
        k = k_ref[tok, sl]
        vT_ext = jnp.concatenate([vT_ref[u, sl, :], ones_rows], axis=0)
        a_c = ga_ref[tok, h:h + 1]
        b_r = gb_ref[h:h + 1, tok]
        li_r = gl_ref[h:h + 1, tok]
        b_last = b_r[:, L - 1:L]
        m_prev = m_ref[h][0:1, 0:1]
        ct = ct_ref[h]

        log_d = jnp.where(causal, a_c + b_r, NEG_BIG)
        inter = b_r + m_prev
        m_t = jnp.maximum(inter, jnp.max(log_d, axis=0, keepdims=True))
        sT = _dot(k, qT) * jnp.exp2(log_d - m_t)
        dec = jnp.exp2(inter - m_t)
        num = _dot(vT_ext, sT.astype(bf16)) + dec * _dot(ct.astype(bf16), qT)
        den = num[dh:dh + 1, :]
        hT = num[0:dh, :] * (1.0 / jnp.maximum(jnp.abs(den), jnp.exp2(-m_t)))

        mu = jnp.mean(hT, axis=0, keepdims=True)
        cen = hT - mu
        var = jnp.mean(cen * cen, axis=0, keepdims=True)
        yn = (cen * lax.rsqrt(var + LN_EPS)).T
        z = z_ref[tok, sl].astype(f32)
        out = (yn * nw_ref[:, sl] + sk_ref[:, sl] * xc_ref[tok, sl].astype(f32)) * (z * jax.nn.sigmoid(z))
        y_ref[tok, sl] = out.astype(y_ref.dtype)

        log_w = b_last - b_r + li_r
        m_new = jnp.maximum(b_last + m_prev, jnp.max(log_w, axis=1, keepdims=True))
        vw = vT_ext * jnp.exp2(log_w - m_new).astype(bf16)
        ct_ref[h] = jnp.exp2(b_last + m_prev - m_new) * ct + _dot(vw, k)
        m_ref[h] = jnp.broadcast_to(m_new, m_ref.shape[1:])


def _mlstm(qT, k, vT, xc, pr, ga, gb, gl, mh_norm, skip, batch, seq):
    T, W = k.shape
    cps = MLSTM_CHUNKS_PER_STEP
    rows = cps * MLSTM_CHUNK
    assert seq % rows == 0
    nc = seq // rows
    z_col = 2
    tok = pl.BlockSpec((rows, W), lambda b, c: (b * nc + c, 0))
    chunkT = pl.BlockSpec((cps, W, MLSTM_CHUNK), lambda b, c: (b * nc + c, 0, 0))
    rows8 = pl.BlockSpec((GATE_ROWS, rows), lambda b, c: (0, b * nc + c))
    return pl.pallas_call(
        _mlstm_kernel,
        grid=(batch, nc),
        in_specs=[chunkT, tok, chunkT, tok,
                  pl.BlockSpec((rows, W), lambda b, c: (b * nc + c, z_col)),
                  pl.BlockSpec((rows, GATE_ROWS), lambda b, c: (b * nc + c, 0)),
                  rows8, rows8,
                  pl.BlockSpec((1, W), lambda b, c: (0, 0)),
                  pl.BlockSpec((1, W), lambda b, c: (0, 0))],
        out_specs=tok,
        out_shape=jax.ShapeDtypeStruct((T, W), bf16),
        scratch_shapes=[pltpu.VMEM((MLSTM_HEADS, MLSTM_HEAD_DIM + SUM_ROWS, MLSTM_HEAD_DIM), f32),
                        pltpu.VMEM((MLSTM_HEADS, 8, LANES), f32)],
        compiler_params=pltpu.CompilerParams(
            dimension_semantics=("arbitrary", "arbitrary"), vmem_limit_bytes=VMEM_LIMIT),
        name="mlstm",
    )(qT, k, vT, xc, pr, ga, gb, gl, mh_norm, skip)


def _moba_kernel(rb_ref, qT_ref, k_ref, vT_ref, g_ref, bias_ref, o_ref, kmean_ref, sel_ref, sa_ref, sb_ref,
                 ca_ref, cb_ref, acc_ref, m_ref, alpha_ref, p_ref, *, nb, heads):
    blk_len = MOBA_BLOCK
    dh = ATT_HEAD_DIM
    own = pl.program_id(2)

    @pl.when(own == 0)
    def _():
        def mean_body(j, c):
            kb = k_ref[pl.ds(pl.multiple_of(j * blk_len, blk_len), blk_len), :].astype(f32)
            kmean_ref[pl.ds(j, 1), :] = jnp.sum(kb, axis=0, keepdims=True) * (1.0 / blk_len)
            return c
        lax.fori_loop(0, nb, mean_body, 0)

    hsl = [slice(hh * dh, (hh + 1) * dh) for hh in range(heads)]

    def score_matmuls(i, buf, with_tile):
        dst_ref, cmax_ref = buf
        ic = jnp.minimum(i, own)
        rows = pl.ds(pl.multiple_of((own - ic) * blk_len, blk_len), blk_len)
        for hh in range(heads):
            s = _dot(k_ref[rows, hsl[hh]], qT_ref[hsl[hh], :])
            if with_tile:
                s = s + bias_ref[hh, jnp.minimum(ic, NEAR_TILES)]
            dst_ref[hh] = s
            cmax_ref[hh] = jnp.max(s, axis=0, keepdims=True)

    def select_blocks():
        blk = lax.broadcasted_iota(jnp.int32, (nb, blk_len), 0)
        past = blk < own
        for hh in range(heads):
            qT = qT_ref[hsl[hh], :]
            km = kmean_ref[:, hsl[hh]]
            km_hi = km.astype(bf16)
            km_lo = (km - km_hi.astype(f32)).astype(bf16)
            gate = _dot(km_hi, qT) + _dot(km_lo, qT)
            g = jnp.where(past, gate, -jnp.inf)
            sel = blk == own
            for _ in range(MOBA_TOPK):
                mx = jnp.max(g, axis=0, keepdims=True)
                first = jnp.min(jnp.where(g == mx, blk, nb), axis=0, keepdims=True)
                pick = blk == first
                sel = sel | (pick & past)
                g = jnp.where(pick, -jnp.inf, g)
            sel_ref[hh, 0:nb, :] = jnp.where(sel, 0.0, NEG_BIG)
            sel_ref[hh, nb:nb + SEL_PAD, :] = jnp.full((SEL_PAD, blk_len), NEG_BIG, f32)

    ones_rows = jnp.ones((SUM_ROWS, blk_len), bf16)
    m_ref[...] = jnp.full(m_ref.shape, NEG_BIG, f32)
    acc_ref[...] = jnp.zeros(acc_ref.shape, f32)

    head0 = pl.program_id(0) * heads
    far_bias = [rb_ref[REL_BUCKETS - 1, head0 + hh] * LOG2E for hh in range(heads)]

    def softmax(i, buf):
        src_ref, cmax_ref = buf
        mask_row = jnp.where(i <= own, own - i, nb)
        stats = []
        for hh in range(heads):
            m = m_ref[hh]
            shift = jnp.where(i >= FIRST_BARE_STEP, far_bias[hh], 0.0)
            selrow = sel_ref[hh, pl.ds(mask_row, 1), :]
            m_new = jnp.maximum(m, cmax_ref[hh] + (selrow + shift))
            p = jnp.exp2(src_ref[hh] - jnp.where(selrow < 0.0, -NEG_BIG, m_new - shift)).astype(bf16)
            m_ref[hh] = m_new
            stats.append((jnp.exp2(m - m_new), p))
        return stats

    def value_update(i, stats):
        block = jnp.maximum(own - i, 0)
        for hh in range(heads):
            alpha, p = stats[hh]
            lhs = jnp.concatenate([vT_ref[block, hsl[hh], :], ones_rows], axis=0)
            acc_ref[hh] = alpha * acc_ref[hh] + _dot(lhs, p)

    def keep_pending(stats):
        for hh in range(heads):
            alpha_ref[hh], p_ref[hh] = stats[hh]

    even_buf, odd_buf = (sa_ref, ca_ref), (sb_ref, cb_ref)
    score_buf = (even_buf, odd_buf)

    def trip(i, n, with_tile):
        value_update(i, [(alpha_ref[hh], p_ref[hh]) for hh in range(heads)])
        score_matmuls(i + 2, even_buf, with_tile)
        for t in range(1, n):
            stats = softmax(i + t, score_buf[t % 2])
            if n > 2:
                score_matmuls(i + 2 + t, score_buf[t % 2], with_tile)
                value_update(i + t, stats)
            else:
                value_update(i + t, stats)
                score_matmuls(i + 2 + t, score_buf[t % 2], with_tile)
        keep_pending(softmax(i + n, even_buf))

    select_blocks()
    score_matmuls(0, even_buf, True)
    score_matmuls(1, odd_buf, True)
    keep_pending(softmax(0, even_buf))

    n_steps = own + 1
    n_bare = jnp.maximum(n_steps - TILED_STEPS, 0)
    n_long = n_bare // LONG_TRIP

    whole_tiled = n_steps >= TILED_STEPS

    def tiled_long_body(t, c):
        trip(0, TILED_STEPS, True)
        return c

    def tiled_short_body(t, c):
        trip(2 * t, 2, True)
        return c

    def long_body(t, c):
        trip(TILED_STEPS + LONG_TRIP * t, LONG_TRIP, False)
        return c

    def short_body(t, c):
        trip(TILED_STEPS + LONG_TRIP * n_long + 2 * t, 2, False)
        return c

    lax.fori_loop(0, jnp.where(whole_tiled, 1, 0), tiled_long_body, 0)
    lax.fori_loop(0, jnp.where(whole_tiled, 0, (n_steps + 1) // 2), tiled_short_body, 0)
    lax.fori_loop(0, n_long, long_body, 0)
    lax.fori_loop(0, (n_bare - LONG_TRIP * n_long + 1) // 2, short_body, 0)
    for hh in range(heads):
        acc = acc_ref[hh]
        o = (acc[0:dh, :] * (1.0 / acc[dh:dh + 1, :])).T
        gg = g_ref[:, hsl[hh]].astype(f32)
        o_ref[:, hsl[hh]] = (o * (gg * jax.nn.sigmoid(gg))).astype(o_ref.dtype)


def _moba(rel_bias, qT, vT, pr, bias_tiles, batch, seq):
    T = pr.shape[0]
    nb = seq // MOBA_BLOCK
    G = MOBA_HEADS_PER_STEP
    gw = G * ATT_HEAD_DIM
    gate_col0 = ATT_WIDTH // gw
    vT4 = vT.reshape(batch, nb, ATT_WIDTH, MOBA_BLOCK)
    return pl.pallas_call(
        functools.partial(_moba_kernel, nb=nb, heads=G),
        grid=(ATT_HEADS // G, batch, nb),
        in_specs=[
            pl.BlockSpec(memory_space=pltpu.SMEM),
            pl.BlockSpec((None, gw, MOBA_BLOCK), lambda h, b, i: (b * nb + i, h, 0)),
            pl.BlockSpec((seq, gw), lambda h, b, i: (b, h)),
            pl.BlockSpec((None, nb, gw, MOBA_BLOCK), lambda h, b, i: (b, 0, h, 0)),
            pl.BlockSpec((MOBA_BLOCK, gw), lambda h, b, i: (b * nb + i, gate_col0 + h)),
            pl.BlockSpec((G, BIAS_TILES, MOBA_BLOCK, MOBA_BLOCK), lambda h, b, i: (h, 0, 0, 0),
                         pipeline_mode=pl.Buffered(1)),
        ],
        out_specs=pl.BlockSpec((MOBA_BLOCK, gw), lambda h, b, i: (b * nb + i, h)),
        out_shape=jax.ShapeDtypeStruct((T, ATT_WIDTH), bf16),
        scratch_shapes=[pltpu.VMEM((nb, gw), f32),
                        pltpu.VMEM((G, nb + SEL_PAD, MOBA_BLOCK), f32),
                        pltpu.VMEM((G, MOBA_BLOCK, MOBA_BLOCK), f32),
                        pltpu.VMEM((G, MOBA_BLOCK, MOBA_BLOCK), f32),
                        pltpu.VMEM((G, 1, MOBA_BLOCK), f32),
                        pltpu.VMEM((G, 1, MOBA_BLOCK), f32),
                        pltpu.VMEM((G, ATT_HEAD_DIM + SUM_ROWS, MOBA_BLOCK), f32),
                        pltpu.VMEM((G, 1, MOBA_BLOCK), f32),
                        pltpu.VMEM((G, 1, MOBA_BLOCK), f32),
                        pltpu.VMEM((G, MOBA_BLOCK, MOBA_BLOCK), bf16)],
        compiler_params=pltpu.CompilerParams(
            dimension_semantics=("arbitrary", "arbitrary", "arbitrary"),
            vmem_limit_bytes=VMEM_LIMIT),
        name="moba",
    )(rel_bias, qT, pr, vT4, pr, bias_tiles)


def _outproj_kernel(ya_ref, ym_ref, wa_ref, wm_ref, g_ref, x_ref, o_ref):
    y = _dot(ya_ref[...], wa_ref[...]) + _dot(ym_ref[...], wm_ref[...])
    ms = jnp.mean(y * y, axis=-1, keepdims=True)
    o_ref[...] = x_ref[...] + y * lax.rsqrt(ms + RMS_EPS) * g_ref[...]


def _outproj(ya, ym, wa, wm, g_post, x2):
    T, D = x2.shape
    tm = TOKEN_TILE
    resident = dict(pipeline_mode=pl.Buffered(1))
    return pl.pallas_call(
        _outproj_kernel,
        grid=(T // tm,),
        in_specs=[
            pl.BlockSpec((tm, ya.shape[1]), lambda i: (i, 0)),
            pl.BlockSpec((tm, ym.shape[1]), lambda i: (i, 0)),
            pl.BlockSpec(wa.shape, lambda i: (0, 0), **resident),
            pl.BlockSpec(wm.shape, lambda i: (0, 0), **resident),
            pl.BlockSpec((1, D), lambda i: (0, 0)),
            pl.BlockSpec((tm, D), lambda i: (i, 0)),
        ],
        out_specs=pl.BlockSpec((tm, D), lambda i: (i, 0)),
        out_shape=jax.ShapeDtypeStruct((T, D), f32),
        compiler_params=pltpu.CompilerParams(
            dimension_semantics=("arbitrary",), vmem_limit_bytes=VMEM_LIMIT),
        name="outproj",
    )(ya, ym, wa, wm, g_post, x2)


def _block_diag_256(w):
    width = w.shape[0] * QKV_BLOCK
    rows = w.reshape(width, QKV_BLOCK)
    col = np.arange(256)
    spread = jnp.asarray(col[None, :] % QKV_BLOCK == np.arange(QKV_BLOCK)[:, None], w.dtype)
    tiled = jnp.dot(rows, spread, precision=lax.Precision.HIGHEST)
    same_block = (np.arange(width)[:, None] % 256) // QKV_BLOCK == col[None, :] // QKV_BLOCK
    dense = jnp.where(jnp.asarray(same_block), tiled, 0.0)
    return dense.reshape(-1, 256, 256)


def _gate_lanes(a):
    nh = MLSTM_HEADS
    out = jnp.zeros(a.shape[:-1] + (LANES,), a.dtype)
    return out.at[..., 0:nh].set(a[..., 0:nh]).at[..., GATE_ROWS:GATE_ROWS + nh].set(a[..., nh:2 * nh])


def _layer(x, rel_bias, g_pre, g_post, w_in, conv_w, conv_b, wq_m, wk_m, wv_m,
           w_if, b_if, mh_norm, skip, w_out):
    batch, seq, d_model = x.shape
    assert seq % TOKEN_TILE == 0 and TOKEN_TILE % MLSTM_CHUNK == 0 and TOKEN_TILE % MOBA_BLOCK == 0
    aw, nh = ATT_WIDTH, MLSTM_HEADS
    x2 = x.reshape(batch * seq, d_model)

    w_bf = w_in.astype(bf16)
    wq = w_bf[:, 0:aw]
    wv = w_bf[:, 2 * aw:3 * aw]
    mw = MLSTM_WIDTH
    wr = jnp.concatenate([w_bf[:, aw:2 * aw], w_bf[:, 3 * aw:4 * aw], w_bf[:, 4 * aw + mw:]], axis=1)
    wx = w_bf[:, 4 * aw:4 * aw + mw]
    wq_bd = _block_diag_256(wq_m).astype(bf16)
    wk_bd = _block_diag_256(wk_m).astype(bf16)
    wv_bd = _block_diag_256(wv_m).astype(bf16)
    wg = _gate_lanes(w_if).astype(bf16)
    bg = _gate_lanes(b_if).reshape(1, LANES)
    w_out_bf = w_out.astype(bf16)

    bias_tiles = _bias_tiles(rel_bias)
    qT, vT, pr, xc, qmT, km, vmT, ga, gb, gl = _inproj(
        x2, g_pre.reshape(1, -1), wq, wv, wr, wx, conv_w, conv_b.reshape(1, -1),
        wq_bd, wk_bd, wv_bd, wg, bg, seq)
    ym = _mlstm(qmT, km, vmT, xc, pr, ga, gb, gl, mh_norm.reshape(1, -1), skip.reshape(1, -1),
                batch, seq)
    ya = _moba(rel_bias, qT, vT, pr, bias_tiles, batch, seq)
    out = _outproj(ya, ym, w_out_bf[0:aw], w_out_bf[aw:], g_post.reshape(1, -1), x2)
    return out.reshape(batch, seq, d_model)


def kernel(x, rel_bias, g_pre, g_post, w_in, conv_w, conv_b, wq_m, wk_m, wv_m, w_if, b_if,
           mh_norm, skip, w_out):
    depth = w_in.shape[0]
    for l in range(depth):
        x = _layer(x, rel_bias, g_pre[l], g_post[l], w_in[l], conv_w[l], conv_b[l], wq_m[l],
                   wk_m[l], wv_m[l], w_if[l], b_if[l], mh_norm[l], skip[l], w_out[l])
    return x
```

```python
import functools
import math

import jax
import jax.numpy as jnp
import numpy as np
from jax import lax
from jax.experimental import pallas as pl
from jax.experimental.pallas import tpu as pltpu

f32 = jnp.float32
bf16 = jnp.bfloat16

ATT_HEADS = 8
ATT_HEAD_DIM = 128
ATT_WIDTH = ATT_HEADS * ATT_HEAD_DIM
MOBA_BLOCK = 256
MOBA_TOPK = 3
REL_BUCKETS = 32
REL_MAX_DIST = 2048
MLSTM_HEADS = 4
MLSTM_WIDTH = 1024
MLSTM_HEAD_DIM = MLSTM_WIDTH // MLSTM_HEADS
QKV_BLOCK = 4
CONV_WIDTH = 4
RMS_EPS = 1e-6
LN_EPS = 1e-5

MLSTM_CHUNK = 256
TOKEN_TILE = 512
OUT_TOKEN_TILE = 1024
MOBA_HEADS_PER_STEP = 4
MOBA_QBLOCKS_PER_STEP = 2
MLSTM_CHUNKS_PER_STEP = 4
LONG_TRIP = 8
HALO_ROWS = 16
SUM_ROWS = 16
GATE_ROWS = 8
SEL_PAD = 8
LANES = 128
NEG_BIG = -1e30
VMEM_LIMIT = 56 * 1024 * 1024
LOG2E = math.log2(math.e)

_NT = (((1,), (1,)), ((), ()))


def _t5_thresholds():
    n = np.arange(0, 2 * REL_MAX_DIST, dtype=np.int64)
    max_exact = REL_BUCKETS // 2
    nf = np.maximum(n, 1).astype(np.float32)
    large = max_exact + (np.log(nf / np.float32(max_exact))
                         / np.float32(math.log(REL_MAX_DIST / max_exact))
                         * np.float32(REL_BUCKETS - max_exact)).astype(np.int32)
    large = np.minimum(large, REL_BUCKETS - 1)
    bucket = np.where(n < max_exact, n, large)
    assert np.all(np.diff(bucket) >= 0)
    return [int(np.argmax(bucket >= k)) for k in range(1, REL_BUCKETS)]


T5_THR = _t5_thresholds()
NEAR_TILES = -(-(T5_THR[-1] + MOBA_BLOCK - 1) // MOBA_BLOCK)
assert NEAR_TILES * MOBA_BLOCK - (MOBA_BLOCK - 1) >= T5_THR[-1]
BIAS_TILES = NEAR_TILES + 1
TILED_STEPS = 2 * (-(-NEAR_TILES // 2))
FIRST_BARE_STEP = TILED_STEPS + 2
assert FIRST_BARE_STEP >= NEAR_TILES


def _dot(a, b):
    return jnp.dot(a, b, preferred_element_type=f32)


def _dot_nt(a, b):
    return lax.dot_general(a, b, _NT, preferred_element_type=f32)


def _split3(x):
    hi = x.astype(bf16)
    r = x - hi.astype(f32)
    mid = r.astype(bf16)
    lo = (r - mid.astype(f32)).astype(bf16)
    return hi, mid, lo


def _bias_kernel(rb_ref, out_ref):
    h = pl.program_id(0)
    key = lax.broadcasted_iota(jnp.int32, (MOBA_BLOCK, MOBA_BLOCK), 0)
    qry = lax.broadcasted_iota(jnp.int32, (MOBA_BLOCK, MOBA_BLOCK), 1)
    base = qry - key
    for d in range(BIAS_TILES):
        dist = base + d * MOBA_BLOCK
        n = jnp.maximum(dist, 0)
        val = jnp.full((MOBA_BLOCK, MOBA_BLOCK), rb_ref[REL_BUCKETS - 1, h] * LOG2E, f32)
        for k in range(REL_BUCKETS - 2, -1, -1):
            val = jnp.where(n < T5_THR[k], rb_ref[k, h] * LOG2E, val)
        if d == 0:
            val = jnp.where(dist >= 0, val, NEG_BIG)
        out_ref[d] = val


def _bias_tiles(rel_bias):
    return pl.pallas_call(
        _bias_kernel,
        grid=(ATT_HEADS,),
        in_specs=[pl.BlockSpec(memory_space=pltpu.SMEM)],
        out_specs=pl.BlockSpec((None, BIAS_TILES, MOBA_BLOCK, MOBA_BLOCK), lambda h: (h, 0, 0, 0)),
        out_shape=jax.ShapeDtypeStruct((ATT_HEADS, BIAS_TILES, MOBA_BLOCK, MOBA_BLOCK), f32),
        name="bias_tiles",
    )(rel_bias)


def _log_sigmoid(v):
    return jnp.minimum(v, 0.0) - jnp.log1p(jnp.exp(-jnp.abs(v)))


def _inproj_kernel(x_ref, g_ref, wq_ref, wv_ref, wr_ref, wx_ref, cw_ref, cb_ref, wmq_ref, wmk_ref,
                   wmv_ref, wg_ref, bg_ref,
                   qT_ref, vT_ref, pr_ref, xc_ref, mqT_ref, mk_ref, mvT_ref, ga_ref, gb_ref, gl_ref,
                   xpad_ref, *, tm, seq, q_scale, k_scale):
    i = pl.program_id(0)
    W = MLSTM_WIDTH
    L = MLSTM_CHUNK
    x = x_ref[...]
    ms = jnp.mean(x * x, axis=-1, keepdims=True)
    h = (x * lax.rsqrt(ms + RMS_EPS) * g_ref[...]).astype(bf16)

    @pl.when((i * tm) % seq == 0)
    def _():
        xpad_ref[0:HALO_ROWS, :] = jnp.zeros((HALO_ROWS, W), f32)

    for cc in range(W // 512):
        cols = slice(cc * 512, (cc + 1) * 512)
        xpad_ref[HALO_ROWS:HALO_ROWS + tm, cols] = _dot(h, wx_ref[:, cols])

    def att_qv(cc):
        rows = slice(cc * 256, (cc + 1) * 256)
        qt = (_dot(h, wq_ref[:, rows]) * q_scale).T
        vt = _dot(h, wv_ref[:, rows]).T
        for u in range(tm // MOBA_BLOCK):
            cols = slice(u * MOBA_BLOCK, (u + 1) * MOBA_BLOCK)
            qT_ref[u, rows, :] = qt[:, cols].astype(bf16)
            vT_ref[u, rows, :] = vt[:, cols].astype(bf16)

    def token_major(cc):
        cols = slice(cc * 512, (cc + 1) * 512)
        pr_ref[:, cols] = _dot(h, wr_ref[:, cols]).astype(bf16)

    def conv_silu(g):
        sl = slice(g * 256, (g + 1) * 256)
        acc = cb_ref[:, sl] + cw_ref[CONV_WIDTH - 1:CONV_WIDTH, sl] * xpad_ref[HALO_ROWS:HALO_ROWS + tm, sl]
        for j in range(CONV_WIDTH - 1):
            off = HALO_ROWS - (CONV_WIDTH - 1) + j
            acc = acc + cw_ref[j:j + 1, sl] * xpad_ref[off:off + tm, sl]
        xc_ref[:, sl] = (acc * jax.nn.sigmoid(acc)).astype(bf16)
        xpad_ref[0:HALO_ROWS, sl] = xpad_ref[tm:tm + HALO_ROWS, sl]

    k_unscale = 1.0 / k_scale
    gates = [jnp.zeros((tm, LANES), f32)]

    def block_diag_qkv(g):
        sl = slice(g * 256, (g + 1) * 256)
        xc_g = xc_ref[:, sl]
        xm_g = xpad_ref[HALO_ROWS:HALO_ROWS + tm, sl].astype(bf16)
        q = _dot(xc_g, wmq_ref[g])
        v = _dot(xm_g, wmv_ref[g])
        q_bf = q.astype(bf16)
        v_bf = v.astype(bf16)
        kk = (_dot(xc_g, wmk_ref[g]) * k_scale).astype(bf16)
        mk_ref[:, sl] = kk
        qT = q.T.astype(bf16)
        vT = v.T.astype(bf16)
        for u in range(tm // L):
            cols = slice(u * L, (u + 1) * L)
            mqT_ref[u, sl, :] = qT[:, cols]
            mvT_ref[u, sl, :] = vT[:, cols]
        ks = slice(W + g * 256, W + (g + 1) * 256)
        vs = slice(2 * W + g * 256, 2 * W + (g + 1) * 256)
        gates[0] = (gates[0] + _dot(q_bf, wg_ref[sl, :]) + k_unscale * _dot(kk, wg_ref[ks, :])
                    + _dot(v_bf, wg_ref[vs, :]))

    def gate_outputs():
        gates_t = (gates[0] + bg_ref[...]).T
        li = gates_t[0:GATE_ROWS, :] * LOG2E
        lf = _log_sigmoid(gates_t[GATE_ROWS:2 * GATE_ROWS, :]) * LOG2E
        row = lax.broadcasted_iota(jnp.int32, (L, L), 0)
        col = lax.broadcasted_iota(jnp.int32, (L, L), 1)
        upper = jnp.where(row <= col, 1.0, 0.0).astype(bf16)
        gl_ref[...] = li
        for u in range(tm // L):
            cols = slice(u * L, (u + 1) * L)
            b = sum(_dot(part, upper) for part in _split3(lf[:, cols]))
            gb_ref[:, cols] = b
            a_pad = jnp.concatenate([li[:, cols] - b, jnp.zeros((LANES - GATE_ROWS, L), f32)], axis=0)
            ga_ref[cols, :] = a_pad.T[:, 0:GATE_ROWS]

    big = ([functools.partial(token_major, cc) for cc in range(pr_ref.shape[1] // 512)]
           + [functools.partial(att_qv, cc) for cc in range(ATT_WIDTH // 256)])
    conv = [functools.partial(conv_silu, g) for g in range(W // 256)]
    proj = [functools.partial(block_diag_qkv, g) for g in range(W // 256)]
    prep = conv + proj + [gate_outputs]
    assert len(big) >= len(prep)
    for n, task in enumerate(big):
        task()
        if n < len(prep):
            prep[n]()


def _inproj(x2, g_pre, wq, wv, wr, wx, conv_w, conv_b, wmq, wmk, wmv, wg, bg, seq):
    T, D = x2.shape
    W = MLSTM_WIDTH
    tm = TOKEN_TILE
    L = MLSTM_CHUNK
    nblk = tm // MOBA_BLOCK
    ncols = wr.shape[1]
    resident = dict(pipeline_mode=pl.Buffered(1))
    const2 = lambda i: (0, 0)
    const3 = lambda i: (0, 0, 0)
    tok = pl.BlockSpec((tm, W), lambda i: (i, 0))
    att_T = pl.BlockSpec((nblk, ATT_WIDTH, MOBA_BLOCK), lambda i: (i, 0, 0))
    chunk_T = pl.BlockSpec((tm // L, W, L), lambda i: (i, 0, 0))
    rows8 = pl.BlockSpec((GATE_ROWS, tm), lambda i: (0, i))
    return pl.pallas_call(
        functools.partial(_inproj_kernel, tm=tm, seq=seq, q_scale=ATT_HEAD_DIM ** -0.5 * LOG2E,
                          k_scale=MLSTM_HEAD_DIM ** -0.5),
        grid=(T // tm,),
        in_specs=[
            pl.BlockSpec((tm, D), lambda i: (i, 0)),
            pl.BlockSpec((1, D), const2),
            pl.BlockSpec((D, ATT_WIDTH), const2, **resident),
            pl.BlockSpec((D, ATT_WIDTH), const2, **resident),
            pl.BlockSpec((D, ncols), const2, **resident),
            pl.BlockSpec((D, W), const2, **resident),
            pl.BlockSpec((CONV_WIDTH, W), const2),
            pl.BlockSpec((1, W), const2),
            pl.BlockSpec((W // 256, 256, 256), const3, **resident),
            pl.BlockSpec((W // 256, 256, 256), const3, **resident),
            pl.BlockSpec((W // 256, 256, 256), const3, **resident),
            pl.BlockSpec((3 * W, LANES), const2, **resident),
            pl.BlockSpec((1, LANES), const2),
        ],
        out_specs=[att_T, att_T, pl.BlockSpec((tm, ncols), lambda i: (i, 0)),
                   tok, chunk_T, tok, chunk_T,
                   pl.BlockSpec((tm, GATE_ROWS), lambda i: (i, 0)), rows8, rows8],
        out_shape=[
            jax.ShapeDtypeStruct((T // MOBA_BLOCK, ATT_WIDTH, MOBA_BLOCK), bf16),
            jax.ShapeDtypeStruct((T // MOBA_BLOCK, ATT_WIDTH, MOBA_BLOCK), bf16),
            jax.ShapeDtypeStruct((T, ncols), bf16),
            jax.ShapeDtypeStruct((T, W), bf16),
            jax.ShapeDtypeStruct((T // L, W, L), bf16),
            jax.ShapeDtypeStruct((T, W), bf16),
            jax.ShapeDtypeStruct((T // L, W, L), bf16),
            jax.ShapeDtypeStruct((T, GATE_ROWS), f32),
            jax.ShapeDtypeStruct((GATE_ROWS, T), f32),
            jax.ShapeDtypeStruct((GATE_ROWS, T), f32),
        ],
        scratch_shapes=[pltpu.VMEM((HALO_ROWS + tm, W), f32)],
        compiler_params=pltpu.CompilerParams(
            dimension_semantics=("arbitrary",), vmem_limit_bytes=VMEM_LIMIT),
        name="inproj",
    )(x2, g_pre, wq, wv, wr, wx, conv_w, conv_b, wmq, wmk, wmv, wg, bg)


def _mlstm_kernel(qT_ref, k_ref, vT_ref, xc_ref, z_ref, ga_ref, gb_ref, gl_ref, nw_ref, sk_ref,
                  y_ref, ct_ref, m_ref):
    L = MLSTM_CHUNK
    dh = MLSTM_HEAD_DIM

    @pl.when(pl.program_id(1) == 0)
    def _():
        ct_ref[...] = jnp.zeros_like(ct_ref)
        m_ref[...] = jnp.zeros_like(m_ref)

    s_idx = lax.broadcasted_iota(jnp.int32, (L, L), 0)
    t_idx = lax.broadcasted_iota(jnp.int32, (L, L), 1)
    causal = s_idx <= t_idx
    ones_rows = jnp.ones((SUM_ROWS, L), bf16)
    for u, h in ((u, h) for u in range(MLSTM_CHUNKS_PER_STEP) for h in range(MLSTM_HEADS)):
        sl = slice(h * dh, (h + 1) * dh)
        tok = slice(u * L, (u + 1) * L)
        qT = qT_ref[u, sl, :]
        k = k_ref[tok, sl]
        vT_ext = jnp.concatenate([vT_ref[u, sl, :], ones_rows], axis=0)
        a_c = ga_ref[tok, h:h + 1]
        b_r = gb_ref[h:h + 1, tok]
        li_r = gl_ref[h:h + 1, tok]
        b_last = b_r[:, L - 1:L]
        m_prev = m_ref[h][0:1, 0:1]
        ct = ct_ref[h]

        log_d = jnp.where(causal, a_c + b_r, NEG_BIG)
        inter = b_r + m_prev
        m_t = jnp.maximum(inter, jnp.max(log_d, axis=0, keepdims=True))
        sT = _dot(k, qT) * jnp.exp2(log_d - m_t)
        dec = jnp.exp2(inter - m_t)
        num = _dot(vT_ext, sT.astype(bf16)) + dec * _dot(ct.astype(bf16), qT)
        den = num[dh:dh + 1, :]
        hT = num[0:dh, :] * (1.0 / jnp.maximum(jnp.abs(den), jnp.exp2(-m_t)))

        mu = jnp.mean(hT, axis=0, keepdims=True)
        cen = hT - mu
        var = jnp.mean(cen * cen, axis=0, keepdims=True)
        yn = (cen * lax.rsqrt(var + LN_EPS)).T
        z = z_ref[tok, sl].astype(f32)
        out = (yn * nw_ref[:, sl] + sk_ref[:, sl] * xc_ref[tok, sl].astype(f32)) * (z * jax.nn.sigmoid(z))
        y_ref[tok, sl] = out.astype(y_ref.dtype)

        log_w = b_last - b_r + li_r
        m_new = jnp.maximum(b_last + m_prev, jnp.max(log_w, axis=1, keepdims=True))
        vw = vT_ext * jnp.exp2(log_w - m_new).astype(bf16)
        ct_ref[h] = jnp.exp2(b_last + m_prev - m_new) * ct + _dot(vw, k)
        m_ref[h] = jnp.broadcast_to(m_new, m_ref.shape[1:])


def _mlstm(qT, k, vT, xc, pr, ga, gb, gl, mh_norm, skip, batch, seq):
    T, W = k.shape
    cps = MLSTM_CHUNKS_PER_STEP
    rows = cps * MLSTM_CHUNK
    assert seq % rows == 0
    nc = seq // rows
    z_col = 2
    tok = pl.BlockSpec((rows, W), lambda b, c: (b * nc + c, 0))
    chunkT = pl.BlockSpec((cps, W, MLSTM_CHUNK), lambda b, c: (b * nc + c, 0, 0))
    rows8 = pl.BlockSpec((GATE_ROWS, rows), lambda b, c: (0, b * nc + c))
    return pl.pallas_call(
        _mlstm_kernel,
        grid=(batch, nc),
        in_specs=[chunkT, tok, chunkT, tok,
                  pl.BlockSpec((rows, W), lambda b, c: (b * nc + c, z_col)),
                  pl.BlockSpec((rows, GATE_ROWS), lambda b, c: (b * nc + c, 0)),
                  rows8, rows8,
                  pl.BlockSpec((1, W), lambda b, c: (0, 0)),
                  pl.BlockSpec((1, W), lambda b, c: (0, 0))],
        out_specs=tok,
        out_shape=jax.ShapeDtypeStruct((T, W), bf16),
        scratch_shapes=[pltpu.VMEM((MLSTM_HEADS, MLSTM_HEAD_DIM + SUM_ROWS, MLSTM_HEAD_DIM), f32),
                        pltpu.VMEM((MLSTM_HEADS, 8, LANES), f32)],
        compiler_params=pltpu.CompilerParams(
            dimension_semantics=("arbitrary", "arbitrary"), vmem_limit_bytes=VMEM_LIMIT),
        name="mlstm",
    )(qT, k, vT, xc, pr, ga, gb, gl, mh_norm, skip)


def _moba_kernel(*refs, nb, heads):
    def query_block(sub, c):
        _moba_query_block(sub, *refs, nb=nb, heads=heads)
        return c
    lax.fori_loop(0, MOBA_QBLOCKS_PER_STEP, query_block, 0)


def _moba_query_block(sub, rb_ref, qT_ref, k_ref, vT_ref, g_ref, bias_ref, o_ref, kmean_ref, sel_ref, sa_ref,
                      sb_ref, ca_ref, cb_ref, acc_ref, m_ref, alpha_ref, p_ref, *, nb, heads):
    blk_len = MOBA_BLOCK
    dh = ATT_HEAD_DIM
    own = pl.program_id(2) * MOBA_QBLOCKS_PER_STEP + sub
    q_rows = pl.ds(pl.multiple_of(sub * blk_len, blk_len), blk_len)
    qT_ref = qT_ref.at[sub]
    g_ref = g_ref.at[q_rows]
    o_ref = o_ref.at[q_rows]

    @pl.when(own == 0)
    def _():
        def mean_body(j, c):
            kb = k_ref[pl.ds(pl.multiple_of(j * blk_len, blk_len), blk_len), :].astype(f32)
            kmean_ref[pl.ds(j, 1), :] = jnp.sum(kb, axis=0, keepdims=True) * (1.0 / blk_len)
            return c
        lax.fori_loop(0, nb, mean_body, 0)

    hsl = [slice(hh * dh, (hh + 1) * dh) for hh in range(heads)]

    def score_matmuls(i, buf, with_tile):
        dst_ref, cmax_ref = buf
        ic = jnp.minimum(i, own)
        rows = pl.ds(pl.multiple_of((own - ic) * blk_len, blk_len), blk_len)
        for hh in range(heads):
            s = _dot(k_ref[rows, hsl[hh]], qT_ref[hsl[hh], :])
            if with_tile:
                s = s + bias_ref[hh, jnp.minimum(ic, NEAR_TILES)]
            dst_ref[hh] = s
            cmax_ref[hh] = jnp.max(s, axis=0, keepdims=True)

    def select_blocks():
        blk = lax.broadcasted_iota(jnp.int32, (nb, blk_len), 0)
        past = blk < own
        for hh in range(heads):
            qT = qT_ref[hsl[hh], :]
            km = kmean_ref[:, hsl[hh]]
            km_hi = km.astype(bf16)
            km_lo = (km - km_hi.astype(f32)).astype(bf16)
            gate = _dot(km_hi, qT) + _dot(km_lo, qT)
            g = jnp.where(past, gate, -jnp.inf)
            sel = blk == own
            for _ in range(MOBA_TOPK):
                mx = jnp.max(g, axis=0, keepdims=True)
                first = jnp.min(jnp.where(g == mx, blk, nb), axis=0, keepdims=True)
                pick = blk == first
                sel = sel | (pick & past)
                g = jnp.where(pick, -jnp.inf, g)
            sel_ref[hh, 0:nb, :] = jnp.where(sel, 0.0, NEG_BIG)
            sel_ref[hh, nb:nb + SEL_PAD, :] = jnp.full((SEL_PAD, blk_len), NEG_BIG, f32)

    ones_rows = jnp.ones((SUM_ROWS, blk_len), bf16)
    m_ref[...] = jnp.full(m_ref.shape, NEG_BIG, f32)
    acc_ref[...] = jnp.zeros(acc_ref.shape, f32)

    head0 = pl.program_id(0) * heads
    far_bias = [rb_ref[REL_BUCKETS - 1, head0 + hh] * LOG2E for hh in range(heads)]

    def softmax(i, buf):
        src_ref, cmax_ref = buf
        mask_row = jnp.where(i <= own, own - i, nb)
        stats = []
        for hh in range(heads):
            m = m_ref[hh]
            shift = jnp.where(i >= FIRST_BARE_STEP, far_bias[hh], 0.0)
            selrow = sel_ref[hh, pl.ds(mask_row, 1), :]
            m_new = jnp.maximum(m, cmax_ref[hh] + (selrow + shift))
            p = jnp.exp2(src_ref[hh] - jnp.where(selrow < 0.0, -NEG_BIG, m_new - shift)).astype(bf16)
            m_ref[hh] = m_new
            stats.append((jnp.exp2(m - m_new), p))
        return stats

    def value_update(i, stats):
        block = jnp.maximum(own - i, 0)
        for hh in range(heads):
            alpha, p = stats[hh]
            lhs = jnp.concatenate([vT_ref[block, hsl[hh], :], ones_rows], axis=0)
            acc_ref[hh] = alpha * acc_ref[hh] + _dot(lhs, p)

    def keep_pending(stats):
        for hh in range(heads):
            alpha_ref[hh], p_ref[hh] = stats[hh]

    even_buf, odd_buf = (sa_ref, ca_ref), (sb_ref, cb_ref)
    score_buf = (even_buf, odd_buf)

    def trip(i, n, with_tile):
        value_update(i, [(alpha_ref[hh], p_ref[hh]) for hh in range(heads)])
        score_matmuls(i + 2, even_buf, with_tile)
        for t in range(1, n):
            stats = softmax(i + t, score_buf[t % 2])
            if n > 2:
                score_matmuls(i + 2 + t, score_buf[t % 2], with_tile)
                value_update(i + t, stats)
            else:
                value_update(i + t, stats)
                score_matmuls(i + 2 + t, score_buf[t % 2], with_tile)
        keep_pending(softmax(i + n, even_buf))

    select_blocks()
    score_matmuls(0, even_buf, True)
    score_matmuls(1, odd_buf, True)
    keep_pending(softmax(0, even_buf))

    n_steps = own + 1
    n_bare = jnp.maximum(n_steps - TILED_STEPS, 0)
    n_long = n_bare // LONG_TRIP

    whole_tiled = n_steps >= TILED_STEPS

    def tiled_long_body(t, c):
        trip(0, TILED_STEPS, True)
        return c

    def tiled_short_body(t, c):
        trip(2 * t, 2, True)
        return c

    def long_body(t, c):
        trip(TILED_STEPS + LONG_TRIP * t, LONG_TRIP, False)
        return c

    def short_body(t, c):
        trip(TILED_STEPS + LONG_TRIP * n_long + 2 * t, 2, False)
        return c

    lax.fori_loop(0, jnp.where(whole_tiled, 1, 0), tiled_long_body, 0)
    lax.fori_loop(0, jnp.where(whole_tiled, 0, (n_steps + 1) // 2), tiled_short_body, 0)
    lax.fori_loop(0, n_long, long_body, 0)
    lax.fori_loop(0, (n_bare - LONG_TRIP * n_long + 1) // 2, short_body, 0)
    for hh in range(heads):
        acc = acc_ref[hh]
        o = (acc[0:dh, :] * (1.0 / acc[dh:dh + 1, :])).T
        gg = g_ref[:, hsl[hh]].astype(f32)
        o_ref[:, hsl[hh]] = (o * (gg * jax.nn.sigmoid(gg))).astype(o_ref.dtype)


def _moba(rel_bias, qT, vT, pr, bias_tiles, batch, seq):
    T = pr.shape[0]
    nb = seq // MOBA_BLOCK
    G = MOBA_HEADS_PER_STEP
    gw = G * ATT_HEAD_DIM
    gate_col0 = ATT_WIDTH // gw
    qb = MOBA_QBLOCKS_PER_STEP
    assert nb % qb == 0
    steps = nb // qb
    vT4 = vT.reshape(batch, nb, ATT_WIDTH, MOBA_BLOCK)
    return pl.pallas_call(
        functools.partial(_moba_kernel, nb=nb, heads=G),
        grid=(ATT_HEADS // G, batch, steps),
        in_specs=[
            pl.BlockSpec(memory_space=pltpu.SMEM),
            pl.BlockSpec((qb, gw, MOBA_BLOCK), lambda h, b, i: (b * steps + i, h, 0)),
            pl.BlockSpec((seq, gw), lambda h, b, i: (b, h)),
            pl.BlockSpec((None, nb, gw, MOBA_BLOCK), lambda h, b, i: (b, 0, h, 0)),
            pl.BlockSpec((qb * MOBA_BLOCK, gw), lambda h, b, i: (b * steps + i, gate_col0 + h)),
            pl.BlockSpec((G, BIAS_TILES, MOBA_BLOCK, MOBA_BLOCK), lambda h, b, i: (h, 0, 0, 0),
                         pipeline_mode=pl.Buffered(1)),
        ],
        out_specs=pl.BlockSpec((qb * MOBA_BLOCK, gw), lambda h, b, i: (b * steps + i, h)),
        out_shape=jax.ShapeDtypeStruct((T, ATT_WIDTH), bf16),
        scratch_shapes=[pltpu.VMEM((nb, gw), f32),
                        pltpu.VMEM((G, nb + SEL_PAD, MOBA_BLOCK), f32),
                        pltpu.VMEM((G, MOBA_BLOCK, MOBA_BLOCK), f32),
                        pltpu.VMEM((G, MOBA_BLOCK, MOBA_BLOCK), f32),
                        pltpu.VMEM((G, 1, MOBA_BLOCK), f32),
                        pltpu.VMEM((G, 1, MOBA_BLOCK), f32),
                        pltpu.VMEM((G, ATT_HEAD_DIM + SUM_ROWS, MOBA_BLOCK), f32),
                        pltpu.VMEM((G, 1, MOBA_BLOCK), f32),
                        pltpu.VMEM((G, 1, MOBA_BLOCK), f32),
                        pltpu.VMEM((G, MOBA_BLOCK, MOBA_BLOCK), bf16)],
        compiler_params=pltpu.CompilerParams(
            dimension_semantics=("arbitrary", "arbitrary", "arbitrary"),
            vmem_limit_bytes=VMEM_LIMIT),
        name="moba",
    )(rel_bias, qT, pr, vT4, pr, bias_tiles)


def _outproj_kernel(ya_ref, ym_ref, wa_ref, wm_ref, g_ref, x_ref, o_ref):
    y = _dot(ya_ref[...], wa_ref[...]) + _dot(ym_ref[...], wm_ref[...])
    ms = jnp.mean(y * y, axis=-1, keepdims=True)
    o_ref[...] = x_ref[...] + y * lax.rsqrt(ms + RMS_EPS) * g_ref[...]


def _outproj(ya, ym, wa, wm, g_post, x2):
    T, D = x2.shape
    tm = OUT_TOKEN_TILE
    assert T % tm == 0
    resident = dict(pipeline_mode=pl.Buffered(1))
    return pl.pallas_call(
        _outproj_kernel,
        grid=(T // tm,),
        in_specs=[
            pl.BlockSpec((tm, ya.shape[1]), lambda i: (i, 0)),
            pl.BlockSpec((tm, ym.shape[1]), lambda i: (i, 0)),
            pl.BlockSpec(wa.shape, lambda i: (0, 0), **resident),
            pl.BlockSpec(wm.shape, lambda i: (0, 0), **resident),
            pl.BlockSpec((1, D), lambda i: (0, 0)),
            pl.BlockSpec((tm, D), lambda i: (i, 0)),
        ],
        out_specs=pl.BlockSpec((tm, D), lambda i: (i, 0)),
        out_shape=jax.ShapeDtypeStruct((T, D), f32),
        compiler_params=pltpu.CompilerParams(
            dimension_semantics=("arbitrary",), vmem_limit_bytes=VMEM_LIMIT),
        name="outproj",
    )(ya, ym, wa, wm, g_post, x2)


def _block_diag_256(w):
    width = w.shape[0] * QKV_BLOCK
    rows = w.reshape(width, QKV_BLOCK)
    col = np.arange(256)
    spread = jnp.asarray(col[None, :] % QKV_BLOCK == np.arange(QKV_BLOCK)[:, None], w.dtype)
    tiled = jnp.dot(rows, spread, precision=lax.Precision.HIGHEST)
    same_block = (np.arange(width)[:, None] % 256) // QKV_BLOCK == col[None, :] // QKV_BLOCK
    dense = jnp.where(jnp.asarray(same_block), tiled, 0.0)
    return dense.reshape(-1, 256, 256)


def _gate_lanes(a):
    nh = MLSTM_HEADS
    out = jnp.zeros(a.shape[:-1] + (LANES,), a.dtype)
    return out.at[..., 0:nh].set(a[..., 0:nh]).at[..., GATE_ROWS:GATE_ROWS + nh].set(a[..., nh:2 * nh])


def _layer(x, rel_bias, g_pre, g_post, w_in, conv_w, conv_b, wq_m, wk_m, wv_m,
           w_if, b_if, mh_norm, skip, w_out):
    batch, seq, d_model = x.shape
    assert seq % TOKEN_TILE == 0 and TOKEN_TILE % MLSTM_CHUNK == 0 and TOKEN_TILE % MOBA_BLOCK == 0
    aw, nh = ATT_WIDTH, MLSTM_HEADS
    x2 = x.reshape(batch * seq, d_model)

    w_bf = w_in.astype(bf16)
    wq = w_bf[:, 0:aw]
    wv = w_bf[:, 2 * aw:3 * aw]
    mw = MLSTM_WIDTH
    wr = jnp.concatenate([w_bf[:, aw:2 * aw], w_bf[:, 3 * aw:4 * aw], w_bf[:, 4 * aw + mw:]], axis=1)
    wx = w_bf[:, 4 * aw:4 * aw + mw]
    wq_bd = _block_diag_256(wq_m).astype(bf16)
    wk_bd = _block_diag_256(wk_m).astype(bf16)
    wv_bd = _block_diag_256(wv_m).astype(bf16)
    wg = _gate_lanes(w_if).astype(bf16)
    bg = _gate_lanes(b_if).reshape(1, LANES)
    w_out_bf = w_out.astype(bf16)

    bias_tiles = _bias_tiles(rel_bias)
    qT, vT, pr, xc, qmT, km, vmT, ga, gb, gl = _inproj(
        x2, g_pre.reshape(1, -1), wq, wv, wr, wx, conv_w, conv_b.reshape(1, -1),
        wq_bd, wk_bd, wv_bd, wg, bg, seq)
    ym = _mlstm(qmT, km, vmT, xc, pr, ga, gb, gl, mh_norm.reshape(1, -1), skip.reshape(1, -1),
                batch, seq)
    ya = _moba(rel_bias, qT, vT, pr, bias_tiles, batch, seq)
    out = _outproj(ya, ym, w_out_bf[0:aw], w_out_bf[aw:], g_post.reshape(1, -1), x2)
    return out.reshape(batch, seq, d_model)


def kernel(x, rel_bias, g_pre, g_post, w_in, conv_w, conv_b, wq_m, wk_m, wv_m, w_if, b_if,
           mh_norm, skip, w_out):
    depth = w_in.shape[0]
    for l in range(depth):
        x = _layer(x, rel_bias, g_pre[l], g_post[l], w_in[l], conv_w[l], conv_b[l], wq_m[l],
                   wk_m[l], wv_m[l], w_if[l], b_if[l], mh_norm[l], skip[l], w_out[l])
    return x
```

```python
import functools
import math

import jax
import jax.numpy as jnp
import numpy as np
from jax import lax
from jax.experimental import pallas as pl
from jax.experimental.pallas import tpu as pltpu

f32 = jnp.float32
bf16 = jnp.bfloat16

ATT_HEADS = 8
ATT_HEAD_DIM = 128
ATT_WIDTH = ATT_HEADS * ATT_HEAD_DIM
MOBA_BLOCK = 256
MOBA_TOPK = 3
REL_BUCKETS = 32
REL_MAX_DIST = 2048
MLSTM_HEADS = 4
MLSTM_WIDTH = 1024
MLSTM_HEAD_DIM = MLSTM_WIDTH // MLSTM_HEADS
QKV_BLOCK = 4
CONV_WIDTH = 4
RMS_EPS = 1e-6
LN_EPS = 1e-5

MLSTM_CHUNK = 256
TOKEN_TILE = 512
OUT_TOKEN_TILE = 1024
MOBA_HEADS_PER_STEP = 4
MOBA_QBLOCKS_PER_STEP = 2
MLSTM_CHUNKS_PER_STEP = 4
LONG_TRIP = 8
HALO_ROWS = 16
SUM_ROWS = 16
GATE_ROWS = 8
SEL_PAD = 8
LANES = 128
NEG_BIG = -1e30
VMEM_LIMIT = 56 * 1024 * 1024
LOG2E = math.log2(math.e)

_NT = (((1,), (1,)), ((), ()))


def _t5_thresholds():
    n = np.arange(0, 2 * REL_MAX_DIST, dtype=np.int64)
    max_exact = REL_BUCKETS // 2
    nf = np.maximum(n, 1).astype(np.float32)
    large = max_exact + (np.log(nf / np.float32(max_exact))
                         / np.float32(math.log(REL_MAX_DIST / max_exact))
                         * np.float32(REL_BUCKETS - max_exact)).astype(np.int32)
    large = np.minimum(large, REL_BUCKETS - 1)
    bucket = np.where(n < max_exact, n, large)
    assert np.all(np.diff(bucket) >= 0)
    return [int(np.argmax(bucket >= k)) for k in range(1, REL_BUCKETS)]


T5_THR = _t5_thresholds()
NEAR_TILES = -(-(T5_THR[-1] + MOBA_BLOCK - 1) // MOBA_BLOCK)
assert NEAR_TILES * MOBA_BLOCK - (MOBA_BLOCK - 1) >= T5_THR[-1]
BIAS_TILES = NEAR_TILES + 1
TILED_STEPS = 2 * (-(-NEAR_TILES // 2))
FIRST_BARE_STEP = NEAR_TILES
assert 2 <= FIRST_BARE_STEP <= TILED_STEPS


def _dot(a, b):
    return jnp.dot(a, b, preferred_element_type=f32)


def _dot_nt(a, b):
    return lax.dot_general(a, b, _NT, preferred_element_type=f32)


def _split3(x):
    hi = x.astype(bf16)
    r = x - hi.astype(f32)
    mid = r.astype(bf16)
    lo = (r - mid.astype(f32)).astype(bf16)
    return hi, mid, lo


def _bias_kernel(rb_ref, out_ref):
    h = pl.program_id(0)
    key = lax.broadcasted_iota(jnp.int32, (MOBA_BLOCK, MOBA_BLOCK), 0)
    qry = lax.broadcasted_iota(jnp.int32, (MOBA_BLOCK, MOBA_BLOCK), 1)
    base = qry - key
    for d in range(BIAS_TILES):
        dist = base + d * MOBA_BLOCK
        n = jnp.maximum(dist, 0)
        val = jnp.full((MOBA_BLOCK, MOBA_BLOCK), rb_ref[REL_BUCKETS - 1, h] * LOG2E, f32)
        for k in range(REL_BUCKETS - 2, -1, -1):
            val = jnp.where(n < T5_THR[k], rb_ref[k, h] * LOG2E, val)
        if d == 0:
            val = jnp.where(dist >= 0, val, NEG_BIG)
        out_ref[d] = val


def _bias_tiles(rel_bias):
    return pl.pallas_call(
        _bias_kernel,
        grid=(ATT_HEADS,),
        in_specs=[pl.BlockSpec(memory_space=pltpu.SMEM)],
        out_specs=pl.BlockSpec((None, BIAS_TILES, MOBA_BLOCK, MOBA_BLOCK), lambda h: (h, 0, 0, 0)),
        out_shape=jax.ShapeDtypeStruct((ATT_HEADS, BIAS_TILES, MOBA_BLOCK, MOBA_BLOCK), f32),
        name="bias_tiles",
    )(rel_bias)


def _log_sigmoid(v):
    return jnp.minimum(v, 0.0) - jnp.log1p(jnp.exp(-jnp.abs(v)))


def _inproj_kernel(x_ref, g_ref, wq_ref, wv_ref, wr_ref, wx_ref, cw_ref, cb_ref, wmq_ref, wmk_ref,
                   wmv_ref, wg_ref, bg_ref,
                   qT_ref, vT_ref, pr_ref, xc_ref, mqT_ref, mk_ref, mvT_ref, ga_ref, gb_ref, gl_ref,
                   xpad_ref, *, tm, seq, q_scale, k_scale):
    i = pl.program_id(0)
    W = MLSTM_WIDTH
    L = MLSTM_CHUNK
    x = x_ref[...]
    ms = jnp.mean(x * x, axis=-1, keepdims=True)
    h = (x * lax.rsqrt(ms + RMS_EPS) * g_ref[...]).astype(bf16)

    @pl.when((i * tm) % seq == 0)
    def _():
        xpad_ref[0:HALO_ROWS, :] = jnp.zeros((HALO_ROWS, W), f32)

    for cc in range(W // 512):
        cols = slice(cc * 512, (cc + 1) * 512)
        xpad_ref[HALO_ROWS:HALO_ROWS + tm, cols] = _dot(h, wx_ref[:, cols])

    def att_qv(cc):
        rows = slice(cc * 256, (cc + 1) * 256)
        qt = (_dot(h, wq_ref[:, rows]) * q_scale).T
        vt = _dot(h, wv_ref[:, rows]).T
        for u in range(tm // MOBA_BLOCK):
            cols = slice(u * MOBA_BLOCK, (u + 1) * MOBA_BLOCK)
            qT_ref[u, rows, :] = qt[:, cols].astype(bf16)
            vT_ref[u, rows, :] = vt[:, cols].astype(bf16)

    def token_major(cc):
        cols = slice(cc * 512, (cc + 1) * 512)
        pr_ref[:, cols] = _dot(h, wr_ref[:, cols]).astype(bf16)

    def conv_silu(g):
        sl = slice(g * 256, (g + 1) * 256)
        acc = cb_ref[:, sl] + cw_ref[CONV_WIDTH - 1:CONV_WIDTH, sl] * xpad_ref[HALO_ROWS:HALO_ROWS + tm, sl]
        for j in range(CONV_WIDTH - 1):
            off = HALO_ROWS - (CONV_WIDTH - 1) + j
            acc = acc + cw_ref[j:j + 1, sl] * xpad_ref[off:off + tm, sl]
        xc_ref[:, sl] = (acc * jax.nn.sigmoid(acc)).astype(bf16)
        xpad_ref[0:HALO_ROWS, sl] = xpad_ref[tm:tm + HALO_ROWS, sl]

    k_unscale = 1.0 / k_scale
    gates = [jnp.zeros((tm, LANES), f32)]

    def block_diag_qkv(g):
        sl = slice(g * 256, (g + 1) * 256)
        xc_g = xc_ref[:, sl]
        xm_g = xpad_ref[HALO_ROWS:HALO_ROWS + tm, sl].astype(bf16)
        q = _dot(xc_g, wmq_ref[g])
        v = _dot(xm_g, wmv_ref[g])
        q_bf = q.astype(bf16)
        v_bf = v.astype(bf16)
        kk = (_dot(xc_g, wmk_ref[g]) * k_scale).astype(bf16)
        mk_ref[:, sl] = kk
        qT = q.T.astype(bf16)
        vT = v.T.astype(bf16)
        for u in range(tm // L):
            cols = slice(u * L, (u + 1) * L)
            mqT_ref[u, sl, :] = qT[:, cols]
            mvT_ref[u, sl, :] = vT[:, cols]
        ks = slice(W + g * 256, W + (g + 1) * 256)
        vs = slice(2 * W + g * 256, 2 * W + (g + 1) * 256)
        gates[0] = (gates[0] + _dot(q_bf, wg_ref[sl, :]) + k_unscale * _dot(kk, wg_ref[ks, :])
                    + _dot(v_bf, wg_ref[vs, :]))

    def gate_outputs():
        gates_t = (gates[0] + bg_ref[...]).T
        li = gates_t[0:GATE_ROWS, :] * LOG2E
        lf = _log_sigmoid(gates_t[GATE_ROWS:2 * GATE_ROWS, :]) * LOG2E
        row = lax.broadcasted_iota(jnp.int32, (L, L), 0)
        col = lax.broadcasted_iota(jnp.int32, (L, L), 1)
        upper = jnp.where(row <= col, 1.0, 0.0).astype(bf16)
        gl_ref[...] = li
        for u in range(tm // L):
            cols = slice(u * L, (u + 1) * L)
            b = sum(_dot(part, upper) for part in _split3(lf[:, cols]))
            gb_ref[:, cols] = b
            a_pad = jnp.concatenate([li[:, cols] - b, jnp.zeros((LANES - GATE_ROWS, L), f32)], axis=0)
            ga_ref[cols, :] = a_pad.T[:, 0:GATE_ROWS]

    big = ([functools.partial(token_major, cc) for cc in range(pr_ref.shape[1] // 512)]
           + [functools.partial(att_qv, cc) for cc in range(ATT_WIDTH // 256)])
    conv = [functools.partial(conv_silu, g) for g in range(W // 256)]
    proj = [functools.partial(block_diag_qkv, g) for g in range(W // 256)]
    prep = conv + proj + [gate_outputs]
    assert len(big) >= len(prep)
    for n, task in enumerate(big):
        task()
        if n < len(prep):
            prep[n]()


def _inproj(x2, g_pre, wq, wv, wr, wx, conv_w, conv_b, wmq, wmk, wmv, wg, bg, seq):
    T, D = x2.shape
    W = MLSTM_WIDTH
    tm = TOKEN_TILE
    L = MLSTM_CHUNK
    nblk = tm // MOBA_BLOCK
    ncols = wr.shape[1]
    resident = dict(pipeline_mode=pl.Buffered(1))
    const2 = lambda i: (0, 0)
    const3 = lambda i: (0, 0, 0)
    tok = pl.BlockSpec((tm, W), lambda i: (i, 0))
    att_T = pl.BlockSpec((nblk, ATT_WIDTH, MOBA_BLOCK), lambda i: (i, 0, 0))
    chunk_T = pl.BlockSpec((tm // L, W, L), lambda i: (i, 0, 0))
    rows8 = pl.BlockSpec((GATE_ROWS, tm), lambda i: (0, i))
    return pl.pallas_call(
        functools.partial(_inproj_kernel, tm=tm, seq=seq, q_scale=ATT_HEAD_DIM ** -0.5 * LOG2E,
                          k_scale=MLSTM_HEAD_DIM ** -0.5),
        grid=(T // tm,),
        in_specs=[
            pl.BlockSpec((tm, D), lambda i: (i, 0)),
            pl.BlockSpec((1, D), const2),
            pl.BlockSpec((D, ATT_WIDTH), const2, **resident),
            pl.BlockSpec((D, ATT_WIDTH), const2, **resident),
            pl.BlockSpec((D, ncols), const2, **resident),
            pl.BlockSpec((D, W), const2, **resident),
            pl.BlockSpec((CONV_WIDTH, W), const2),
            pl.BlockSpec((1, W), const2),
            pl.BlockSpec((W // 256, 256, 256), const3, **resident),
            pl.BlockSpec((W // 256, 256, 256), const3, **resident),
            pl.BlockSpec((W // 256, 256, 256), const3, **resident),
            pl.BlockSpec((3 * W, LANES), const2, **resident),
            pl.BlockSpec((1, LANES), const2),
        ],
        out_specs=[att_T, att_T, pl.BlockSpec((tm, ncols), lambda i: (i, 0)),
                   tok, chunk_T, tok, chunk_T,
                   pl.BlockSpec((tm, GATE_ROWS), lambda i: (i, 0)), rows8, rows8],
        out_shape=[
            jax.ShapeDtypeStruct((T // MOBA_BLOCK, ATT_WIDTH, MOBA_BLOCK), bf16),
            jax.ShapeDtypeStruct((T // MOBA_BLOCK, ATT_WIDTH, MOBA_BLOCK), bf16),
            jax.ShapeDtypeStruct((T, ncols), bf16),
            jax.ShapeDtypeStruct((T, W), bf16),
            jax.ShapeDtypeStruct((T // L, W, L), bf16),
            jax.ShapeDtypeStruct((T, W), bf16),
            jax.ShapeDtypeStruct((T // L, W, L), bf16),
            jax.ShapeDtypeStruct((T, GATE_ROWS), f32),
            jax.ShapeDtypeStruct((GATE_ROWS, T), f32),
            jax.ShapeDtypeStruct((GATE_ROWS, T), f32),
        ],
        scratch_shapes=[pltpu.VMEM((HALO_ROWS + tm, W), f32)],
        compiler_params=pltpu.CompilerParams(
            dimension_semantics=("arbitrary",), vmem_limit_bytes=VMEM_LIMIT),
        name="inproj",
    )(x2, g_pre, wq, wv, wr, wx, conv_w, conv_b, wmq, wmk, wmv, wg, bg)


def _mlstm_kernel(qT_ref, k_ref, vT_ref, xc_ref, z_ref, ga_ref, gb_ref, gl_ref, nw_ref, sk_ref,
                  y_ref, ct_ref, m_ref):
    L = MLSTM_CHUNK
    dh = MLSTM_HEAD_DIM

    @pl.when(pl.program_id(1) == 0)
    def _():
        ct_ref[...] = jnp.zeros_like(ct_ref)
        m_ref[...] = jnp.zeros_like(m_ref)

    s_idx = lax.broadcasted_iota(jnp.int32, (L, L), 0)
    t_idx = lax.broadcasted_iota(jnp.int32, (L, L), 1)
    causal = s_idx <= t_idx
    ones_rows = jnp.ones((SUM_ROWS, L), bf16)
    for u, h in ((u, h) for u in range(MLSTM_CHUNKS_PER_STEP) for h in range(MLSTM_HEADS)):
        sl = slice(h * dh, (h + 1) * dh)
        tok = slice(u * L, (u + 1) * L)
        qT = qT_ref[u, sl, :]
        k = k_ref[tok, sl]
        vT_ext = jnp.concatenate([vT_ref[u, sl, :], ones_rows], axis=0)
        a_c = ga_ref[tok, h:h + 1]
        b_r = gb_ref[h:h + 1, tok]
        li_r = gl_ref[h:h + 1, tok]
        b_last = b_r[:, L - 1:L]
        m_prev = m_ref[h][0:1, 0:1]
        ct = ct_ref[h]

        log_d = jnp.where(causal, a_c + b_r, NEG_BIG)
        inter = b_r + m_prev
        m_t = jnp.maximum(inter, jnp.max(log_d, axis=0, keepdims=True))
        sT = _dot(k, qT) * jnp.exp2(log_d - m_t)
        dec = jnp.exp2(inter - m_t)
        num = _dot(vT_ext, sT.astype(bf16)) + dec * _dot(ct.astype(bf16), qT)
        den = num[dh:dh + 1, :]
        hT = num[0:dh, :] * (1.0 / jnp.maximum(jnp.abs(den), jnp.exp2(-m_t)))

        mu = jnp.mean(hT, axis=0, keepdims=True)
        cen = hT - mu
        var = jnp.mean(cen * cen, axis=0, keepdims=True)
        yn = (cen * lax.rsqrt(var + LN_EPS)).T
        z = z_ref[tok, sl].astype(f32)
        out = (yn * nw_ref[:, sl] + sk_ref[:, sl] * xc_ref[tok, sl].astype(f32)) * (z * jax.nn.sigmoid(z))
        y_ref[tok, sl] = out.astype(y_ref.dtype)

        log_w = b_last - b_r + li_r
        m_new = jnp.maximum(b_last + m_prev, jnp.max(log_w, axis=1, keepdims=True))
        vw = vT_ext * jnp.exp2(log_w - m_new).astype(bf16)
        ct_ref[h] = jnp.exp2(b_last + m_prev - m_new) * ct + _dot(vw, k)
        m_ref[h] = jnp.broadcast_to(m_new, m_ref.shape[1:])


def _mlstm(qT, k, vT, xc, pr, ga, gb, gl, mh_norm, skip, batch, seq):
    T, W = k.shape
    cps = MLSTM_CHUNKS_PER_STEP
    rows = cps * MLSTM_CHUNK
    assert seq % rows == 0
    nc = seq // rows
    z_col = 2
    tok = pl.BlockSpec((rows, W), lambda b, c: (b * nc + c, 0))
    chunkT = pl.BlockSpec((cps, W, MLSTM_CHUNK), lambda b, c: (b * nc + c, 0, 0))
    rows8 = pl.BlockSpec((GATE_ROWS, rows), lambda b, c: (0, b * nc + c))
    return pl.pallas_call(
        _mlstm_kernel,
        grid=(batch, nc),
        in_specs=[chunkT, tok, chunkT, tok,
                  pl.BlockSpec((rows, W), lambda b, c: (b * nc + c, z_col)),
                  pl.BlockSpec((rows, GATE_ROWS), lambda b, c: (b * nc + c, 0)),
                  rows8, rows8,
                  pl.BlockSpec((1, W), lambda b, c: (0, 0)),
                  pl.BlockSpec((1, W), lambda b, c: (0, 0))],
        out_specs=tok,
        out_shape=jax.ShapeDtypeStruct((T, W), bf16),
        scratch_shapes=[pltpu.VMEM((MLSTM_HEADS, MLSTM_HEAD_DIM + SUM_ROWS, MLSTM_HEAD_DIM), f32),
                        pltpu.VMEM((MLSTM_HEADS, 8, LANES), f32)],
        compiler_params=pltpu.CompilerParams(
            dimension_semantics=("arbitrary", "arbitrary"), vmem_limit_bytes=VMEM_LIMIT),
        name="mlstm",
    )(qT, k, vT, xc, pr, ga, gb, gl, mh_norm, skip)


def _moba_kernel(*refs, nb, heads):
    def query_block(sub, c):
        _moba_query_block(sub, *refs, nb=nb, heads=heads)
        return c
    lax.fori_loop(0, MOBA_QBLOCKS_PER_STEP, query_block, 0)


def _moba_query_block(sub, rb_ref, qT_ref, k_ref, vT_ref, g_ref, bias_ref, o_ref, kmean_ref, sel_ref, sa_ref,
                      sb_ref, ca_ref, cb_ref, acc_ref, m_ref, alpha_ref, p_ref, *, nb, heads):
    blk_len = MOBA_BLOCK
    dh = ATT_HEAD_DIM
    own = pl.program_id(2) * MOBA_QBLOCKS_PER_STEP + sub
    q_rows = pl.ds(pl.multiple_of(sub * blk_len, blk_len), blk_len)
    qT_ref = qT_ref.at[sub]
    g_ref = g_ref.at[q_rows]
    o_ref = o_ref.at[q_rows]

    @pl.when(own == 0)
    def _():
        def mean_body(j, c):
            kb = k_ref[pl.ds(pl.multiple_of(j * blk_len, blk_len), blk_len), :].astype(f32)
            kmean_ref[pl.ds(j, 1), :] = jnp.sum(kb, axis=0, keepdims=True) * (1.0 / blk_len)
            return c
        lax.fori_loop(0, nb, mean_body, 0)

    hsl = [slice(hh * dh, (hh + 1) * dh) for hh in range(heads)]

    def score_matmuls(i, buf, with_tile):
        dst_ref, cmax_ref = buf
        ic = jnp.minimum(i, own)
        rows = pl.ds(pl.multiple_of((own - ic) * blk_len, blk_len), blk_len)
        for hh in range(heads):
            s = _dot(k_ref[rows, hsl[hh]], qT_ref[hsl[hh], :])
            if with_tile:
                s = s + bias_ref[hh, jnp.minimum(ic, NEAR_TILES)]
            dst_ref[hh] = s
            cmax_ref[hh] = jnp.max(s, axis=0, keepdims=True)

    def select_blocks():
        blk = lax.broadcasted_iota(jnp.int32, (nb, blk_len), 0)
        past = blk < own
        for hh in range(heads):
            qT = qT_ref[hsl[hh], :]
            km = kmean_ref[:, hsl[hh]]
            km_hi = km.astype(bf16)
            km_lo = (km - km_hi.astype(f32)).astype(bf16)
            gate = _dot(km_hi, qT) + _dot(km_lo, qT)
            g = jnp.where(past, gate, -jnp.inf)
            sel = blk == own
            for _ in range(MOBA_TOPK):
                mx = jnp.max(g, axis=0, keepdims=True)
                first = jnp.min(jnp.where(g == mx, blk, nb), axis=0, keepdims=True)
                pick = blk == first
                sel = sel | (pick & past)
                g = jnp.where(pick, -jnp.inf, g)
            sel_ref[hh, 0:nb, :] = jnp.where(sel, 0.0, NEG_BIG)
            sel_ref[hh, nb:nb + SEL_PAD, :] = jnp.full((SEL_PAD, blk_len), NEG_BIG, f32)

    ones_rows = jnp.ones((SUM_ROWS, blk_len), bf16)
    m_ref[...] = jnp.full(m_ref.shape, NEG_BIG, f32)
    acc_ref[...] = jnp.zeros(acc_ref.shape, f32)

    head0 = pl.program_id(0) * heads
    far_bias = [rb_ref[REL_BUCKETS - 1, head0 + hh] * LOG2E for hh in range(heads)]

    def softmax(i, buf):
        src_ref, cmax_ref = buf
        mask_row = jnp.where(i <= own, own - i, nb)
        stats = []
        for hh in range(heads):
            m = m_ref[hh]
            shift = jnp.where(i >= FIRST_BARE_STEP, far_bias[hh], 0.0)
            selrow = sel_ref[hh, pl.ds(mask_row, 1), :]
            m_new = jnp.maximum(m, cmax_ref[hh] + (selrow + shift))
            p = jnp.exp2(src_ref[hh] - jnp.where(selrow < 0.0, -NEG_BIG, m_new - shift)).astype(bf16)
            m_ref[hh] = m_new
            stats.append((jnp.exp2(m - m_new), p))
        return stats

    def value_update(i, stats):
        block = jnp.maximum(own - i, 0)
        for hh in range(heads):
            alpha, p = stats[hh]
            lhs = jnp.concatenate([vT_ref[block, hsl[hh], :], ones_rows], axis=0)
            acc_ref[hh] = alpha * acc_ref[hh] + _dot(lhs, p)

    def keep_pending(stats):
        for hh in range(heads):
            alpha_ref[hh], p_ref[hh] = stats[hh]

    even_buf, odd_buf = (sa_ref, ca_ref), (sb_ref, cb_ref)
    score_buf = (even_buf, odd_buf)

    def trip(i, n, with_tile):
        value_update(i, [(alpha_ref[hh], p_ref[hh]) for hh in range(heads)])
        score_matmuls(i + 2, even_buf, with_tile(i + 2))
        for t in range(1, n):
            stats = softmax(i + t, score_buf[t % 2])
            if n > 2:
                score_matmuls(i + 2 + t, score_buf[t % 2], with_tile(i + 2 + t))
                value_update(i + t, stats)
            else:
                value_update(i + t, stats)
                score_matmuls(i + 2 + t, score_buf[t % 2], with_tile(i + 2 + t))
        keep_pending(softmax(i + n, even_buf))

    select_blocks()
    score_matmuls(0, even_buf, True)
    score_matmuls(1, odd_buf, True)
    keep_pending(softmax(0, even_buf))

    n_steps = own + 1
    n_bare = jnp.maximum(n_steps - TILED_STEPS, 0)
    n_long = n_bare // LONG_TRIP

    whole_tiled = n_steps >= TILED_STEPS

    def tiled_long_body(t, c):
        trip(0, TILED_STEPS, lambda step: step < FIRST_BARE_STEP)
        return c

    def tiled_short_body(t, c):
        trip(2 * t, 2, lambda step: True)
        return c

    def long_body(t, c):
        trip(TILED_STEPS + LONG_TRIP * t, LONG_TRIP, lambda step: False)
        return c

    def short_body(t, c):
        trip(TILED_STEPS + LONG_TRIP * n_long + 2 * t, 2, lambda step: False)
        return c

    lax.fori_loop(0, jnp.where(whole_tiled, 1, 0), tiled_long_body, 0)
    lax.fori_loop(0, jnp.where(whole_tiled, 0, (n_steps + 1) // 2), tiled_short_body, 0)
    lax.fori_loop(0, n_long, long_body, 0)
    lax.fori_loop(0, (n_bare - LONG_TRIP * n_long + 1) // 2, short_body, 0)
    for hh in range(heads):
        acc = acc_ref[hh]
        o = (acc[0:dh, :] * (1.0 / acc[dh:dh + 1, :])).T
        gg = g_ref[:, hsl[hh]].astype(f32)
        o_ref[:, hsl[hh]] = (o * (gg * jax.nn.sigmoid(gg))).astype(o_ref.dtype)


def _moba(rel_bias, qT, vT, pr, bias_tiles, batch, seq):
    T = pr.shape[0]
    nb = seq // MOBA_BLOCK
    G = MOBA_HEADS_PER_STEP
    gw = G * ATT_HEAD_DIM
    gate_col0 = ATT_WIDTH // gw
    qb = MOBA_QBLOCKS_PER_STEP
    assert nb % qb == 0
    steps = nb // qb
    vT4 = vT.reshape(batch, nb, ATT_WIDTH, MOBA_BLOCK)
    return pl.pallas_call(
        functools.partial(_moba_kernel, nb=nb, heads=G),
        grid=(ATT_HEADS // G, batch, steps),
        in_specs=[
            pl.BlockSpec(memory_space=pltpu.SMEM),
            pl.BlockSpec((qb, gw, MOBA_BLOCK), lambda h, b, i: (b * steps + i, h, 0)),
            pl.BlockSpec((seq, gw), lambda h, b, i: (b, h)),
            pl.BlockSpec((None, nb, gw, MOBA_BLOCK), lambda h, b, i: (b, 0, h, 0)),
            pl.BlockSpec((qb * MOBA_BLOCK, gw), lambda h, b, i: (b * steps + i, gate_col0 + h)),
            pl.BlockSpec((G, BIAS_TILES, MOBA_BLOCK, MOBA_BLOCK), lambda h, b, i: (h, 0, 0, 0),
                         pipeline_mode=pl.Buffered(1)),
        ],
        out_specs=pl.BlockSpec((qb * MOBA_BLOCK, gw), lambda h, b, i: (b * steps + i, h)),
        out_shape=jax.ShapeDtypeStruct((T, ATT_WIDTH), bf16),
        scratch_shapes=[pltpu.VMEM((nb, gw), f32),
                        pltpu.VMEM((G, nb + SEL_PAD, MOBA_BLOCK), f32),
                        pltpu.VMEM((G, MOBA_BLOCK, MOBA_BLOCK), f32),
                        pltpu.VMEM((G, MOBA_BLOCK, MOBA_BLOCK), f32),
                        pltpu.VMEM((G, 1, MOBA_BLOCK), f32),
                        pltpu.VMEM((G, 1, MOBA_BLOCK), f32),
                        pltpu.VMEM((G, ATT_HEAD_DIM + SUM_ROWS, MOBA_BLOCK), f32),
                        pltpu.VMEM((G, 1, MOBA_BLOCK), f32),
                        pltpu.VMEM((G, 1, MOBA_BLOCK), f32),
                        pltpu.VMEM((G, MOBA_BLOCK, MOBA_BLOCK), bf16)],
        compiler_params=pltpu.CompilerParams(
            dimension_semantics=("arbitrary", "arbitrary", "arbitrary"),
            vmem_limit_bytes=VMEM_LIMIT),
        name="moba",
    )(rel_bias, qT, pr, vT4, pr, bias_tiles)


def _outproj_kernel(ya_ref, ym_ref, wa_ref, wm_ref, g_ref, x_ref, o_ref):
    y = _dot(ya_ref[...], wa_ref[...]) + _dot(ym_ref[...], wm_ref[...])
    ms = jnp.mean(y * y, axis=-1, keepdims=True)
    o_ref[...] = x_ref[...] + y * lax.rsqrt(ms + RMS_EPS) * g_ref[...]


def _outproj(ya, ym, wa, wm, g_post, x2):
    T, D = x2.shape
    tm = OUT_TOKEN_TILE
    assert T % tm == 0
    resident = dict(pipeline_mode=pl.Buffered(1))
    return pl.pallas_call(
        _outproj_kernel,
        grid=(T // tm,),
        in_specs=[
            pl.BlockSpec((tm, ya.shape[1]), lambda i: (i, 0)),
            pl.BlockSpec((tm, ym.shape[1]), lambda i: (i, 0)),
            pl.BlockSpec(wa.shape, lambda i: (0, 0), **resident),
            pl.BlockSpec(wm.shape, lambda i: (0, 0), **resident),
            pl.BlockSpec((1, D), lambda i: (0, 0)),
            pl.BlockSpec((tm, D), lambda i: (i, 0)),
        ],
        out_specs=pl.BlockSpec((tm, D), lambda i: (i, 0)),
        out_shape=jax.ShapeDtypeStruct((T, D), f32),
        compiler_params=pltpu.CompilerParams(
            dimension_semantics=("arbitrary",), vmem_limit_bytes=VMEM_LIMIT),
        name="outproj",
    )(ya, ym, wa, wm, g_post, x2)


def _block_diag_256(w):
    width = w.shape[0] * QKV_BLOCK
    rows = w.reshape(width, QKV_BLOCK)
    col = np.arange(256)
    spread = jnp.asarray(col[None, :] % QKV_BLOCK == np.arange(QKV_BLOCK)[:, None], w.dtype)
    tiled = jnp.dot(rows, spread, precision=lax.Precision.HIGHEST)
    same_block = (np.arange(width)[:, None] % 256) // QKV_BLOCK == col[None, :] // QKV_BLOCK
    dense = jnp.where(jnp.asarray(same_block), tiled, 0.0)
    return dense.reshape(-1, 256, 256)


def _gate_lanes(a):
    nh = MLSTM_HEADS
    zeros = lambda n: jnp.zeros(a.shape[:-1] + (n,), a.dtype)
    return jnp.concatenate([a[..., 0:nh], zeros(GATE_ROWS - nh), a[..., nh:2 * nh],
                            zeros(LANES - GATE_ROWS - nh)], axis=-1)


def _layer(x, rel_bias, g_pre, g_post, w_in, conv_w, conv_b, wq_m, wk_m, wv_m,
           w_if, b_if, mh_norm, skip, w_out):
    batch, seq, d_model = x.shape
    assert seq % TOKEN_TILE == 0 and TOKEN_TILE % MLSTM_CHUNK == 0 and TOKEN_TILE % MOBA_BLOCK == 0
    aw, nh = ATT_WIDTH, MLSTM_HEADS
    x2 = x.reshape(batch * seq, d_model)

    w_bf = w_in.astype(bf16)
    wq = w_bf[:, 0:aw]
    wv = w_bf[:, 2 * aw:3 * aw]
    mw = MLSTM_WIDTH
    wr = jnp.concatenate([w_bf[:, aw:2 * aw], w_bf[:, 3 * aw:4 * aw], w_bf[:, 4 * aw + mw:]], axis=1)
    wx = w_bf[:, 4 * aw:4 * aw + mw]
    wq_bd = _block_diag_256(wq_m).astype(bf16)
    wk_bd = _block_diag_256(wk_m).astype(bf16)
    wv_bd = _block_diag_256(wv_m).astype(bf16)
    wg = _gate_lanes(w_if).astype(bf16)
    bg = _gate_lanes(b_if).reshape(1, LANES)
    w_out_bf = w_out.astype(bf16)

    bias_tiles = _bias_tiles(rel_bias)
    qT, vT, pr, xc, qmT, km, vmT, ga, gb, gl = _inproj(
        x2, g_pre.reshape(1, -1), wq, wv, wr, wx, conv_w, conv_b.reshape(1, -1),
        wq_bd, wk_bd, wv_bd, wg, bg, seq)
    ym = _mlstm(qmT, km, vmT, xc, pr, ga, gb, gl, mh_norm.reshape(1, -1), skip.reshape(1, -1),
                batch, seq)
    ya = _moba(rel_bias, qT, vT, pr, bias_tiles, batch, seq)
    out = _outproj(ya, ym, w_out_bf[0:aw], w_out_bf[aw:], g_post.reshape(1, -1), x2)
    return out.reshape(batch, seq, d_model)


def kernel(x, rel_bias, g_pre, g_post, w_in, conv_w, conv_b, wq_m, wk_m, wv_m, w_if, b_if,
           mh_norm, skip, w_out):
    depth = w_in.shape[0]
    for l in range(depth):
        x = _layer(x, rel_bias, g_pre[l], g_post[l], w_in[l], conv_w[l], conv_b[l], wq_m[l],
                   wk_m[l], wv_m[l], w_if[l], b_if[l], mh_norm[l], skip[l], w_out[l])
    return x
```

```python
import functools
import math

import jax
import jax.numpy as jnp
import numpy as np
from jax import lax
from jax.experimental import pallas as pl
from jax.experimental.pallas import tpu as pltpu

f32 = jnp.float32
bf16 = jnp.bfloat16

ATT_HEADS = 8
ATT_HEAD_DIM = 128
ATT_WIDTH = ATT_HEADS * ATT_HEAD_DIM
MOBA_BLOCK = 256
MOBA_TOPK = 3
REL_BUCKETS = 32
REL_MAX_DIST = 2048
MLSTM_HEADS = 4
MLSTM_WIDTH = 1024
MLSTM_HEAD_DIM = MLSTM_WIDTH // MLSTM_HEADS
QKV_BLOCK = 4
CONV_WIDTH = 4
RMS_EPS = 1e-6
LN_EPS = 1e-5

MLSTM_CHUNK = 256
TOKEN_TILE = 512
OUT_TOKEN_TILE = 1024
MOBA_HEADS_PER_STEP = 4
MOBA_QBLOCKS_PER_STEP = 2
MLSTM_CHUNKS_PER_STEP = 4
LONG_TRIP = 8
LONG_TRIP_SLACK = 1
TILED_TRIP_SLACK = 3
HALO_ROWS = 16
SUM_ROWS = 16
GATE_ROWS = 8
SEL_PAD = 8
LANES = 128
NEG_BIG = -1e30
VMEM_LIMIT = 56 * 1024 * 1024
LOG2E = math.log2(math.e)

_NT = (((1,), (1,)), ((), ()))


def _t5_thresholds():
    n = np.arange(0, 2 * REL_MAX_DIST, dtype=np.int64)
    max_exact = REL_BUCKETS // 2
    nf = np.maximum(n, 1).astype(np.float32)
    large = max_exact + (np.log(nf / np.float32(max_exact))
                         / np.float32(math.log(REL_MAX_DIST / max_exact))
                         * np.float32(REL_BUCKETS - max_exact)).astype(np.int32)
    large = np.minimum(large, REL_BUCKETS - 1)
    bucket = np.where(n < max_exact, n, large)
    assert np.all(np.diff(bucket) >= 0)
    return [int(np.argmax(bucket >= k)) for k in range(1, REL_BUCKETS)]


T5_THR = _t5_thresholds()
NEAR_TILES = -(-(T5_THR[-1] + MOBA_BLOCK - 1) // MOBA_BLOCK)
assert NEAR_TILES * MOBA_BLOCK - (MOBA_BLOCK - 1) >= T5_THR[-1]
BIAS_TILES = NEAR_TILES + 1
TILED_STEPS = 2 * (-(-NEAR_TILES // 2))
FIRST_BARE_STEP = NEAR_TILES
assert 2 <= FIRST_BARE_STEP <= TILED_STEPS


def _dot(a, b):
    return jnp.dot(a, b, preferred_element_type=f32)


def _dot_nt(a, b):
    return lax.dot_general(a, b, _NT, preferred_element_type=f32)


def _split3(x):
    hi = x.astype(bf16)
    r = x - hi.astype(f32)
    mid = r.astype(bf16)
    lo = (r - mid.astype(f32)).astype(bf16)
    return hi, mid, lo


def _bias_kernel(rb_ref, out_ref):
    h = pl.program_id(0)
    key = lax.broadcasted_iota(jnp.int32, (MOBA_BLOCK, MOBA_BLOCK), 0)
    qry = lax.broadcasted_iota(jnp.int32, (MOBA_BLOCK, MOBA_BLOCK), 1)
    base = qry - key
    for d in range(BIAS_TILES):
        dist = base + d * MOBA_BLOCK
        n = jnp.maximum(dist, 0)
        val = jnp.full((MOBA_BLOCK, MOBA_BLOCK), rb_ref[REL_BUCKETS - 1, h] * LOG2E, f32)
        for k in range(REL_BUCKETS - 2, -1, -1):
            val = jnp.where(n < T5_THR[k], rb_ref[k, h] * LOG2E, val)
        if d == 0:
            val = jnp.where(dist >= 0, val, NEG_BIG)
        out_ref[d] = val


def _bias_tiles(rel_bias):
    return pl.pallas_call(
        _bias_kernel,
        grid=(ATT_HEADS,),
        in_specs=[pl.BlockSpec(memory_space=pltpu.SMEM)],
        out_specs=pl.BlockSpec((None, BIAS_TILES, MOBA_BLOCK, MOBA_BLOCK), lambda h: (h, 0, 0, 0)),
        out_shape=jax.ShapeDtypeStruct((ATT_HEADS, BIAS_TILES, MOBA_BLOCK, MOBA_BLOCK), f32),
        name="bias_tiles",
    )(rel_bias)


def _log_sigmoid(v):
    return jnp.minimum(v, 0.0) - jnp.log1p(jnp.exp(-jnp.abs(v)))


def _inproj_kernel(x_ref, g_ref, wq_ref, wv_ref, wr_ref, wx_ref, cw_ref, cb_ref, wmq_ref, wmk_ref,
                   wmv_ref, wg_ref, bg_ref,
                   qT_ref, vT_ref, pr_ref, xc_ref, mqT_ref, mk_ref, mvT_ref, ga_ref, gb_ref, gl_ref,
                   xpad_ref, *, tm, seq, q_scale, k_scale):
    i = pl.program_id(0)
    W = MLSTM_WIDTH
    L = MLSTM_CHUNK
    x = x_ref[...]
    ms = jnp.mean(x * x, axis=-1, keepdims=True)
    h = (x * lax.rsqrt(ms + RMS_EPS) * g_ref[...]).astype(bf16)

    @pl.when((i * tm) % seq == 0)
    def _():
        xpad_ref[0:HALO_ROWS, :] = jnp.zeros((HALO_ROWS, W), f32)

    for cc in range(W // 512):
        cols = slice(cc * 512, (cc + 1) * 512)
        xpad_ref[HALO_ROWS:HALO_ROWS + tm, cols] = _dot(h, wx_ref[:, cols])

    def att_qv(cc):
        rows = slice(cc * 256, (cc + 1) * 256)
        qt = (_dot(h, wq_ref[:, rows]) * q_scale).T
        vt = _dot(h, wv_ref[:, rows]).T
        for u in range(tm // MOBA_BLOCK):
            cols = slice(u * MOBA_BLOCK, (u + 1) * MOBA_BLOCK)
            qT_ref[u, rows, :] = qt[:, cols].astype(bf16)
            vT_ref[u, rows, :] = vt[:, cols].astype(bf16)

    def token_major(cc):
        cols = slice(cc * 512, (cc + 1) * 512)
        pr_ref[:, cols] = _dot(h, wr_ref[:, cols]).astype(bf16)

    def conv_silu(g):
        sl = slice(g * 256, (g + 1) * 256)
        acc = cb_ref[:, sl] + cw_ref[CONV_WIDTH - 1:CONV_WIDTH, sl] * xpad_ref[HALO_ROWS:HALO_ROWS + tm, sl]
        for j in range(CONV_WIDTH - 1):
            off = HALO_ROWS - (CONV_WIDTH - 1) + j
            acc = acc + cw_ref[j:j + 1, sl] * xpad_ref[off:off + tm, sl]
        xc_ref[:, sl] = (acc * jax.nn.sigmoid(acc)).astype(bf16)
        xpad_ref[0:HALO_ROWS, sl] = xpad_ref[tm:tm + HALO_ROWS, sl]

    k_unscale = 1.0 / k_scale
    gates = [jnp.zeros((tm, LANES), f32)]

    def block_diag_qkv(g):
        sl = slice(g * 256, (g + 1) * 256)
        xc_g = xc_ref[:, sl]
        xm_g = xpad_ref[HALO_ROWS:HALO_ROWS + tm, sl].astype(bf16)
        q = _dot(xc_g, wmq_ref[g])
        v = _dot(xm_g, wmv_ref[g])
        q_bf = q.astype(bf16)
        v_bf = v.astype(bf16)
        kk = (_dot(xc_g, wmk_ref[g]) * k_scale).astype(bf16)
        mk_ref[:, sl] = kk
        qT = q.T.astype(bf16)
        vT = v.T.astype(bf16)
        for u in range(tm // L):
            cols = slice(u * L, (u + 1) * L)
            mqT_ref[u, sl, :] = qT[:, cols]
            mvT_ref[u, sl, :] = vT[:, cols]
        ks = slice(W + g * 256, W + (g + 1) * 256)
        vs = slice(2 * W + g * 256, 2 * W + (g + 1) * 256)
        gates[0] = (gates[0] + _dot(q_bf, wg_ref[sl, :]) + k_unscale * _dot(kk, wg_ref[ks, :])
                    + _dot(v_bf, wg_ref[vs, :]))

    def gate_outputs():
        gates_t = (gates[0] + bg_ref[...]).T
        li = gates_t[0:GATE_ROWS, :] * LOG2E
        lf = _log_sigmoid(gates_t[GATE_ROWS:2 * GATE_ROWS, :]) * LOG2E
        row = lax.broadcasted_iota(jnp.int32, (L, L), 0)
        col = lax.broadcasted_iota(jnp.int32, (L, L), 1)
        upper = jnp.where(row <= col, 1.0, 0.0).astype(bf16)
        gl_ref[...] = li
        for u in range(tm // L):
            cols = slice(u * L, (u + 1) * L)
            b = sum(_dot(part, upper) for part in _split3(lf[:, cols]))
            gb_ref[:, cols] = b
            a_pad = jnp.concatenate([li[:, cols] - b, jnp.zeros((LANES - GATE_ROWS, L), f32)], axis=0)
            ga_ref[cols, :] = a_pad.T[:, 0:GATE_ROWS]

    big = ([functools.partial(token_major, cc) for cc in range(pr_ref.shape[1] // 512)]
           + [functools.partial(att_qv, cc) for cc in range(ATT_WIDTH // 256)])
    conv = [functools.partial(conv_silu, g) for g in range(W // 256)]
    proj = [functools.partial(block_diag_qkv, g) for g in range(W // 256)]
    prep = conv + proj + [gate_outputs]
    assert len(big) >= len(prep)
    for n, task in enumerate(big):
        task()
        if n < len(prep):
            prep[n]()


def _inproj(x2, g_pre, wq, wv, wr, wx, conv_w, conv_b, wmq, wmk, wmv, wg, bg, seq):
    T, D = x2.shape
    W = MLSTM_WIDTH
    tm = TOKEN_TILE
    L = MLSTM_CHUNK
    nblk = tm // MOBA_BLOCK
    ncols = wr.shape[1]
    resident = dict(pipeline_mode=pl.Buffered(1))
    const2 = lambda i: (0, 0)
    const3 = lambda i: (0, 0, 0)
    tok = pl.BlockSpec((tm, W), lambda i: (i, 0))
    att_T = pl.BlockSpec((nblk, ATT_WIDTH, MOBA_BLOCK), lambda i: (i, 0, 0))
    chunk_T = pl.BlockSpec((tm // L, W, L), lambda i: (i, 0, 0))
    rows8 = pl.BlockSpec((GATE_ROWS, tm), lambda i: (0, i))
    return pl.pallas_call(
        functools.partial(_inproj_kernel, tm=tm, seq=seq, q_scale=ATT_HEAD_DIM ** -0.5 * LOG2E,
                          k_scale=MLSTM_HEAD_DIM ** -0.5),
        grid=(T // tm,),
        in_specs=[
            pl.BlockSpec((tm, D), lambda i: (i, 0)),
            pl.BlockSpec((1, D), const2),
            pl.BlockSpec((D, ATT_WIDTH), const2, **resident),
            pl.BlockSpec((D, ATT_WIDTH), const2, **resident),
            pl.BlockSpec((D, ncols), const2, **resident),
            pl.BlockSpec((D, W), const2, **resident),
            pl.BlockSpec((CONV_WIDTH, W), const2),
            pl.BlockSpec((1, W), const2),
            pl.BlockSpec((W // 256, 256, 256), const3, **resident),
            pl.BlockSpec((W // 256, 256, 256), const3, **resident),
            pl.BlockSpec((W // 256, 256, 256), const3, **resident),
            pl.BlockSpec((3 * W, LANES), const2, **resident),
            pl.BlockSpec((1, LANES), const2),
        ],
        out_specs=[att_T, att_T, pl.BlockSpec((tm, ncols), lambda i: (i, 0)),
                   tok, chunk_T, tok, chunk_T,
                   pl.BlockSpec((tm, GATE_ROWS), lambda i: (i, 0)), rows8, rows8],
        out_shape=[
            jax.ShapeDtypeStruct((T // MOBA_BLOCK, ATT_WIDTH, MOBA_BLOCK), bf16),
            jax.ShapeDtypeStruct((T // MOBA_BLOCK, ATT_WIDTH, MOBA_BLOCK), bf16),
            jax.ShapeDtypeStruct((T, ncols), bf16),
            jax.ShapeDtypeStruct((T, W), bf16),
            jax.ShapeDtypeStruct((T // L, W, L), bf16),
            jax.ShapeDtypeStruct((T, W), bf16),
            jax.ShapeDtypeStruct((T // L, W, L), bf16),
            jax.ShapeDtypeStruct((T, GATE_ROWS), f32),
            jax.ShapeDtypeStruct((GATE_ROWS, T), f32),
            jax.ShapeDtypeStruct((GATE_ROWS, T), f32),
        ],
        scratch_shapes=[pltpu.VMEM((HALO_ROWS + tm, W), f32)],
        compiler_params=pltpu.CompilerParams(
            dimension_semantics=("arbitrary",), vmem_limit_bytes=VMEM_LIMIT),
        name="inproj",
    )(x2, g_pre, wq, wv, wr, wx, conv_w, conv_b, wmq, wmk, wmv, wg, bg)


def _mlstm_kernel(qT_ref, k_ref, vT_ref, xc_ref, z_ref, ga_ref, gb_ref, gl_ref, nw_ref, sk_ref,
                  y_ref, ct_ref, m_ref):
    L = MLSTM_CHUNK
    dh = MLSTM_HEAD_DIM

    @pl.when(pl.program_id(1) == 0)
    def _():
        ct_ref[...] = jnp.zeros_like(ct_ref)
        m_ref[...] = jnp.zeros_like(m_ref)

    s_idx = lax.broadcasted_iota(jnp.int32, (L, L), 0)
    t_idx = lax.broadcasted_iota(jnp.int32, (L, L), 1)
    causal = s_idx <= t_idx
    ones_rows = jnp.ones((SUM_ROWS, L), bf16)
    for u, h in ((u, h) for u in range(MLSTM_CHUNKS_PER_STEP) for h in range(MLSTM_HEADS)):
        sl = slice(h * dh, (h + 1) * dh)
        tok = slice(u * L, (u + 1) * L)
        qT = qT_ref[u, sl, :]
        k = k_ref[tok, sl]
        vT_ext = jnp.concatenate([vT_ref[u, sl, :], ones_rows], axis=0)
        a_c = ga_ref[tok, h:h + 1]
        b_r = gb_ref[h:h + 1, tok]
        li_r = gl_ref[h:h + 1, tok]
        b_last = b_r[:, L - 1:L]
        m_prev = m_ref[h][0:1, 0:1]
        ct = ct_ref[h]

        log_d = jnp.where(causal, a_c + b_r, NEG_BIG)
        inter = b_r + m_prev
        m_t = jnp.maximum(inter, jnp.max(log_d, axis=0, keepdims=True))
        sT = _dot(k, qT) * jnp.exp2(log_d - m_t)
        dec = jnp.exp2(inter - m_t)
        num = _dot(vT_ext, sT.astype(bf16)) + dec * _dot(ct.astype(bf16), qT)
        den = num[dh:dh + 1, :]
        hT = num[0:dh, :] * (1.0 / jnp.maximum(jnp.abs(den), jnp.exp2(-m_t)))

        mu = jnp.mean(hT, axis=0, keepdims=True)
        cen = hT - mu
        var = jnp.mean(cen * cen, axis=0, keepdims=True)
        yn = (cen * lax.rsqrt(var + LN_EPS)).T
        z = z_ref[tok, sl].astype(f32)
        out = (yn * nw_ref[:, sl] + sk_ref[:, sl] * xc_ref[tok, sl].astype(f32)) * (z * jax.nn.sigmoid(z))
        y_ref[tok, sl] = out.astype(y_ref.dtype)

        log_w = b_last - b_r + li_r
        m_new = jnp.maximum(b_last + m_prev, jnp.max(log_w, axis=1, keepdims=True))
        vw = vT_ext * jnp.exp2(log_w - m_new).astype(bf16)
        ct_ref[h] = jnp.exp2(b_last + m_prev - m_new) * ct + _dot(vw, k)
        m_ref[h] = jnp.broadcast_to(m_new, m_ref.shape[1:])


def _mlstm(qT, k, vT, xc, pr, ga, gb, gl, mh_norm, skip, batch, seq):
    T, W = k.shape
    cps = MLSTM_CHUNKS_PER_STEP
    rows = cps * MLSTM_CHUNK
    assert seq % rows == 0
    nc = seq // rows
    z_col = 2
    tok = pl.BlockSpec((rows, W), lambda b, c: (b * nc + c, 0))
    chunkT = pl.BlockSpec((cps, W, MLSTM_CHUNK), lambda b, c: (b * nc + c, 0, 0))
    rows8 = pl.BlockSpec((GATE_ROWS, rows), lambda b, c: (0, b * nc + c))
    return pl.pallas_call(
        _mlstm_kernel,
        grid=(batch, nc),
        in_specs=[chunkT, tok, chunkT, tok,
                  pl.BlockSpec((rows, W), lambda b, c: (b * nc + c, z_col)),
                  pl.BlockSpec((rows, GATE_ROWS), lambda b, c: (b * nc + c, 0)),
                  rows8, rows8,
                  pl.BlockSpec((1, W), lambda b, c: (0, 0)),
                  pl.BlockSpec((1, W), lambda b, c: (0, 0))],
        out_specs=tok,
        out_shape=jax.ShapeDtypeStruct((T, W), bf16),
        scratch_shapes=[pltpu.VMEM((MLSTM_HEADS, MLSTM_HEAD_DIM + SUM_ROWS, MLSTM_HEAD_DIM), f32),
                        pltpu.VMEM((MLSTM_HEADS, 8, LANES), f32)],
        compiler_params=pltpu.CompilerParams(
            dimension_semantics=("arbitrary", "arbitrary"), vmem_limit_bytes=VMEM_LIMIT),
        name="mlstm",
    )(qT, k, vT, xc, pr, ga, gb, gl, mh_norm, skip)


def _moba_kernel(*refs, nb, heads):
    def query_block(sub, c):
        _moba_query_block(sub, *refs, nb=nb, heads=heads)
        return c
    lax.fori_loop(0, MOBA_QBLOCKS_PER_STEP, query_block, 0)


def _moba_query_block(sub, rb_ref, qT_ref, k_ref, vT_ref, g_ref, bias_ref, o_ref, kmean_ref, sel_ref, sa_ref,
                      sb_ref, ca_ref, cb_ref, acc_ref, m_ref, alpha_ref, p_ref, *, nb, heads):
    blk_len = MOBA_BLOCK
    dh = ATT_HEAD_DIM
    own = pl.program_id(2) * MOBA_QBLOCKS_PER_STEP + sub
    q_rows = pl.ds(pl.multiple_of(sub * blk_len, blk_len), blk_len)
    qT_ref = qT_ref.at[sub]
    g_ref = g_ref.at[q_rows]
    o_ref = o_ref.at[q_rows]

    @pl.when(own == 0)
    def _():
        def mean_body(j, c):
            kb = k_ref[pl.ds(pl.multiple_of(j * blk_len, blk_len), blk_len), :].astype(f32)
            kmean_ref[pl.ds(j, 1), :] = jnp.sum(kb, axis=0, keepdims=True) * (1.0 / blk_len)
            return c
        lax.fori_loop(0, nb, mean_body, 0)

    hsl = [slice(hh * dh, (hh + 1) * dh) for hh in range(heads)]

    def score_matmuls(i, buf, with_tile):
        dst_ref, cmax_ref = buf
        ic = jnp.minimum(i, own)
        rows = pl.ds(pl.multiple_of((own - ic) * blk_len, blk_len), blk_len)
        for hh in range(heads):
            s = _dot(k_ref[rows, hsl[hh]], qT_ref[hsl[hh], :])
            if with_tile:
                s = s + bias_ref[hh, jnp.minimum(ic, NEAR_TILES)]
            dst_ref[hh] = s
            cmax_ref[hh] = jnp.max(s, axis=0, keepdims=True)

    def select_blocks():
        blk = lax.broadcasted_iota(jnp.int32, (nb, blk_len), 0)
        past = blk < own
        for hh in range(heads):
            qT = qT_ref[hsl[hh], :]
            km = kmean_ref[:, hsl[hh]]
            km_hi = km.astype(bf16)
            km_lo = (km - km_hi.astype(f32)).astype(bf16)
            gate = _dot(km_hi, qT) + _dot(km_lo, qT)
            g = jnp.where(past, gate, -jnp.inf)
            sel = blk == own
            for _ in range(MOBA_TOPK):
                mx = jnp.max(g, axis=0, keepdims=True)
                first = jnp.min(jnp.where(g == mx, blk, nb), axis=0, keepdims=True)
                pick = blk == first
                sel = sel | (pick & past)
                g = jnp.where(pick, -jnp.inf, g)
            sel_ref[hh, 0:nb, :] = jnp.where(sel, 0.0, NEG_BIG)
            sel_ref[hh, nb:nb + SEL_PAD, :] = jnp.full((SEL_PAD, blk_len), NEG_BIG, f32)

    ones_rows = jnp.ones((SUM_ROWS, blk_len), bf16)
    m_ref[...] = jnp.full(m_ref.shape, NEG_BIG, f32)
    acc_ref[...] = jnp.zeros(acc_ref.shape, f32)

    head0 = pl.program_id(0) * heads
    far_bias = [rb_ref[REL_BUCKETS - 1, head0 + hh] * LOG2E for hh in range(heads)]

    def softmax(i, buf):
        src_ref, cmax_ref = buf
        mask_row = jnp.where(i <= own, own - i, nb)
        stats = []
        for hh in range(heads):
            m = m_ref[hh]
            shift = jnp.where(i >= FIRST_BARE_STEP, far_bias[hh], 0.0)
            selrow = sel_ref[hh, pl.ds(mask_row, 1), :]
            m_new = jnp.maximum(m, cmax_ref[hh] + (selrow + shift))
            p = jnp.exp2(src_ref[hh] - jnp.where(selrow < 0.0, -NEG_BIG, m_new - shift)).astype(bf16)
            m_ref[hh] = m_new
            stats.append((jnp.exp2(m - m_new), p))
        return stats

    def value_update(i, stats):
        block = jnp.maximum(own - i, 0)
        for hh in range(heads):
            alpha, p = stats[hh]
            lhs = jnp.concatenate([vT_ref[block, hsl[hh], :], ones_rows], axis=0)
            acc_ref[hh] = alpha * acc_ref[hh] + _dot(lhs, p)

    def keep_pending(stats):
        for hh in range(heads):
            alpha_ref[hh], p_ref[hh] = stats[hh]

    even_buf, odd_buf = (sa_ref, ca_ref), (sb_ref, cb_ref)
    score_buf = (even_buf, odd_buf)

    def trip(i, n, with_tile):
        value_update(i, [(alpha_ref[hh], p_ref[hh]) for hh in range(heads)])
        score_matmuls(i + 2, even_buf, with_tile(i + 2))
        for t in range(1, n):
            stats = softmax(i + t, score_buf[t % 2])
            if n > 2:
                score_matmuls(i + 2 + t, score_buf[t % 2], with_tile(i + 2 + t))
                value_update(i + t, stats)
            else:
                value_update(i + t, stats)
                score_matmuls(i + 2 + t, score_buf[t % 2], with_tile(i + 2 + t))
        keep_pending(softmax(i + n, even_buf))

    select_blocks()
    score_matmuls(0, even_buf, True)
    score_matmuls(1, odd_buf, True)
    keep_pending(softmax(0, even_buf))

    n_steps = own + 1
    n_bare = jnp.maximum(n_steps - TILED_STEPS, 0)
    n_long = (n_bare + LONG_TRIP_SLACK) // LONG_TRIP
    n_short = (jnp.maximum(n_bare - LONG_TRIP * n_long, 0) + 1) // 2

    whole_tiled = n_steps >= TILED_STEPS - TILED_TRIP_SLACK

    def tiled_long_body(t, c):
        trip(0, TILED_STEPS, lambda step: step < FIRST_BARE_STEP)
        return c

    def tiled_short_body(t, c):
        trip(2 * t, 2, lambda step: True)
        return c

    def long_body(t, c):
        trip(TILED_STEPS + LONG_TRIP * t, LONG_TRIP, lambda step: False)
        return c

    def short_body(t, c):
        trip(TILED_STEPS + LONG_TRIP * n_long + 2 * t, 2, lambda step: False)
        return c

    lax.fori_loop(0, jnp.where(whole_tiled, 1, 0), tiled_long_body, 0)
    lax.fori_loop(0, jnp.where(whole_tiled, 0, (n_steps + 1) // 2), tiled_short_body, 0)
    lax.fori_loop(0, n_long, long_body, 0)
    lax.fori_loop(0, n_short, short_body, 0)
    for hh in range(heads):
        acc = acc_ref[hh]
        o = (acc[0:dh, :] * (1.0 / acc[dh:dh + 1, :])).T
        gg = g_ref[:, hsl[hh]].astype(f32)
        o_ref[:, hsl[hh]] = (o * (gg * jax.nn.sigmoid(gg))).astype(o_ref.dtype)


def _moba(rel_bias, qT, vT, pr, bias_tiles, batch, seq):
    T = pr.shape[0]
    nb = seq // MOBA_BLOCK
    G = MOBA_HEADS_PER_STEP
    gw = G * ATT_HEAD_DIM
    gate_col0 = ATT_WIDTH // gw
    qb = MOBA_QBLOCKS_PER_STEP
    assert nb % qb == 0
    steps = nb // qb
    vT4 = vT.reshape(batch, nb, ATT_WIDTH, MOBA_BLOCK)
    return pl.pallas_call(
        functools.partial(_moba_kernel, nb=nb, heads=G),
        grid=(ATT_HEADS // G, batch, steps),
        in_specs=[
            pl.BlockSpec(memory_space=pltpu.SMEM),
            pl.BlockSpec((qb, gw, MOBA_BLOCK), lambda h, b, i: (b * steps + i, h, 0)),
            pl.BlockSpec((seq, gw), lambda h, b, i: (b, h)),
            pl.BlockSpec((None, nb, gw, MOBA_BLOCK), lambda h, b, i: (b, 0, h, 0)),
            pl.BlockSpec((qb * MOBA_BLOCK, gw), lambda h, b, i: (b * steps + i, gate_col0 + h)),
            pl.BlockSpec((G, BIAS_TILES, MOBA_BLOCK, MOBA_BLOCK), lambda h, b, i: (h, 0, 0, 0),
                         pipeline_mode=pl.Buffered(1)),
        ],
        out_specs=pl.BlockSpec((qb * MOBA_BLOCK, gw), lambda h, b, i: (b * steps + i, h)),
        out_shape=jax.ShapeDtypeStruct((T, ATT_WIDTH), bf16),
        scratch_shapes=[pltpu.VMEM((nb, gw), f32),
                        pltpu.VMEM((G, nb + SEL_PAD, MOBA_BLOCK), f32),
                        pltpu.VMEM((G, MOBA_BLOCK, MOBA_BLOCK), f32),
                        pltpu.VMEM((G, MOBA_BLOCK, MOBA_BLOCK), f32),
                        pltpu.VMEM((G, 1, MOBA_BLOCK), f32),
                        pltpu.VMEM((G, 1, MOBA_BLOCK), f32),
                        pltpu.VMEM((G, ATT_HEAD_DIM + SUM_ROWS, MOBA_BLOCK), f32),
                        pltpu.VMEM((G, 1, MOBA_BLOCK), f32),
                        pltpu.VMEM((G, 1, MOBA_BLOCK), f32),
                        pltpu.VMEM((G, MOBA_BLOCK, MOBA_BLOCK), bf16)],
        compiler_params=pltpu.CompilerParams(
            dimension_semantics=("arbitrary", "arbitrary", "arbitrary"),
            vmem_limit_bytes=VMEM_LIMIT),
        name="moba",
    )(rel_bias, qT, pr, vT4, pr, bias_tiles)


def _outproj_kernel(ya_ref, ym_ref, wa_ref, wm_ref, g_ref, x_ref, o_ref):
    y = _dot(ya_ref[...], wa_ref[...]) + _dot(ym_ref[...], wm_ref[...])
    ms = jnp.mean(y * y, axis=-1, keepdims=True)
    o_ref[...] = x_ref[...] + y * lax.rsqrt(ms + RMS_EPS) * g_ref[...]


def _outproj(ya, ym, wa, wm, g_post, x2):
    T, D = x2.shape
    tm = OUT_TOKEN_TILE
    assert T % tm == 0
    resident = dict(pipeline_mode=pl.Buffered(1))
    return pl.pallas_call(
        _outproj_kernel,
        grid=(T // tm,),
        in_specs=[
            pl.BlockSpec((tm, ya.shape[1]), lambda i: (i, 0)),
            pl.BlockSpec((tm, ym.shape[1]), lambda i: (i, 0)),
            pl.BlockSpec(wa.shape, lambda i: (0, 0), **resident),
            pl.BlockSpec(wm.shape, lambda i: (0, 0), **resident),
            pl.BlockSpec((1, D), lambda i: (0, 0)),
            pl.BlockSpec((tm, D), lambda i: (i, 0)),
        ],
        out_specs=pl.BlockSpec((tm, D), lambda i: (i, 0)),
        out_shape=jax.ShapeDtypeStruct((T, D), f32),
        compiler_params=pltpu.CompilerParams(
            dimension_semantics=("arbitrary",), vmem_limit_bytes=VMEM_LIMIT),
        name="outproj",
    )(ya, ym, wa, wm, g_post, x2)


def _block_diag_256(w):
    width = w.shape[0] * QKV_BLOCK
    rows = w.reshape(width, QKV_BLOCK)
    col = np.arange(256)
    spread = jnp.asarray(col[None, :] % QKV_BLOCK == np.arange(QKV_BLOCK)[:, None], w.dtype)
    tiled = jnp.dot(rows, spread, precision=lax.Precision.HIGHEST)
    same_block = (np.arange(width)[:, None] % 256) // QKV_BLOCK == col[None, :] // QKV_BLOCK
    dense = jnp.where(jnp.asarray(same_block), tiled, 0.0)
    return dense.reshape(-1, 256, 256)


def _gate_lanes(a):
    nh = MLSTM_HEADS
    zeros = lambda n: jnp.zeros(a.shape[:-1] + (n,), a.dtype)
    return jnp.concatenate([a[..., 0:nh], zeros(GATE_ROWS - nh), a[..., nh:2 * nh],
                            zeros(LANES - GATE_ROWS - nh)], axis=-1)


def _layer(x, rel_bias, g_pre, g_post, w_in, conv_w, conv_b, wq_m, wk_m, wv_m,
           w_if, b_if, mh_norm, skip, w_out):
    batch, seq, d_model = x.shape
    assert seq % TOKEN_TILE == 0 and TOKEN_TILE % MLSTM_CHUNK == 0 and TOKEN_TILE % MOBA_BLOCK == 0
    aw, nh = ATT_WIDTH, MLSTM_HEADS
    x2 = x.reshape(batch * seq, d_model)

    w_bf = w_in.astype(bf16)
    wq = w_bf[:, 0:aw]
    wv = w_bf[:, 2 * aw:3 * aw]
    mw = MLSTM_WIDTH
    wr = jnp.concatenate([w_bf[:, aw:2 * aw], w_bf[:, 3 * aw:4 * aw], w_bf[:, 4 * aw + mw:]], axis=1)
    wx = w_bf[:, 4 * aw:4 * aw + mw]
    wq_bd = _block_diag_256(wq_m).astype(bf16)
    wk_bd = _block_diag_256(wk_m).astype(bf16)
    wv_bd = _block_diag_256(wv_m).astype(bf16)
    wg = _gate_lanes(w_if).astype(bf16)
    bg = _gate_lanes(b_if).reshape(1, LANES)
    w_out_bf = w_out.astype(bf16)

    bias_tiles = _bias_tiles(rel_bias)
    qT, vT, pr, xc, qmT, km, vmT, ga, gb, gl = _inproj(
        x2, g_pre.reshape(1, -1), wq, wv, wr, wx, conv_w, conv_b.reshape(1, -1),
        wq_bd, wk_bd, wv_bd, wg, bg, seq)
    ym = _mlstm(qmT, km, vmT, xc, pr, ga, gb, gl, mh_norm.reshape(1, -1), skip.reshape(1, -1),
                batch, seq)
    ya = _moba(rel_bias, qT, vT, pr, bias_tiles, batch, seq)
    out = _outproj(ya, ym, w_out_bf[0:aw], w_out_bf[aw:], g_post.reshape(1, -1), x2)
    return out.reshape(batch, seq, d_model)


def kernel(x, rel_bias, g_pre, g_post, w_in, conv_w, conv_b, wq_m, wk_m, wv_m, w_if, b_if,
           mh_norm, skip, w_out):
    depth = w_in.shape[0]
    for l in range(depth):
        x = _layer(x, rel_bias, g_pre[l], g_post[l], w_in[l], conv_w[l], conv_b[l], wq_m[l],
                   wk_m[l], wv_m[l], w_if[l], b_if[l], mh_norm[l], skip[l], w_out[l])
    return x
```

```python
import functools
import math

import jax
import jax.numpy as jnp
import numpy as np
from jax import lax
from jax.experimental import pallas as pl
from jax.experimental.pallas import tpu as pltpu

f32 = jnp.float32
bf16 = jnp.bfloat16

ATT_HEADS = 8
ATT_HEAD_DIM = 128
ATT_WIDTH = ATT_HEADS * ATT_HEAD_DIM
MOBA_BLOCK = 256
MOBA_TOPK = 3
REL_BUCKETS = 32
REL_MAX_DIST = 2048
MLSTM_HEADS = 4
MLSTM_WIDTH = 1024
MLSTM_HEAD_DIM = MLSTM_WIDTH // MLSTM_HEADS
QKV_BLOCK = 4
CONV_WIDTH = 4
RMS_EPS = 1e-6
LN_EPS = 1e-5

MLSTM_CHUNK = 256
TOKEN_TILE = 512
OUT_TOKEN_TILE = 1024
MOBA_HEADS_PER_STEP = 4
MOBA_QBLOCKS_PER_STEP = 2
MLSTM_CHUNKS_PER_STEP = 4
LONG_TRIP = 8
MID_TRIP = 4
LONG_TRIP_SLACK = 1
TILED_TRIP_SLACK = 3
HALO_ROWS = 16
SUM_ROWS = 16
GATE_ROWS = 8
SEL_PAD = 8
LANES = 128
NEG_BIG = -1e30
VMEM_LIMIT = 56 * 1024 * 1024
LOG2E = math.log2(math.e)

_NT = (((1,), (1,)), ((), ()))


def _t5_thresholds():
    n = np.arange(0, 2 * REL_MAX_DIST, dtype=np.int64)
    max_exact = REL_BUCKETS // 2
    nf = np.maximum(n, 1).astype(np.float32)
    large = max_exact + (np.log(nf / np.float32(max_exact))
                         / np.float32(math.log(REL_MAX_DIST / max_exact))
                         * np.float32(REL_BUCKETS - max_exact)).astype(np.int32)
    large = np.minimum(large, REL_BUCKETS - 1)
    bucket = np.where(n < max_exact, n, large)
    assert np.all(np.diff(bucket) >= 0)
    return [int(np.argmax(bucket >= k)) for k in range(1, REL_BUCKETS)]


T5_THR = _t5_thresholds()
NEAR_TILES = -(-(T5_THR[-1] + MOBA_BLOCK - 1) // MOBA_BLOCK)
assert NEAR_TILES * MOBA_BLOCK - (MOBA_BLOCK - 1) >= T5_THR[-1]
BIAS_TILES = NEAR_TILES + 1
TILED_STEPS = 2 * (-(-NEAR_TILES // 2))
FIRST_BARE_STEP = NEAR_TILES
assert 2 <= FIRST_BARE_STEP <= TILED_STEPS


def _dot(a, b):
    return jnp.dot(a, b, preferred_element_type=f32)


def _dot_nt(a, b):
    return lax.dot_general(a, b, _NT, preferred_element_type=f32)


def _split3(x):
    hi = x.astype(bf16)
    r = x - hi.astype(f32)
    mid = r.astype(bf16)
    lo = (r - mid.astype(f32)).astype(bf16)
    return hi, mid, lo


def _bias_kernel(rb_ref, out_ref):
    h = pl.program_id(0)
    key = lax.broadcasted_iota(jnp.int32, (MOBA_BLOCK, MOBA_BLOCK), 0)
    qry = lax.broadcasted_iota(jnp.int32, (MOBA_BLOCK, MOBA_BLOCK), 1)
    base = qry - key
    for d in range(BIAS_TILES):
        dist = base + d * MOBA_BLOCK
        n = jnp.maximum(dist, 0)
        val = jnp.full((MOBA_BLOCK, MOBA_BLOCK), rb_ref[REL_BUCKETS - 1, h] * LOG2E, f32)
        for k in range(REL_BUCKETS - 2, -1, -1):
            val = jnp.where(n < T5_THR[k], rb_ref[k, h] * LOG2E, val)
        if d == 0:
            val = jnp.where(dist >= 0, val, NEG_BIG)
        out_ref[d] = val


def _bias_tiles(rel_bias):
    return pl.pallas_call(
        _bias_kernel,
        grid=(ATT_HEADS,),
        in_specs=[pl.BlockSpec(memory_space=pltpu.SMEM)],
        out_specs=pl.BlockSpec((None, BIAS_TILES, MOBA_BLOCK, MOBA_BLOCK), lambda h: (h, 0, 0, 0)),
        out_shape=jax.ShapeDtypeStruct((ATT_HEADS, BIAS_TILES, MOBA_BLOCK, MOBA_BLOCK), f32),
        name="bias_tiles",
    )(rel_bias)


def _log_sigmoid(v):
    return jnp.minimum(v, 0.0) - jnp.log1p(jnp.exp(-jnp.abs(v)))


def _inproj_kernel(x_ref, g_ref, wq_ref, wv_ref, wr_ref, wx_ref, cw_ref, cb_ref, wmq_ref, wmk_ref,
                   wmv_ref, wg_ref, bg_ref,
                   qT_ref, vT_ref, pr_ref, xc_ref, mqT_ref, mk_ref, mvT_ref, ga_ref, gb_ref, gl_ref,
                   xpad_ref, *, tm, seq, q_scale, k_scale):
    i = pl.program_id(0)
    W = MLSTM_WIDTH
    L = MLSTM_CHUNK
    x = x_ref[...]
    ms = jnp.mean(x * x, axis=-1, keepdims=True)
    h = (x * lax.rsqrt(ms + RMS_EPS) * g_ref[...]).astype(bf16)

    @pl.when((i * tm) % seq == 0)
    def _():
        xpad_ref[0:HALO_ROWS, :] = jnp.zeros((HALO_ROWS, W), f32)

    for cc in range(W // 512):
        cols = slice(cc * 512, (cc + 1) * 512)
        xpad_ref[HALO_ROWS:HALO_ROWS + tm, cols] = _dot(h, wx_ref[:, cols])

    def att_qv(cc):
        rows = slice(cc * 256, (cc + 1) * 256)
        qt = (_dot(h, wq_ref[:, rows]) * q_scale).T
        vt = _dot(h, wv_ref[:, rows]).T
        for u in range(tm // MOBA_BLOCK):
            cols = slice(u * MOBA_BLOCK, (u + 1) * MOBA_BLOCK)
            qT_ref[u, rows, :] = qt[:, cols].astype(bf16)
            vT_ref[u, rows, :] = vt[:, cols].astype(bf16)

    def token_major(cc):
        cols = slice(cc * 512, (cc + 1) * 512)
        pr_ref[:, cols] = _dot(h, wr_ref[:, cols]).astype(bf16)

    def conv_silu(g):
        sl = slice(g * 256, (g + 1) * 256)
        acc = cb_ref[:, sl] + cw_ref[CONV_WIDTH - 1:CONV_WIDTH, sl] * xpad_ref[HALO_ROWS:HALO_ROWS + tm, sl]
        for j in range(CONV_WIDTH - 1):
            off = HALO_ROWS - (CONV_WIDTH - 1) + j
            acc = acc + cw_ref[j:j + 1, sl] * xpad_ref[off:off + tm, sl]
        xc_ref[:, sl] = (acc * jax.nn.sigmoid(acc)).astype(bf16)
        xpad_ref[0:HALO_ROWS, sl] = xpad_ref[tm:tm + HALO_ROWS, sl]

    k_unscale = 1.0 / k_scale
    gates = [jnp.zeros((tm, LANES), f32)]

    def block_diag_qkv(g):
        sl = slice(g * 256, (g + 1) * 256)
        xc_g = xc_ref[:, sl]
        xm_g = xpad_ref[HALO_ROWS:HALO_ROWS + tm, sl].astype(bf16)
        q = _dot(xc_g, wmq_ref[g])
        v = _dot(xm_g, wmv_ref[g])
        q_bf = q.astype(bf16)
        v_bf = v.astype(bf16)
        kk = (_dot(xc_g, wmk_ref[g]) * k_scale).astype(bf16)
        mk_ref[:, sl] = kk
        qT = q.T.astype(bf16)
        vT = v.T.astype(bf16)
        for u in range(tm // L):
            cols = slice(u * L, (u + 1) * L)
            mqT_ref[u, sl, :] = qT[:, cols]
            mvT_ref[u, sl, :] = vT[:, cols]
        ks = slice(W + g * 256, W + (g + 1) * 256)
        vs = slice(2 * W + g * 256, 2 * W + (g + 1) * 256)
        gates[0] = (gates[0] + _dot(q_bf, wg_ref[sl, :]) + k_unscale * _dot(kk, wg_ref[ks, :])
                    + _dot(v_bf, wg_ref[vs, :]))

    def gate_outputs():
        gates_t = (gates[0] + bg_ref[...]).T
        li = gates_t[0:GATE_ROWS, :] * LOG2E
        lf = _log_sigmoid(gates_t[GATE_ROWS:2 * GATE_ROWS, :]) * LOG2E
        row = lax.broadcasted_iota(jnp.int32, (L, L), 0)
        col = lax.broadcasted_iota(jnp.int32, (L, L), 1)
        upper = jnp.where(row <= col, 1.0, 0.0).astype(bf16)
        gl_ref[...] = li
        for u in range(tm // L):
            cols = slice(u * L, (u + 1) * L)
            b = sum(_dot(part, upper) for part in _split3(lf[:, cols]))
            gb_ref[:, cols] = b
            a_pad = jnp.concatenate([li[:, cols] - b, jnp.zeros((LANES - GATE_ROWS, L), f32)], axis=0)
            ga_ref[cols, :] = a_pad.T[:, 0:GATE_ROWS]

    big = ([functools.partial(token_major, cc) for cc in range(pr_ref.shape[1] // 512)]
           + [functools.partial(att_qv, cc) for cc in range(ATT_WIDTH // 256)])
    conv = [functools.partial(conv_silu, g) for g in range(W // 256)]
    proj = [functools.partial(block_diag_qkv, g) for g in range(W // 256)]
    prep = conv + proj + [gate_outputs]
    assert len(big) >= len(prep)
    for n, task in enumerate(big):
        task()
        if n < len(prep):
            prep[n]()


def _inproj(x2, g_pre, wq, wv, wr, wx, conv_w, conv_b, wmq, wmk, wmv, wg, bg, seq):
    T, D = x2.shape
    W = MLSTM_WIDTH
    tm = TOKEN_TILE
    L = MLSTM_CHUNK
    nblk = tm // MOBA_BLOCK
    ncols = wr.shape[1]
    resident = dict(pipeline_mode=pl.Buffered(1))
    const2 = lambda i: (0, 0)
    const3 = lambda i: (0, 0, 0)
    tok = pl.BlockSpec((tm, W), lambda i: (i, 0))
    att_T = pl.BlockSpec((nblk, ATT_WIDTH, MOBA_BLOCK), lambda i: (i, 0, 0))
    chunk_T = pl.BlockSpec((tm // L, W, L), lambda i: (i, 0, 0))
    rows8 = pl.BlockSpec((GATE_ROWS, tm), lambda i: (0, i))
    return pl.pallas_call(
        functools.partial(_inproj_kernel, tm=tm, seq=seq, q_scale=ATT_HEAD_DIM ** -0.5 * LOG2E,
                          k_scale=MLSTM_HEAD_DIM ** -0.5),
        grid=(T // tm,),
        in_specs=[
            pl.BlockSpec((tm, D), lambda i: (i, 0)),
            pl.BlockSpec((1, D), const2),
            pl.BlockSpec((D, ATT_WIDTH), const2, **resident),
            pl.BlockSpec((D, ATT_WIDTH), const2, **resident),
            pl.BlockSpec((D, ncols), const2, **resident),
            pl.BlockSpec((D, W), const2, **resident),
            pl.BlockSpec((CONV_WIDTH, W), const2),
            pl.BlockSpec((1, W), const2),
            pl.BlockSpec((W // 256, 256, 256), const3, **resident),
            pl.BlockSpec((W // 256, 256, 256), const3, **resident),
            pl.BlockSpec((W // 256, 256, 256), const3, **resident),
            pl.BlockSpec((3 * W, LANES), const2, **resident),
            pl.BlockSpec((1, LANES), const2),
        ],
        out_specs=[att_T, att_T, pl.BlockSpec((tm, ncols), lambda i: (i, 0)),
                   tok, chunk_T, tok, chunk_T,
                   pl.BlockSpec((tm, GATE_ROWS), lambda i: (i, 0)), rows8, rows8],
        out_shape=[
            jax.ShapeDtypeStruct((T // MOBA_BLOCK, ATT_WIDTH, MOBA_BLOCK), bf16),
            jax.ShapeDtypeStruct((T // MOBA_BLOCK, ATT_WIDTH, MOBA_BLOCK), bf16),
            jax.ShapeDtypeStruct((T, ncols), bf16),
            jax.ShapeDtypeStruct((T, W), bf16),
            jax.ShapeDtypeStruct((T // L, W, L), bf16),
            jax.ShapeDtypeStruct((T, W), bf16),
            jax.ShapeDtypeStruct((T // L, W, L), bf16),
            jax.ShapeDtypeStruct((T, GATE_ROWS), f32),
            jax.ShapeDtypeStruct((GATE_ROWS, T), f32),
            jax.ShapeDtypeStruct((GATE_ROWS, T), f32),
        ],
        scratch_shapes=[pltpu.VMEM((HALO_ROWS + tm, W), f32)],
        compiler_params=pltpu.CompilerParams(
            dimension_semantics=("arbitrary",), vmem_limit_bytes=VMEM_LIMIT),
        name="inproj",
    )(x2, g_pre, wq, wv, wr, wx, conv_w, conv_b, wmq, wmk, wmv, wg, bg)


def _mlstm_kernel(qT_ref, k_ref, vT_ref, xc_ref, z_ref, ga_ref, gb_ref, gl_ref, nw_ref, sk_ref,
                  y_ref, ct_ref, m_ref):
    L = MLSTM_CHUNK
    dh = MLSTM_HEAD_DIM

    @pl.when(pl.program_id(1) == 0)
    def _():
        ct_ref[...] = jnp.zeros_like(ct_ref)
        m_ref[...] = jnp.zeros_like(m_ref)

    s_idx = lax.broadcasted_iota(jnp.int32, (L, L), 0)
    t_idx = lax.broadcasted_iota(jnp.int32, (L, L), 1)
    causal = s_idx <= t_idx
    ones_rows = jnp.ones((SUM_ROWS, L), bf16)
    for u, h in ((u, h) for u in range(MLSTM_CHUNKS_PER_STEP) for h in range(MLSTM_HEADS)):
        sl = slice(h * dh, (h + 1) * dh)
        tok = slice(u * L, (u + 1) * L)
        qT = qT_ref[u, sl, :]
        k = k_ref[tok, sl]
        vT_ext = jnp.concatenate([vT_ref[u, sl, :], ones_rows], axis=0)
        a_c = ga_ref[tok, h:h + 1]
        b_r = gb_ref[h:h + 1, tok]
        li_r = gl_ref[h:h + 1, tok]
        b_last = b_r[:, L - 1:L]
        m_prev = m_ref[h][0:1, 0:1]
        ct = ct_ref[h]

        log_d = jnp.where(causal, a_c + b_r, NEG_BIG)
        inter = b_r + m_prev
        m_t = jnp.maximum(inter, jnp.max(log_d, axis=0, keepdims=True))
        sT = _dot(k, qT) * jnp.exp2(log_d - m_t)
        dec = jnp.exp2(inter - m_t)
        num = _dot(vT_ext, sT.astype(bf16)) + dec * _dot(ct.astype(bf16), qT)
        den = num[dh:dh + 1, :]
        hT = num[0:dh, :] * (1.0 / jnp.maximum(jnp.abs(den), jnp.exp2(-m_t)))

        mu = jnp.mean(hT, axis=0, keepdims=True)
        cen = hT - mu
        var = jnp.mean(cen * cen, axis=0, keepdims=True)
        yn = (cen * lax.rsqrt(var + LN_EPS)).T
        z = z_ref[tok, sl].astype(f32)
        out = (yn * nw_ref[:, sl] + sk_ref[:, sl] * xc_ref[tok, sl].astype(f32)) * (z * jax.nn.sigmoid(z))
        y_ref[tok, sl] = out.astype(y_ref.dtype)

        log_w = b_last - b_r + li_r
        m_new = jnp.maximum(b_last + m_prev, jnp.max(log_w, axis=1, keepdims=True))
        vw = vT_ext * jnp.exp2(log_w - m_new).astype(bf16)
        ct_ref[h] = jnp.exp2(b_last + m_prev - m_new) * ct + _dot(vw, k)
        m_ref[h] = jnp.broadcast_to(m_new, m_ref.shape[1:])


def _mlstm(qT, k, vT, xc, pr, ga, gb, gl, mh_norm, skip, batch, seq):
    T, W = k.shape
    cps = MLSTM_CHUNKS_PER_STEP
    rows = cps * MLSTM_CHUNK
    assert seq % rows == 0
    nc = seq // rows
    z_col = 2
    tok = pl.BlockSpec((rows, W), lambda b, c: (b * nc + c, 0))
    chunkT = pl.BlockSpec((cps, W, MLSTM_CHUNK), lambda b, c: (b * nc + c, 0, 0))
    rows8 = pl.BlockSpec((GATE_ROWS, rows), lambda b, c: (0, b * nc + c))
    return pl.pallas_call(
        _mlstm_kernel,
        grid=(batch, nc),
        in_specs=[chunkT, tok, chunkT, tok,
                  pl.BlockSpec((rows, W), lambda b, c: (b * nc + c, z_col)),
                  pl.BlockSpec((rows, GATE_ROWS), lambda b, c: (b * nc + c, 0)),
                  rows8, rows8,
                  pl.BlockSpec((1, W), lambda b, c: (0, 0)),
                  pl.BlockSpec((1, W), lambda b, c: (0, 0))],
        out_specs=tok,
        out_shape=jax.ShapeDtypeStruct((T, W), bf16),
        scratch_shapes=[pltpu.VMEM((MLSTM_HEADS, MLSTM_HEAD_DIM + SUM_ROWS, MLSTM_HEAD_DIM), f32),
                        pltpu.VMEM((MLSTM_HEADS, 8, LANES), f32)],
        compiler_params=pltpu.CompilerParams(
            dimension_semantics=("arbitrary", "arbitrary"), vmem_limit_bytes=VMEM_LIMIT),
        name="mlstm",
    )(qT, k, vT, xc, pr, ga, gb, gl, mh_norm, skip)


def _moba_kernel(*refs, nb, heads):
    def query_block(sub, c):
        _moba_query_block(sub, *refs, nb=nb, heads=heads)
        return c
    lax.fori_loop(0, MOBA_QBLOCKS_PER_STEP, query_block, 0)


def _moba_query_block(sub, rb_ref, qT_ref, k_ref, vT_ref, g_ref, bias_ref, o_ref, kmean_ref, sel_ref, sa_ref,
                      sb_ref, ca_ref, cb_ref, acc_ref, m_ref, alpha_ref, p_ref, *, nb, heads):
    blk_len = MOBA_BLOCK
    dh = ATT_HEAD_DIM
    own = pl.program_id(2) * MOBA_QBLOCKS_PER_STEP + sub
    q_rows = pl.ds(pl.multiple_of(sub * blk_len, blk_len), blk_len)
    qT_ref = qT_ref.at[sub]
    g_ref = g_ref.at[q_rows]
    o_ref = o_ref.at[q_rows]

    @pl.when(own == 0)
    def _():
        def mean_body(j, c):
            kb = k_ref[pl.ds(pl.multiple_of(j * blk_len, blk_len), blk_len), :].astype(f32)
            kmean_ref[pl.ds(j, 1), :] = jnp.sum(kb, axis=0, keepdims=True) * (1.0 / blk_len)
            return c
        lax.fori_loop(0, nb, mean_body, 0)

    hsl = [slice(hh * dh, (hh + 1) * dh) for hh in range(heads)]

    def score_matmuls(i, buf, with_tile):
        dst_ref, cmax_ref = buf
        ic = jnp.minimum(i, own)
        rows = pl.ds(pl.multiple_of((own - ic) * blk_len, blk_len), blk_len)
        for hh in range(heads):
            s = _dot(k_ref[rows, hsl[hh]], qT_ref[hsl[hh], :])
            if with_tile:
                s = s + bias_ref[hh, jnp.minimum(ic, NEAR_TILES)]
            dst_ref[hh] = s
            cmax_ref[hh] = jnp.max(s, axis=0, keepdims=True)

    def select_blocks():
        blk = lax.broadcasted_iota(jnp.int32, (nb, blk_len), 0)
        past = blk < own
        for hh in range(heads):
            qT = qT_ref[hsl[hh], :]
            km = kmean_ref[:, hsl[hh]]
            km_hi = km.astype(bf16)
            km_lo = (km - km_hi.astype(f32)).astype(bf16)
            gate = _dot(km_hi, qT) + _dot(km_lo, qT)
            g = jnp.where(past, gate, -jnp.inf)
            sel = blk == own
            for _ in range(MOBA_TOPK):
                mx = jnp.max(g, axis=0, keepdims=True)
                first = jnp.min(jnp.where(g == mx, blk, nb), axis=0, keepdims=True)
                pick = blk == first
                sel = sel | (pick & past)
                g = jnp.where(pick, -jnp.inf, g)
            sel_ref[hh, 0:nb, :] = jnp.where(sel, 0.0, NEG_BIG)
            sel_ref[hh, nb:nb + SEL_PAD, :] = jnp.full((SEL_PAD, blk_len), NEG_BIG, f32)

    ones_rows = jnp.ones((SUM_ROWS, blk_len), bf16)
    m_ref[...] = jnp.full(m_ref.shape, NEG_BIG, f32)
    acc_ref[...] = jnp.zeros(acc_ref.shape, f32)

    head0 = pl.program_id(0) * heads
    far_bias = [rb_ref[REL_BUCKETS - 1, head0 + hh] * LOG2E for hh in range(heads)]

    def softmax(i, buf):
        src_ref, cmax_ref = buf
        mask_row = jnp.where(i <= own, own - i, nb)
        stats = []
        for hh in range(heads):
            m = m_ref[hh]
            shift = jnp.where(i >= FIRST_BARE_STEP, far_bias[hh], 0.0)
            selrow = sel_ref[hh, pl.ds(mask_row, 1), :]
            m_new = jnp.maximum(m, cmax_ref[hh] + (selrow + shift))
            p = jnp.exp2(src_ref[hh] - jnp.where(selrow < 0.0, -NEG_BIG, m_new - shift)).astype(bf16)
            m_ref[hh] = m_new
            stats.append((jnp.exp2(m - m_new), p))
        return stats

    def value_update(i, stats):
        block = jnp.maximum(own - i, 0)
        for hh in range(heads):
            alpha, p = stats[hh]
            lhs = jnp.concatenate([vT_ref[block, hsl[hh], :], ones_rows], axis=0)
            acc_ref[hh] = alpha * acc_ref[hh] + _dot(lhs, p)

    def keep_pending(stats):
        for hh in range(heads):
            alpha_ref[hh], p_ref[hh] = stats[hh]

    even_buf, odd_buf = (sa_ref, ca_ref), (sb_ref, cb_ref)
    score_buf = (even_buf, odd_buf)

    def trip(i, n, with_tile):
        value_update(i, [(alpha_ref[hh], p_ref[hh]) for hh in range(heads)])
        score_matmuls(i + 2, even_buf, with_tile(i + 2))
        for t in range(1, n):
            stats = softmax(i + t, score_buf[t % 2])
            if n > 2:
                score_matmuls(i + 2 + t, score_buf[t % 2], with_tile(i + 2 + t))
                value_update(i + t, stats)
            else:
                value_update(i + t, stats)
                score_matmuls(i + 2 + t, score_buf[t % 2], with_tile(i + 2 + t))
        keep_pending(softmax(i + n, even_buf))

    select_blocks()
    score_matmuls(0, even_buf, True)
    score_matmuls(1, odd_buf, True)
    keep_pending(softmax(0, even_buf))

    n_steps = own + 1
    n_bare = jnp.maximum(n_steps - TILED_STEPS, 0)
    n_long = (n_bare + LONG_TRIP_SLACK) // LONG_TRIP
    n_left = jnp.maximum(n_bare - LONG_TRIP * n_long, 0)
    n_mid = n_left // MID_TRIP
    n_short = (n_left - MID_TRIP * n_mid + 1) // 2

    whole_tiled = n_steps >= TILED_STEPS - TILED_TRIP_SLACK

    def tiled_long_body(t, c):
        trip(0, TILED_STEPS, lambda step: step < FIRST_BARE_STEP)
        return c

    def tiled_short_body(t, c):
        trip(2 * t, 2, lambda step: True)
        return c

    def long_body(t, c):
        trip(TILED_STEPS + LONG_TRIP * t, LONG_TRIP, lambda step: False)
        return c

    def mid_body(t, c):
        trip(TILED_STEPS + LONG_TRIP * n_long + MID_TRIP * t, MID_TRIP, lambda step: False)
        return c

    def short_body(t, c):
        trip(TILED_STEPS + LONG_TRIP * n_long + MID_TRIP * n_mid + 2 * t, 2, lambda step: False)
        return c

    lax.fori_loop(0, jnp.where(whole_tiled, 1, 0), tiled_long_body, 0)
    lax.fori_loop(0, jnp.where(whole_tiled, 0, (n_steps + 1) // 2), tiled_short_body, 0)
    lax.fori_loop(0, n_long, long_body, 0)
    lax.fori_loop(0, n_mid, mid_body, 0)
    lax.fori_loop(0, n_short, short_body, 0)
    for hh in range(heads):
        acc = acc_ref[hh]
        o = (acc[0:dh, :] * (1.0 / acc[dh:dh + 1, :])).T
        gg = g_ref[:, hsl[hh]].astype(f32)
        o_ref[:, hsl[hh]] = (o * (gg * jax.nn.sigmoid(gg))).astype(o_ref.dtype)


def _moba(rel_bias, qT, vT, pr, bias_tiles, batch, seq):
    T = pr.shape[0]
    nb = seq // MOBA_BLOCK
    G = MOBA_HEADS_PER_STEP
    gw = G * ATT_HEAD_DIM
    gate_col0 = ATT_WIDTH // gw
    qb = MOBA_QBLOCKS_PER_STEP
    assert nb % qb == 0
    steps = nb // qb
    vT4 = vT.reshape(batch, nb, ATT_WIDTH, MOBA_BLOCK)
    return pl.pallas_call(
        functools.partial(_moba_kernel, nb=nb, heads=G),
        grid=(ATT_HEADS // G, batch, steps),
        in_specs=[
            pl.BlockSpec(memory_space=pltpu.SMEM),
            pl.BlockSpec((qb, gw, MOBA_BLOCK), lambda h, b, i: (b * steps + i, h, 0)),
            pl.BlockSpec((seq, gw), lambda h, b, i: (b, h)),
            pl.BlockSpec((None, nb, gw, MOBA_BLOCK), lambda h, b, i: (b, 0, h, 0)),
            pl.BlockSpec((qb * MOBA_BLOCK, gw), lambda h, b, i: (b * steps + i, gate_col0 + h)),
            pl.BlockSpec((G, BIAS_TILES, MOBA_BLOCK, MOBA_BLOCK), lambda h, b, i: (h, 0, 0, 0),
                         pipeline_mode=pl.Buffered(1)),
        ],
        out_specs=pl.BlockSpec((qb * MOBA_BLOCK, gw), lambda h, b, i: (b * steps + i, h)),
        out_shape=jax.ShapeDtypeStruct((T, ATT_WIDTH), bf16),
        scratch_shapes=[pltpu.VMEM((nb, gw), f32),
                        pltpu.VMEM((G, nb + SEL_PAD, MOBA_BLOCK), f32),
                        pltpu.VMEM((G, MOBA_BLOCK, MOBA_BLOCK), f32),
                        pltpu.VMEM((G, MOBA_BLOCK, MOBA_BLOCK), f32),
                        pltpu.VMEM((G, 1, MOBA_BLOCK), f32),
                        pltpu.VMEM((G, 1, MOBA_BLOCK), f32),
                        pltpu.VMEM((G, ATT_HEAD_DIM + SUM_ROWS, MOBA_BLOCK), f32),
                        pltpu.VMEM((G, 1, MOBA_BLOCK), f32),
                        pltpu.VMEM((G, 1, MOBA_BLOCK), f32),
                        pltpu.VMEM((G, MOBA_BLOCK, MOBA_BLOCK), bf16)],
        compiler_params=pltpu.CompilerParams(
            dimension_semantics=("arbitrary", "arbitrary", "arbitrary"),
            vmem_limit_bytes=VMEM_LIMIT),
        name="moba",
    )(rel_bias, qT, pr, vT4, pr, bias_tiles)


def _outproj_kernel(ya_ref, ym_ref, wa_ref, wm_ref, g_ref, x_ref, o_ref):
    y = _dot(ya_ref[...], wa_ref[...]) + _dot(ym_ref[...], wm_ref[...])
    ms = jnp.mean(y * y, axis=-1, keepdims=True)
    o_ref[...] = x_ref[...] + y * lax.rsqrt(ms + RMS_EPS) * g_ref[...]


def _outproj(ya, ym, wa, wm, g_post, x2):
    T, D = x2.shape
    tm = OUT_TOKEN_TILE
    assert T % tm == 0
    resident = dict(pipeline_mode=pl.Buffered(1))
    return pl.pallas_call(
        _outproj_kernel,
        grid=(T // tm,),
        in_specs=[
            pl.BlockSpec((tm, ya.shape[1]), lambda i: (i, 0)),
            pl.BlockSpec((tm, ym.shape[1]), lambda i: (i, 0)),
            pl.BlockSpec(wa.shape, lambda i: (0, 0), **resident),
            pl.BlockSpec(wm.shape, lambda i: (0, 0), **resident),
            pl.BlockSpec((1, D), lambda i: (0, 0)),
            pl.BlockSpec((tm, D), lambda i: (i, 0)),
        ],
        out_specs=pl.BlockSpec((tm, D), lambda i: (i, 0)),
        out_shape=jax.ShapeDtypeStruct((T, D), f32),
        compiler_params=pltpu.CompilerParams(
            dimension_semantics=("arbitrary",), vmem_limit_bytes=VMEM_LIMIT),
        name="outproj",
    )(ya, ym, wa, wm, g_post, x2)


def _block_diag_256(w):
    width = w.shape[0] * QKV_BLOCK
    rows = w.reshape(width, QKV_BLOCK)
    col = np.arange(256)
    spread = jnp.asarray(col[None, :] % QKV_BLOCK == np.arange(QKV_BLOCK)[:, None], w.dtype)
    tiled = jnp.dot(rows, spread, precision=lax.Precision.HIGHEST)
    same_block = (np.arange(width)[:, None] % 256) // QKV_BLOCK == col[None, :] // QKV_BLOCK
    dense = jnp.where(jnp.asarray(same_block), tiled, 0.0)
    return dense.reshape(-1, 256, 256)


def _gate_lanes(a):
    nh = MLSTM_HEADS
    zeros = lambda n: jnp.zeros(a.shape[:-1] + (n,), a.dtype)
    return jnp.concatenate([a[..., 0:nh], zeros(GATE_ROWS - nh), a[..., nh:2 * nh],
                            zeros(LANES - GATE_ROWS - nh)], axis=-1)


def _layer(x, rel_bias, g_pre, g_post, w_in, conv_w, conv_b, wq_m, wk_m, wv_m,
           w_if, b_if, mh_norm, skip, w_out):
    batch, seq, d_model = x.shape
    assert seq % TOKEN_TILE == 0 and TOKEN_TILE % MLSTM_CHUNK == 0 and TOKEN_TILE % MOBA_BLOCK == 0
    aw, nh = ATT_WIDTH, MLSTM_HEADS
    x2 = x.reshape(batch * seq, d_model)

    w_bf = w_in.astype(bf16)
    wq = w_bf[:, 0:aw]
    wv = w_bf[:, 2 * aw:3 * aw]
    mw = MLSTM_WIDTH
    wr = jnp.concatenate([w_bf[:, aw:2 * aw], w_bf[:, 3 * aw:4 * aw], w_bf[:, 4 * aw + mw:]], axis=1)
    wx = w_bf[:, 4 * aw:4 * aw + mw]
    wq_bd = _block_diag_256(wq_m).astype(bf16)
    wk_bd = _block_diag_256(wk_m).astype(bf16)
    wv_bd = _block_diag_256(wv_m).astype(bf16)
    wg = _gate_lanes(w_if).astype(bf16)
    bg = _gate_lanes(b_if).reshape(1, LANES)
    w_out_bf = w_out.astype(bf16)

    bias_tiles = _bias_tiles(rel_bias)
    qT, vT, pr, xc, qmT, km, vmT, ga, gb, gl = _inproj(
        x2, g_pre.reshape(1, -1), wq, wv, wr, wx, conv_w, conv_b.reshape(1, -1),
        wq_bd, wk_bd, wv_bd, wg, bg, seq)
    ym = _mlstm(qmT, km, vmT, xc, pr, ga, gb, gl, mh_norm.reshape(1, -1), skip.reshape(1, -1),
                batch, seq)
    ya = _moba(rel_bias, qT, vT, pr, bias_tiles, batch, seq)
    out = _outproj(ya, ym, w_out_bf[0:aw], w_out_bf[aw:], g_post.reshape(1, -1), x2)
    return out.reshape(batch, seq, d_model)


def kernel(x, rel_bias, g_pre, g_post, w_in, conv_w, conv_b, wq_m, wk_m, wv_m, w_if, b_if,
           mh_norm, skip, w_out):
    depth = w_in.shape[0]
    for l in range(depth):
        x = _layer(x, rel_bias, g_pre[l], g_post[l], w_in[l], conv_w[l], conv_b[l], wq_m[l],
                   wk_m[l], wv_m[l], w_if[l], b_if[l], mh_norm[l], skip[l], w_out[l])
    return x
```

```python
import functools
import math

import jax
import jax.numpy as jnp
import numpy as np
from jax import lax
from jax.experimental import pallas as pl
from jax.experimental.pallas import tpu as pltpu

f32 = jnp.float32
bf16 = jnp.bfloat16

ATT_HEADS = 8
ATT_HEAD_DIM = 128
ATT_WIDTH = ATT_HEADS * ATT_HEAD_DIM
MOBA_BLOCK = 256
MOBA_TOPK = 3
REL_BUCKETS = 32
REL_MAX_DIST = 2048
MLSTM_HEADS = 4
MLSTM_WIDTH = 1024
MLSTM_HEAD_DIM = MLSTM_WIDTH // MLSTM_HEADS
QKV_BLOCK = 4
CONV_WIDTH = 4
RMS_EPS = 1e-6
LN_EPS = 1e-5

MLSTM_CHUNK = 256
TOKEN_TILE = 512
OUT_TOKEN_TILE = 1024
MOBA_HEADS_PER_STEP = 4
MOBA_QBLOCKS_PER_STEP = 2
MLSTM_CHUNKS_PER_STEP = 4
LONG_TRIP = 8
MID_TRIP = 4
LONG_TRIP_SLACK = 1
TILED_TRIP_SLACK = 3
HALO_ROWS = 16
SUM_ROWS = 16
GATE_ROWS = 8
SEL_PAD = 8
LANES = 128
NEG_BIG = -1e30
VMEM_LIMIT = 56 * 1024 * 1024
LOG2E = math.log2(math.e)


def _t5_thresholds():
    n = np.arange(0, 2 * REL_MAX_DIST, dtype=np.int64)
    max_exact = REL_BUCKETS // 2
    nf = np.maximum(n, 1).astype(np.float32)
    large = max_exact + (np.log(nf / np.float32(max_exact))
                         / np.float32(math.log(REL_MAX_DIST / max_exact))
                         * np.float32(REL_BUCKETS - max_exact)).astype(np.int32)
    large = np.minimum(large, REL_BUCKETS - 1)
    bucket = np.where(n < max_exact, n, large)
    assert np.all(np.diff(bucket) >= 0)
    return [int(np.argmax(bucket >= k)) for k in range(1, REL_BUCKETS)]


T5_THR = _t5_thresholds()
NEAR_TILES = -(-(T5_THR[-1] + MOBA_BLOCK - 1) // MOBA_BLOCK)
assert NEAR_TILES * MOBA_BLOCK - (MOBA_BLOCK - 1) >= T5_THR[-1]
BIAS_TILES = NEAR_TILES
TILED_STEPS = 2 * (-(-NEAR_TILES // 2))
FIRST_BARE_STEP = NEAR_TILES
assert 2 <= FIRST_BARE_STEP <= TILED_STEPS


def _dot(a, b):
    return jnp.dot(a, b, preferred_element_type=f32)


def _split3(x):
    hi = x.astype(bf16)
    r = x - hi.astype(f32)
    mid = r.astype(bf16)
    lo = (r - mid.astype(f32)).astype(bf16)
    return hi, mid, lo


def _bias_kernel(rb_ref, out_ref):
    h = pl.program_id(0)
    key = lax.broadcasted_iota(jnp.int32, (MOBA_BLOCK, MOBA_BLOCK), 0)
    qry = lax.broadcasted_iota(jnp.int32, (MOBA_BLOCK, MOBA_BLOCK), 1)
    base = qry - key
    for d in range(BIAS_TILES):
        dist = base + d * MOBA_BLOCK
        n = jnp.maximum(dist, 0)
        val = jnp.full((MOBA_BLOCK, MOBA_BLOCK), rb_ref[REL_BUCKETS - 1, h] * LOG2E, f32)
        for k in range(REL_BUCKETS - 2, -1, -1):
            val = jnp.where(n < T5_THR[k], rb_ref[k, h] * LOG2E, val)
        if d == 0:
            val = jnp.where(dist >= 0, val, NEG_BIG)
        out_ref[d] = val


def _bias_tiles(rel_bias):
    return pl.pallas_call(
        _bias_kernel,
        grid=(ATT_HEADS,),
        in_specs=[pl.BlockSpec(memory_space=pltpu.SMEM)],
        out_specs=pl.BlockSpec((None, BIAS_TILES, MOBA_BLOCK, MOBA_BLOCK), lambda h: (h, 0, 0, 0)),
        out_shape=jax.ShapeDtypeStruct((ATT_HEADS, BIAS_TILES, MOBA_BLOCK, MOBA_BLOCK), f32),
        name="bias_tiles",
    )(rel_bias)


def _log_sigmoid(v):
    return jnp.minimum(v, 0.0) - jnp.log1p(jnp.exp(-jnp.abs(v)))


def _inproj_kernel(x_ref, g_ref, wq_ref, wv_ref, wr_ref, wx_ref, cw_ref, cb_ref, wmq_ref, wmk_ref,
                   wmv_ref, wg_ref, bg_ref,
                   qT_ref, vT_ref, pr_ref, xc_ref, mqT_ref, mk_ref, mvT_ref, ga_ref, gb_ref, gl_ref,
                   xpad_ref, *, tm, seq, q_scale, k_scale):
    i = pl.program_id(0)
    W = MLSTM_WIDTH
    L = MLSTM_CHUNK
    x = x_ref[...]
    ms = jnp.mean(x * x, axis=-1, keepdims=True)
    h = (x * lax.rsqrt(ms + RMS_EPS) * g_ref[...]).astype(bf16)

    @pl.when((i * tm) % seq == 0)
    def _():
        xpad_ref[0:HALO_ROWS, :] = jnp.zeros((HALO_ROWS, W), f32)

    for cc in range(W // 512):
        cols = slice(cc * 512, (cc + 1) * 512)
        xpad_ref[HALO_ROWS:HALO_ROWS + tm, cols] = _dot(h, wx_ref[:, cols])

    def att_qv(cc):
        rows = slice(cc * 256, (cc + 1) * 256)
        qt = (_dot(h, wq_ref[:, rows]) * q_scale).T
        vt = _dot(h, wv_ref[:, rows]).T
        for u in range(tm // MOBA_BLOCK):
            cols = slice(u * MOBA_BLOCK, (u + 1) * MOBA_BLOCK)
            qT_ref[u, rows, :] = qt[:, cols].astype(bf16)
            vT_ref[u, rows, :] = vt[:, cols].astype(bf16)

    def token_major(cc):
        cols = slice(cc * 512, (cc + 1) * 512)
        pr_ref[:, cols] = _dot(h, wr_ref[:, cols]).astype(bf16)

    def conv_silu(g):
        sl = slice(g * 256, (g + 1) * 256)
        acc = cb_ref[:, sl] + cw_ref[CONV_WIDTH - 1:CONV_WIDTH, sl] * xpad_ref[HALO_ROWS:HALO_ROWS + tm, sl]
        for j in range(CONV_WIDTH - 1):
            off = HALO_ROWS - (CONV_WIDTH - 1) + j
            acc = acc + cw_ref[j:j + 1, sl] * xpad_ref[off:off + tm, sl]
        xc_ref[:, sl] = (acc * jax.nn.sigmoid(acc)).astype(bf16)
        xpad_ref[0:HALO_ROWS, sl] = xpad_ref[tm:tm + HALO_ROWS, sl]

    k_unscale = 1.0 / k_scale
    gates = [jnp.zeros((tm, LANES), f32)]

    def block_diag_qkv(g):
        sl = slice(g * 256, (g + 1) * 256)
        xc_g = xc_ref[:, sl]
        xm_g = xpad_ref[HALO_ROWS:HALO_ROWS + tm, sl].astype(bf16)
        q = _dot(xc_g, wmq_ref[g])
        v = _dot(xm_g, wmv_ref[g])
        q_bf = q.astype(bf16)
        v_bf = v.astype(bf16)
        kk = (_dot(xc_g, wmk_ref[g]) * k_scale).astype(bf16)
        mk_ref[:, sl] = kk
        qT = q.T.astype(bf16)
        vT = v.T.astype(bf16)
        for u in range(tm // L):
            cols = slice(u * L, (u + 1) * L)
            mqT_ref[u, sl, :] = qT[:, cols]
            mvT_ref[u, sl, :] = vT[:, cols]
        ks = slice(W + g * 256, W + (g + 1) * 256)
        vs = slice(2 * W + g * 256, 2 * W + (g + 1) * 256)
        gates[0] = (gates[0] + _dot(q_bf, wg_ref[sl, :]) + k_unscale * _dot(kk, wg_ref[ks, :])
                    + _dot(v_bf, wg_ref[vs, :]))

    def gate_outputs():
        gates_t = (gates[0] + bg_ref[...]).T
        li = gates_t[0:GATE_ROWS, :] * LOG2E
        lf = _log_sigmoid(gates_t[GATE_ROWS:2 * GATE_ROWS, :]) * LOG2E
        row = lax.broadcasted_iota(jnp.int32, (L, L), 0)
        col = lax.broadcasted_iota(jnp.int32, (L, L), 1)
        upper = jnp.where(row <= col, 1.0, 0.0).astype(bf16)
        gl_ref[...] = li
        for u in range(tm // L):
            cols = slice(u * L, (u + 1) * L)
            b = sum(_dot(part, upper) for part in _split3(lf[:, cols]))
            gb_ref[:, cols] = b
            a_pad = jnp.concatenate([li[:, cols] - b, jnp.zeros((LANES - GATE_ROWS, L), f32)], axis=0)
            ga_ref[cols, :] = a_pad.T[:, 0:GATE_ROWS]

    big = ([functools.partial(token_major, cc) for cc in range(pr_ref.shape[1] // 512)]
           + [functools.partial(att_qv, cc) for cc in range(ATT_WIDTH // 256)])
    conv = [functools.partial(conv_silu, g) for g in range(W // 256)]
    proj = [functools.partial(block_diag_qkv, g) for g in range(W // 256)]
    prep = conv + proj + [gate_outputs]
    assert len(big) >= len(prep)
    for n, task in enumerate(big):
        task()
        if n < len(prep):
            prep[n]()


def _inproj(x2, g_pre, wq, wv, wr, wx, conv_w, conv_b, wmq, wmk, wmv, wg, bg, seq):
    T, D = x2.shape
    W = MLSTM_WIDTH
    tm = TOKEN_TILE
    L = MLSTM_CHUNK
    nblk = tm // MOBA_BLOCK
    ncols = wr.shape[1]
    resident = dict(pipeline_mode=pl.Buffered(1))
    const2 = lambda i: (0, 0)
    const3 = lambda i: (0, 0, 0)
    tok = pl.BlockSpec((tm, W), lambda i: (i, 0))
    att_T = pl.BlockSpec((nblk, ATT_WIDTH, MOBA_BLOCK), lambda i: (i, 0, 0))
    chunk_T = pl.BlockSpec((tm // L, W, L), lambda i: (i, 0, 0))
    rows8 = pl.BlockSpec((GATE_ROWS, tm), lambda i: (0, i))
    return pl.pallas_call(
        functools.partial(_inproj_kernel, tm=tm, seq=seq, q_scale=ATT_HEAD_DIM ** -0.5 * LOG2E,
                          k_scale=MLSTM_HEAD_DIM ** -0.5),
        grid=(T // tm,),
        in_specs=[
            pl.BlockSpec((tm, D), lambda i: (i, 0)),
            pl.BlockSpec((1, D), const2),
            pl.BlockSpec((D, ATT_WIDTH), const2, **resident),
            pl.BlockSpec((D, ATT_WIDTH), const2, **resident),
            pl.BlockSpec((D, ncols), const2, **resident),
            pl.BlockSpec((D, W), const2, **resident),
            pl.BlockSpec((CONV_WIDTH, W), const2),
            pl.BlockSpec((1, W), const2),
            pl.BlockSpec((W // 256, 256, 256), const3, **resident),
            pl.BlockSpec((W // 256, 256, 256), const3, **resident),
            pl.BlockSpec((W // 256, 256, 256), const3, **resident),
            pl.BlockSpec((3 * W, LANES), const2, **resident),
            pl.BlockSpec((1, LANES), const2),
        ],
        out_specs=[att_T, att_T, pl.BlockSpec((tm, ncols), lambda i: (i, 0)),
                   tok, chunk_T, tok, chunk_T,
                   pl.BlockSpec((tm, GATE_ROWS), lambda i: (i, 0)), rows8, rows8],
        out_shape=[
            jax.ShapeDtypeStruct((T // MOBA_BLOCK, ATT_WIDTH, MOBA_BLOCK), bf16),
            jax.ShapeDtypeStruct((T // MOBA_BLOCK, ATT_WIDTH, MOBA_BLOCK), bf16),
            jax.ShapeDtypeStruct((T, ncols), bf16),
            jax.ShapeDtypeStruct((T, W), bf16),
            jax.ShapeDtypeStruct((T // L, W, L), bf16),
            jax.ShapeDtypeStruct((T, W), bf16),
            jax.ShapeDtypeStruct((T // L, W, L), bf16),
            jax.ShapeDtypeStruct((T, GATE_ROWS), f32),
            jax.ShapeDtypeStruct((GATE_ROWS, T), f32),
            jax.ShapeDtypeStruct((GATE_ROWS, T), f32),
        ],
        scratch_shapes=[pltpu.VMEM((HALO_ROWS + tm, W), f32)],
        compiler_params=pltpu.CompilerParams(
            dimension_semantics=("arbitrary",), vmem_limit_bytes=VMEM_LIMIT),
        name="inproj",
    )(x2, g_pre, wq, wv, wr, wx, conv_w, conv_b, wmq, wmk, wmv, wg, bg)


def _mlstm_kernel(qT_ref, k_ref, vT_ref, xc_ref, z_ref, ga_ref, gb_ref, gl_ref, nw_ref, sk_ref,
                  y_ref, ct_ref, m_ref):
    L = MLSTM_CHUNK
    dh = MLSTM_HEAD_DIM

    @pl.when(pl.program_id(1) == 0)
    def _():
        ct_ref[...] = jnp.zeros_like(ct_ref)
        m_ref[...] = jnp.zeros_like(m_ref)

    s_idx = lax.broadcasted_iota(jnp.int32, (L, L), 0)
    t_idx = lax.broadcasted_iota(jnp.int32, (L, L), 1)
    causal = s_idx <= t_idx
    ones_rows = jnp.ones((SUM_ROWS, L), bf16)
    for u, h in ((u, h) for u in range(MLSTM_CHUNKS_PER_STEP) for h in range(MLSTM_HEADS)):
        sl = slice(h * dh, (h + 1) * dh)
        tok = slice(u * L, (u + 1) * L)
        qT = qT_ref[u, sl, :]
        k = k_ref[tok, sl]
        vT_ext = jnp.concatenate([vT_ref[u, sl, :], ones_rows], axis=0)
        a_c = ga_ref[tok, h:h + 1]
        b_r = gb_ref[h:h + 1, tok]
        li_r = gl_ref[h:h + 1, tok]
        b_last = b_r[:, L - 1:L]
        m_prev = m_ref[h][0:1, 0:1]
        ct = ct_ref[h]

        log_d = jnp.where(causal, a_c + b_r, NEG_BIG)
        inter = b_r + m_prev
        m_t = jnp.maximum(inter, jnp.max(log_d, axis=0, keepdims=True))
        sT = _dot(k, qT) * jnp.exp2(log_d - m_t)
        dec = jnp.exp2(inter - m_t)
        num = _dot(vT_ext, sT.astype(bf16)) + dec * _dot(ct.astype(bf16), qT)
        den = num[dh:dh + 1, :]
        hT = num[0:dh, :] * (1.0 / jnp.maximum(jnp.abs(den), jnp.exp2(-m_t)))

        mu = jnp.mean(hT, axis=0, keepdims=True)
        cen = hT - mu
        var = jnp.mean(cen * cen, axis=0, keepdims=True)
        yn = (cen * lax.rsqrt(var + LN_EPS)).T
        z = z_ref[tok, sl].astype(f32)
        out = (yn * nw_ref[:, sl] + sk_ref[:, sl] * xc_ref[tok, sl].astype(f32)) * (z * jax.nn.sigmoid(z))
        y_ref[tok, sl] = out.astype(y_ref.dtype)

        log_w = b_last - b_r + li_r
        m_new = jnp.maximum(b_last + m_prev, jnp.max(log_w, axis=1, keepdims=True))
        vw = vT_ext * jnp.exp2(log_w - m_new).astype(bf16)
        ct_ref[h] = jnp.exp2(b_last + m_prev - m_new) * ct + _dot(vw, k)
        m_ref[h] = jnp.broadcast_to(m_new, m_ref.shape[1:])


def _mlstm(qT, k, vT, xc, pr, ga, gb, gl, mh_norm, skip, batch, seq):
    T, W = k.shape
    cps = MLSTM_CHUNKS_PER_STEP
    rows = cps * MLSTM_CHUNK
    assert seq % rows == 0
    nc = seq // rows
    z_col = 2
    tok = pl.BlockSpec((rows, W), lambda b, c: (b * nc + c, 0))
    chunkT = pl.BlockSpec((cps, W, MLSTM_CHUNK), lambda b, c: (b * nc + c, 0, 0))
    rows8 = pl.BlockSpec((GATE_ROWS, rows), lambda b, c: (0, b * nc + c))
    return pl.pallas_call(
        _mlstm_kernel,
        grid=(batch, nc),
        in_specs=[chunkT, tok, chunkT, tok,
                  pl.BlockSpec((rows, W), lambda b, c: (b * nc + c, z_col)),
                  pl.BlockSpec((rows, GATE_ROWS), lambda b, c: (b * nc + c, 0)),
                  rows8, rows8,
                  pl.BlockSpec((1, W), lambda b, c: (0, 0)),
                  pl.BlockSpec((1, W), lambda b, c: (0, 0))],
        out_specs=tok,
        out_shape=jax.ShapeDtypeStruct((T, W), bf16),
        scratch_shapes=[pltpu.VMEM((MLSTM_HEADS, MLSTM_HEAD_DIM + SUM_ROWS, MLSTM_HEAD_DIM), f32),
                        pltpu.VMEM((MLSTM_HEADS, 8, LANES), f32)],
        compiler_params=pltpu.CompilerParams(
            dimension_semantics=("arbitrary", "arbitrary"), vmem_limit_bytes=VMEM_LIMIT),
        name="mlstm",
    )(qT, k, vT, xc, pr, ga, gb, gl, mh_norm, skip)


def _moba_kernel(*refs, nb, heads):
    def query_block(sub, c):
        _moba_query_block(sub, *refs, nb=nb, heads=heads)
        return c
    lax.fori_loop(0, MOBA_QBLOCKS_PER_STEP, query_block, 0)


def _moba_query_block(sub, rb_ref, qT_ref, k_ref, vT_ref, g_ref, bias_ref, o_ref, kmean_ref, sel_ref, sa_ref,
                      sb_ref, ca_ref, cb_ref, acc_ref, m_ref, alpha_ref, p_ref, *, nb, heads):
    blk_len = MOBA_BLOCK
    dh = ATT_HEAD_DIM
    own = pl.program_id(2) * MOBA_QBLOCKS_PER_STEP + sub
    q_rows = pl.ds(pl.multiple_of(sub * blk_len, blk_len), blk_len)
    qT_ref = qT_ref.at[sub]
    g_ref = g_ref.at[q_rows]
    o_ref = o_ref.at[q_rows]

    @pl.when(own == 0)
    def _():
        def mean_body(j, c):
            kb = k_ref[pl.ds(pl.multiple_of(j * blk_len, blk_len), blk_len), :].astype(f32)
            kmean_ref[pl.ds(j, 1), :] = jnp.sum(kb, axis=0, keepdims=True) * (1.0 / blk_len)
            return c
        lax.fori_loop(0, nb, mean_body, 0)

    hsl = [slice(hh * dh, (hh + 1) * dh) for hh in range(heads)]

    def score_matmuls(i, buf, with_tile):
        dst_ref, cmax_ref = buf
        ic = jnp.minimum(i, own)
        rows = pl.ds(pl.multiple_of((own - ic) * blk_len, blk_len), blk_len)
        for hh in range(heads):
            s = _dot(k_ref[rows, hsl[hh]], qT_ref[hsl[hh], :])
            if with_tile:
                s = s + bias_ref[hh, jnp.minimum(ic, BIAS_TILES - 1)]
            dst_ref[hh] = s
            cmax_ref[hh] = jnp.max(s, axis=0, keepdims=True)

    def select_blocks():
        blk = lax.broadcasted_iota(jnp.int32, (nb, blk_len), 0)
        past = blk < own
        for hh in range(heads):
            qT = qT_ref[hsl[hh], :]
            km = kmean_ref[:, hsl[hh]]
            km_hi = km.astype(bf16)
            km_lo = (km - km_hi.astype(f32)).astype(bf16)
            gate = _dot(km_hi, qT) + _dot(km_lo, qT)
            g = jnp.where(past, gate, -jnp.inf)
            sel = blk == own
            for _ in range(MOBA_TOPK):
                mx = jnp.max(g, axis=0, keepdims=True)
                first = jnp.min(jnp.where(g == mx, blk, nb), axis=0, keepdims=True)
                pick = blk == first
                sel = sel | (pick & past)
                g = jnp.where(pick, -jnp.inf, g)
            sel_ref[hh, 0:nb, :] = jnp.where(sel, 0.0, NEG_BIG)
            sel_ref[hh, nb:nb + SEL_PAD, :] = jnp.full((SEL_PAD, blk_len), NEG_BIG, f32)

    ones_rows = jnp.ones((SUM_ROWS, blk_len), bf16)
    m_ref[...] = jnp.full(m_ref.shape, NEG_BIG, f32)
    acc_ref[...] = jnp.zeros(acc_ref.shape, f32)

    head0 = pl.program_id(0) * heads
    far_bias = [rb_ref[REL_BUCKETS - 1, head0 + hh] * LOG2E for hh in range(heads)]

    def softmax(i, buf):
        src_ref, cmax_ref = buf
        mask_row = jnp.where(i <= own, own - i, nb)
        stats = []
        for hh in range(heads):
            m = m_ref[hh]
            shift = jnp.where(i >= FIRST_BARE_STEP, far_bias[hh], 0.0)
            selrow = sel_ref[hh, pl.ds(mask_row, 1), :]
            m_new = jnp.maximum(m, cmax_ref[hh] + (selrow + shift))
            p = jnp.exp2(src_ref[hh] - jnp.where(selrow < 0.0, -NEG_BIG, m_new - shift)).astype(bf16)
            m_ref[hh] = m_new
            stats.append((jnp.exp2(m - m_new), p))
        return stats

    def value_update(i, stats):
        block = jnp.maximum(own - i, 0)
        for hh in range(heads):
            alpha, p = stats[hh]
            lhs = jnp.concatenate([vT_ref[block, hsl[hh], :], ones_rows], axis=0)
            acc_ref[hh] = alpha * acc_ref[hh] + _dot(lhs, p)

    def keep_pending(stats):
        for hh in range(heads):
            alpha_ref[hh], p_ref[hh] = stats[hh]

    even_buf, odd_buf = (sa_ref, ca_ref), (sb_ref, cb_ref)
    score_buf = (even_buf, odd_buf)

    def trip(i, n, with_tile):
        value_update(i, [(alpha_ref[hh], p_ref[hh]) for hh in range(heads)])
        score_matmuls(i + 2, even_buf, with_tile(i + 2))
        for t in range(1, n):
            stats = softmax(i + t, score_buf[t % 2])
            if n > 2:
                score_matmuls(i + 2 + t, score_buf[t % 2], with_tile(i + 2 + t))
                value_update(i + t, stats)
            else:
                value_update(i + t, stats)
                score_matmuls(i + 2 + t, score_buf[t % 2], with_tile(i + 2 + t))
        keep_pending(softmax(i + n, even_buf))

    select_blocks()
    score_matmuls(0, even_buf, True)
    score_matmuls(1, odd_buf, True)
    keep_pending(softmax(0, even_buf))

    n_steps = own + 1
    n_bare = jnp.maximum(n_steps - TILED_STEPS, 0)
    n_long = (n_bare + LONG_TRIP_SLACK) // LONG_TRIP
    n_left = jnp.maximum(n_bare - LONG_TRIP * n_long, 0)
    n_mid = n_left // MID_TRIP
    n_short = (n_left - MID_TRIP * n_mid + 1) // 2

    whole_tiled = n_steps >= TILED_STEPS - TILED_TRIP_SLACK

    def tiled_long_body(t, c):
        trip(0, TILED_STEPS, lambda step: step < FIRST_BARE_STEP)
        return c

    def tiled_short_body(t, c):
        trip(2 * t, 2, lambda step: True)
        return c

    def long_body(t, c):
        trip(TILED_STEPS + LONG_TRIP * t, LONG_TRIP, lambda step: False)
        return c

    def mid_body(t, c):
        trip(TILED_STEPS + LONG_TRIP * n_long + MID_TRIP * t, MID_TRIP, lambda step: False)
        return c

    def short_body(t, c):
        trip(TILED_STEPS + LONG_TRIP * n_long + MID_TRIP * n_mid + 2 * t, 2, lambda step: False)
        return c

    lax.fori_loop(0, jnp.where(whole_tiled, 1, 0), tiled_long_body, 0)
    lax.fori_loop(0, jnp.where(whole_tiled, 0, (n_steps + 1) // 2), tiled_short_body, 0)
    lax.fori_loop(0, n_long, long_body, 0)
    lax.fori_loop(0, n_mid, mid_body, 0)
    lax.fori_loop(0, n_short, short_body, 0)
    for hh in range(heads):
        acc = acc_ref[hh]
        o = (acc[0:dh, :] * (1.0 / acc[dh:dh + 1, :])).T
        gg = g_ref[:, hsl[hh]].astype(f32)
        o_ref[:, hsl[hh]] = (o * (gg * jax.nn.sigmoid(gg))).astype(o_ref.dtype)


def _moba(rel_bias, qT, vT, pr, bias_tiles, batch, seq):
    T = pr.shape[0]
    nb = seq // MOBA_BLOCK
    G = MOBA_HEADS_PER_STEP
    gw = G * ATT_HEAD_DIM
    gate_col0 = ATT_WIDTH // gw
    qb = MOBA_QBLOCKS_PER_STEP
    assert nb % qb == 0
    steps = nb // qb
    vT4 = vT.reshape(batch, nb, ATT_WIDTH, MOBA_BLOCK)
    return pl.pallas_call(
        functools.partial(_moba_kernel, nb=nb, heads=G),
        grid=(ATT_HEADS // G, batch, steps),
        in_specs=[
            pl.BlockSpec(memory_space=pltpu.SMEM),
            pl.BlockSpec((qb, gw, MOBA_BLOCK), lambda h, b, i: (b * steps + i, h, 0)),
            pl.BlockSpec((seq, gw), lambda h, b, i: (b, h)),
            pl.BlockSpec((None, nb, gw, MOBA_BLOCK), lambda h, b, i: (b, 0, h, 0)),
            pl.BlockSpec((qb * MOBA_BLOCK, gw), lambda h, b, i: (b * steps + i, gate_col0 + h)),
            pl.BlockSpec((G, BIAS_TILES, MOBA_BLOCK, MOBA_BLOCK), lambda h, b, i: (h, 0, 0, 0),
                         pipeline_mode=pl.Buffered(1)),
        ],
        out_specs=pl.BlockSpec((qb * MOBA_BLOCK, gw), lambda h, b, i: (b * steps + i, h)),
        out_shape=jax.ShapeDtypeStruct((T, ATT_WIDTH), bf16),
        scratch_shapes=[pltpu.VMEM((nb, gw), f32),
                        pltpu.VMEM((G, nb + SEL_PAD, MOBA_BLOCK), f32),
                        pltpu.VMEM((G, MOBA_BLOCK, MOBA_BLOCK), f32),
                        pltpu.VMEM((G, MOBA_BLOCK, MOBA_BLOCK), f32),
                        pltpu.VMEM((G, 1, MOBA_BLOCK), f32),
                        pltpu.VMEM((G, 1, MOBA_BLOCK), f32),
                        pltpu.VMEM((G, ATT_HEAD_DIM + SUM_ROWS, MOBA_BLOCK), f32),
                        pltpu.VMEM((G, 1, MOBA_BLOCK), f32),
                        pltpu.VMEM((G, 1, MOBA_BLOCK), f32),
                        pltpu.VMEM((G, MOBA_BLOCK, MOBA_BLOCK), bf16)],
        compiler_params=pltpu.CompilerParams(
            dimension_semantics=("arbitrary", "arbitrary", "arbitrary"),
            vmem_limit_bytes=VMEM_LIMIT),
        name="moba",
    )(rel_bias, qT, pr, vT4, pr, bias_tiles)


def _outproj_kernel(ya_ref, ym_ref, wa_ref, wm_ref, g_ref, x_ref, o_ref):
    y = _dot(ya_ref[...], wa_ref[...]) + _dot(ym_ref[...], wm_ref[...])
    ms = jnp.mean(y * y, axis=-1, keepdims=True)
    o_ref[...] = x_ref[...] + y * lax.rsqrt(ms + RMS_EPS) * g_ref[...]


def _outproj(ya, ym, wa, wm, g_post, x2):
    T, D = x2.shape
    tm = OUT_TOKEN_TILE
    assert T % tm == 0
    resident = dict(pipeline_mode=pl.Buffered(1))
    return pl.pallas_call(
        _outproj_kernel,
        grid=(T // tm,),
        in_specs=[
            pl.BlockSpec((tm, ya.shape[1]), lambda i: (i, 0)),
            pl.BlockSpec((tm, ym.shape[1]), lambda i: (i, 0)),
            pl.BlockSpec(wa.shape, lambda i: (0, 0), **resident),
            pl.BlockSpec(wm.shape, lambda i: (0, 0), **resident),
            pl.BlockSpec((1, D), lambda i: (0, 0)),
            pl.BlockSpec((tm, D), lambda i: (i, 0)),
        ],
        out_specs=pl.BlockSpec((tm, D), lambda i: (i, 0)),
        out_shape=jax.ShapeDtypeStruct((T, D), f32),
        compiler_params=pltpu.CompilerParams(
            dimension_semantics=("arbitrary",), vmem_limit_bytes=VMEM_LIMIT),
        name="outproj",
    )(ya, ym, wa, wm, g_post, x2)


def _block_diag_256(w):
    width = w.shape[0] * QKV_BLOCK
    rows = w.reshape(width, QKV_BLOCK)
    col = np.arange(256)
    spread = jnp.asarray(col[None, :] % QKV_BLOCK == np.arange(QKV_BLOCK)[:, None], w.dtype)
    tiled = jnp.dot(rows, spread, precision=lax.Precision.HIGHEST)
    same_block = (np.arange(width)[:, None] % 256) // QKV_BLOCK == col[None, :] // QKV_BLOCK
    dense = jnp.where(jnp.asarray(same_block), tiled, 0.0)
    return dense.reshape(-1, 256, 256)


def _gate_lanes(a):
    nh = MLSTM_HEADS
    zeros = lambda n: jnp.zeros(a.shape[:-1] + (n,), a.dtype)
    return jnp.concatenate([a[..., 0:nh], zeros(GATE_ROWS - nh), a[..., nh:2 * nh],
                            zeros(LANES - GATE_ROWS - nh)], axis=-1)


def _layer(x, rel_bias, g_pre, g_post, w_in, conv_w, conv_b, wq_m, wk_m, wv_m,
           w_if, b_if, mh_norm, skip, w_out):
    batch, seq, d_model = x.shape
    assert seq % TOKEN_TILE == 0 and TOKEN_TILE % MLSTM_CHUNK == 0 and TOKEN_TILE % MOBA_BLOCK == 0
    aw, nh = ATT_WIDTH, MLSTM_HEADS
    x2 = x.reshape(batch * seq, d_model)

    w_bf = w_in.astype(bf16)
    wq = w_bf[:, 0:aw]
    wv = w_bf[:, 2 * aw:3 * aw]
    mw = MLSTM_WIDTH
    wr = jnp.concatenate([w_bf[:, aw:2 * aw], w_bf[:, 3 * aw:4 * aw], w_bf[:, 4 * aw + mw:]], axis=1)
    wx = w_bf[:, 4 * aw:4 * aw + mw]
    wq_bd = _block_diag_256(wq_m).astype(bf16)
    wk_bd = _block_diag_256(wk_m).astype(bf16)
    wv_bd = _block_diag_256(wv_m).astype(bf16)
    wg = _gate_lanes(w_if).astype(bf16)
    bg = _gate_lanes(b_if).reshape(1, LANES)
    w_out_bf = w_out.astype(bf16)

    bias_tiles = _bias_tiles(rel_bias)
    qT, vT, pr, xc, qmT, km, vmT, ga, gb, gl = _inproj(
        x2, g_pre.reshape(1, -1), wq, wv, wr, wx, conv_w, conv_b.reshape(1, -1),
        wq_bd, wk_bd, wv_bd, wg, bg, seq)
    ym = _mlstm(qmT, km, vmT, xc, pr, ga, gb, gl, mh_norm.reshape(1, -1), skip.reshape(1, -1),
                batch, seq)
    ya = _moba(rel_bias, qT, vT, pr, bias_tiles, batch, seq)
    out = _outproj(ya, ym, w_out_bf[0:aw], w_out_bf[aw:], g_post.reshape(1, -1), x2)
    return out.reshape(batch, seq, d_model)


def kernel(x, rel_bias, g_pre, g_post, w_in, conv_w, conv_b, wq_m, wk_m, wv_m, w_if, b_if,
           mh_norm, skip, w_out):
    depth = w_in.shape[0]
    for l in range(depth):
        x = _layer(x, rel_bias, g_pre[l], g_post[l], w_in[l], conv_w[l], conv_b[l], wq_m[l],
                   wk_m[l], wv_m[l], w_if[l], b_if[l], mh_norm[l], skip[l], w_out[l])
    return x
```

```python
import functools
import math

import jax
import jax.numpy as jnp
import numpy as np
from jax import lax
from jax.experimental import pallas as pl
from jax.experimental.pallas import tpu as pltpu

f32 = jnp.float32
bf16 = jnp.bfloat16

ATT_HEADS = 8
ATT_HEAD_DIM = 128
ATT_WIDTH = ATT_HEADS * ATT_HEAD_DIM
MOBA_BLOCK = 256
MOBA_TOPK = 3
REL_BUCKETS = 32
REL_MAX_DIST = 2048
MLSTM_HEADS = 4
MLSTM_WIDTH = 1024
MLSTM_HEAD_DIM = MLSTM_WIDTH // MLSTM_HEADS
QKV_BLOCK = 4
CONV_WIDTH = 4
RMS_EPS = 1e-6
LN_EPS = 1e-5

MLSTM_CHUNK = 256
TOKEN_TILE = 512
OUT_TOKEN_TILE = 1024
MOBA_HEADS_PER_STEP = 4
MOBA_QBLOCKS_PER_STEP = 2
MLSTM_CHUNKS_PER_STEP = 4
LONG_TRIP = 8
MID_TRIP = 4
LONG_TRIP_SLACK = 1
TILED_TRIP_SLACK = 3
HALO_ROWS = 16
SUM_ROWS = 16
GATE_ROWS = 8
SEL_PAD = 8
LANES = 128
NEG_BIG = -1e30
VMEM_LIMIT = 56 * 1024 * 1024
LOG2E = math.log2(math.e)


def _t5_thresholds():
    n = np.arange(0, 2 * REL_MAX_DIST, dtype=np.int64)
    max_exact = REL_BUCKETS // 2
    nf = np.maximum(n, 1).astype(np.float32)
    large = max_exact + (np.log(nf / np.float32(max_exact))
                         / np.float32(math.log(REL_MAX_DIST / max_exact))
                         * np.float32(REL_BUCKETS - max_exact)).astype(np.int32)
    large = np.minimum(large, REL_BUCKETS - 1)
    bucket = np.where(n < max_exact, n, large)
    assert np.all(np.diff(bucket) >= 0)
    return [int(np.argmax(bucket >= k)) for k in range(1, REL_BUCKETS)]


T5_THR = _t5_thresholds()
NEAR_TILES = -(-(T5_THR[-1] + MOBA_BLOCK - 1) // MOBA_BLOCK)
assert NEAR_TILES * MOBA_BLOCK - (MOBA_BLOCK - 1) >= T5_THR[-1]
BIAS_TILES = NEAR_TILES
TILED_STEPS = 2 * (-(-NEAR_TILES // 2))
FIRST_BARE_STEP = NEAR_TILES
assert 2 <= FIRST_BARE_STEP <= TILED_STEPS


def _dot(a, b):
    return jnp.dot(a, b, preferred_element_type=f32)


def _split3(x):
    hi = x.astype(bf16)
    r = x - hi.astype(f32)
    mid = r.astype(bf16)
    lo = (r - mid.astype(f32)).astype(bf16)
    return hi, mid, lo


def _write_bias_tiles(rb_ref, h, out_ref):
    key = lax.broadcasted_iota(jnp.int32, (MOBA_BLOCK, MOBA_BLOCK), 0)
    qry = lax.broadcasted_iota(jnp.int32, (MOBA_BLOCK, MOBA_BLOCK), 1)
    base = qry - key
    for d in range(BIAS_TILES):
        dist = base + d * MOBA_BLOCK
        n = jnp.maximum(dist, 0)
        val = jnp.full((MOBA_BLOCK, MOBA_BLOCK), rb_ref[REL_BUCKETS - 1, h] * LOG2E, f32)
        for k in range(REL_BUCKETS - 2, -1, -1):
            val = jnp.where(n < T5_THR[k], rb_ref[k, h] * LOG2E, val)
        if d == 0:
            val = jnp.where(dist >= 0, val, NEG_BIG)
        out_ref[d] = val


def _log_sigmoid(v):
    return jnp.minimum(v, 0.0) - jnp.log1p(jnp.exp(-jnp.abs(v)))


def _inproj_kernel(x_ref, g_ref, wq_ref, wv_ref, wr_ref, wx_ref, cw_ref, cb_ref, wmq_ref, wmk_ref,
                   wmv_ref, wg_ref, bg_ref,
                   qT_ref, vT_ref, pr_ref, xc_ref, mqT_ref, mk_ref, mvT_ref, ga_ref, gb_ref, gl_ref,
                   xpad_ref, *, tm, seq, q_scale, k_scale):
    i = pl.program_id(0)
    W = MLSTM_WIDTH
    L = MLSTM_CHUNK
    x = x_ref[...]
    ms = jnp.mean(x * x, axis=-1, keepdims=True)
    h = (x * lax.rsqrt(ms + RMS_EPS) * g_ref[...]).astype(bf16)

    @pl.when((i * tm) % seq == 0)
    def _():
        xpad_ref[0:HALO_ROWS, :] = jnp.zeros((HALO_ROWS, W), f32)

    for cc in range(W // 512):
        cols = slice(cc * 512, (cc + 1) * 512)
        xpad_ref[HALO_ROWS:HALO_ROWS + tm, cols] = _dot(h, wx_ref[:, cols])

    def att_qv(cc):
        rows = slice(cc * 256, (cc + 1) * 256)
        qt = (_dot(h, wq_ref[:, rows]) * q_scale).T
        vt = _dot(h, wv_ref[:, rows]).T
        for u in range(tm // MOBA_BLOCK):
            cols = slice(u * MOBA_BLOCK, (u + 1) * MOBA_BLOCK)
            qT_ref[u, rows, :] = qt[:, cols].astype(bf16)
            vT_ref[u, rows, :] = vt[:, cols].astype(bf16)

    def token_major(cc):
        cols = slice(cc * 512, (cc + 1) * 512)
        pr_ref[:, cols] = _dot(h, wr_ref[:, cols]).astype(bf16)

    def conv_silu(g):
        sl = slice(g * 256, (g + 1) * 256)
        acc = cb_ref[:, sl] + cw_ref[CONV_WIDTH - 1:CONV_WIDTH, sl] * xpad_ref[HALO_ROWS:HALO_ROWS + tm, sl]
        for j in range(CONV_WIDTH - 1):
            off = HALO_ROWS - (CONV_WIDTH - 1) + j
            acc = acc + cw_ref[j:j + 1, sl] * xpad_ref[off:off + tm, sl]
        xc_ref[:, sl] = (acc * jax.nn.sigmoid(acc)).astype(bf16)
        xpad_ref[0:HALO_ROWS, sl] = xpad_ref[tm:tm + HALO_ROWS, sl]

    k_unscale = 1.0 / k_scale
    gates = [jnp.zeros((tm, LANES), f32)]

    def block_diag_qkv(g):
        sl = slice(g * 256, (g + 1) * 256)
        xc_g = xc_ref[:, sl]
        xm_g = xpad_ref[HALO_ROWS:HALO_ROWS + tm, sl].astype(bf16)
        q = _dot(xc_g, wmq_ref[g])
        v = _dot(xm_g, wmv_ref[g])
        q_bf = q.astype(bf16)
        v_bf = v.astype(bf16)
        kk = (_dot(xc_g, wmk_ref[g]) * k_scale).astype(bf16)
        mk_ref[:, sl] = kk
        qT = q.T.astype(bf16)
        vT = v.T.astype(bf16)
        for u in range(tm // L):
            cols = slice(u * L, (u + 1) * L)
            mqT_ref[u, sl, :] = qT[:, cols]
            mvT_ref[u, sl, :] = vT[:, cols]
        ks = slice(W + g * 256, W + (g + 1) * 256)
        vs = slice(2 * W + g * 256, 2 * W + (g + 1) * 256)
        gates[0] = (gates[0] + _dot(q_bf, wg_ref[sl, :]) + k_unscale * _dot(kk, wg_ref[ks, :])
                    + _dot(v_bf, wg_ref[vs, :]))

    def gate_outputs():
        gates_t = (gates[0] + bg_ref[...]).T
        li = gates_t[0:GATE_ROWS, :] * LOG2E
        lf = _log_sigmoid(gates_t[GATE_ROWS:2 * GATE_ROWS, :]) * LOG2E
        row = lax.broadcasted_iota(jnp.int32, (L, L), 0)
        col = lax.broadcasted_iota(jnp.int32, (L, L), 1)
        upper = jnp.where(row <= col, 1.0, 0.0).astype(bf16)
        gl_ref[...] = li
        for u in range(tm // L):
            cols = slice(u * L, (u + 1) * L)
            b = sum(_dot(part, upper) for part in _split3(lf[:, cols]))
            gb_ref[:, cols] = b
            a_pad = jnp.concatenate([li[:, cols] - b, jnp.zeros((LANES - GATE_ROWS, L), f32)], axis=0)
            ga_ref[cols, :] = a_pad.T[:, 0:GATE_ROWS]

    big = ([functools.partial(token_major, cc) for cc in range(pr_ref.shape[1] // 512)]
           + [functools.partial(att_qv, cc) for cc in range(ATT_WIDTH // 256)])
    conv = [functools.partial(conv_silu, g) for g in range(W // 256)]
    proj = [functools.partial(block_diag_qkv, g) for g in range(W // 256)]
    prep = conv + proj + [gate_outputs]
    assert len(big) >= len(prep)
    for n, task in enumerate(big):
        task()
        if n < len(prep):
            prep[n]()


def _inproj(x2, g_pre, wq, wv, wr, wx, conv_w, conv_b, wmq, wmk, wmv, wg, bg, seq):
    T, D = x2.shape
    W = MLSTM_WIDTH
    tm = TOKEN_TILE
    L = MLSTM_CHUNK
    nblk = tm // MOBA_BLOCK
    ncols = wr.shape[1]
    resident = dict(pipeline_mode=pl.Buffered(1))
    const2 = lambda i: (0, 0)
    const3 = lambda i: (0, 0, 0)
    tok = pl.BlockSpec((tm, W), lambda i: (i, 0))
    att_T = pl.BlockSpec((nblk, ATT_WIDTH, MOBA_BLOCK), lambda i: (i, 0, 0))
    chunk_T = pl.BlockSpec((tm // L, W, L), lambda i: (i, 0, 0))
    rows8 = pl.BlockSpec((GATE_ROWS, tm), lambda i: (0, i))
    return pl.pallas_call(
        functools.partial(_inproj_kernel, tm=tm, seq=seq, q_scale=ATT_HEAD_DIM ** -0.5 * LOG2E,
                          k_scale=MLSTM_HEAD_DIM ** -0.5),
        grid=(T // tm,),
        in_specs=[
            pl.BlockSpec((tm, D), lambda i: (i, 0)),
            pl.BlockSpec((1, D), const2),
            pl.BlockSpec((D, ATT_WIDTH), const2, **resident),
            pl.BlockSpec((D, ATT_WIDTH), const2, **resident),
            pl.BlockSpec((D, ncols), const2, **resident),
            pl.BlockSpec((D, W), const2, **resident),
            pl.BlockSpec((CONV_WIDTH, W), const2),
            pl.BlockSpec((1, W), const2),
            pl.BlockSpec((W // 256, 256, 256), const3, **resident),
            pl.BlockSpec((W // 256, 256, 256), const3, **resident),
            pl.BlockSpec((W // 256, 256, 256), const3, **resident),
            pl.BlockSpec((3 * W, LANES), const2, **resident),
            pl.BlockSpec((1, LANES), const2),
        ],
        out_specs=[att_T, att_T, pl.BlockSpec((tm, ncols), lambda i: (i, 0)),
                   tok, chunk_T, tok, chunk_T,
                   pl.BlockSpec((tm, GATE_ROWS), lambda i: (i, 0)), rows8, rows8],
        out_shape=[
            jax.ShapeDtypeStruct((T // MOBA_BLOCK, ATT_WIDTH, MOBA_BLOCK), bf16),
            jax.ShapeDtypeStruct((T // MOBA_BLOCK, ATT_WIDTH, MOBA_BLOCK), bf16),
            jax.ShapeDtypeStruct((T, ncols), bf16),
            jax.ShapeDtypeStruct((T, W), bf16),
            jax.ShapeDtypeStruct((T // L, W, L), bf16),
            jax.ShapeDtypeStruct((T, W), bf16),
            jax.ShapeDtypeStruct((T // L, W, L), bf16),
            jax.ShapeDtypeStruct((T, GATE_ROWS), f32),
            jax.ShapeDtypeStruct((GATE_ROWS, T), f32),
            jax.ShapeDtypeStruct((GATE_ROWS, T), f32),
        ],
        scratch_shapes=[pltpu.VMEM((HALO_ROWS + tm, W), f32)],
        compiler_params=pltpu.CompilerParams(
            dimension_semantics=("arbitrary",), vmem_limit_bytes=VMEM_LIMIT),
        name="inproj",
    )(x2, g_pre, wq, wv, wr, wx, conv_w, conv_b, wmq, wmk, wmv, wg, bg)


def _mlstm_kernel(qT_ref, k_ref, vT_ref, xc_ref, z_ref, ga_ref, gb_ref, gl_ref, nw_ref, sk_ref,
                  y_ref, ct_ref, m_ref):
    L = MLSTM_CHUNK
    dh = MLSTM_HEAD_DIM

    @pl.when(pl.program_id(1) == 0)
    def _():
        ct_ref[...] = jnp.zeros_like(ct_ref)
        m_ref[...] = jnp.zeros_like(m_ref)

    s_idx = lax.broadcasted_iota(jnp.int32, (L, L), 0)
    t_idx = lax.broadcasted_iota(jnp.int32, (L, L), 1)
    causal = s_idx <= t_idx
    ones_rows = jnp.ones((SUM_ROWS, L), bf16)
    for u, h in ((u, h) for u in range(MLSTM_CHUNKS_PER_STEP) for h in range(MLSTM_HEADS)):
        sl = slice(h * dh, (h + 1) * dh)
        tok = slice(u * L, (u + 1) * L)
        qT = qT_ref[u, sl, :]
        k = k_ref[tok, sl]
        vT_ext = jnp.concatenate([vT_ref[u, sl, :], ones_rows], axis=0)
        a_c = ga_ref[tok, h:h + 1]
        b_r = gb_ref[h:h + 1, tok]
        li_r = gl_ref[h:h + 1, tok]
        b_last = b_r[:, L - 1:L]
        m_prev = m_ref[h][0:1, 0:1]
        ct = ct_ref[h]

        log_d = jnp.where(causal, a_c + b_r, NEG_BIG)
        inter = b_r + m_prev
        m_t = jnp.maximum(inter, jnp.max(log_d, axis=0, keepdims=True))
        sT = _dot(k, qT) * jnp.exp2(log_d - m_t)
        dec = jnp.exp2(inter - m_t)
        num = _dot(vT_ext, sT.astype(bf16)) + dec * _dot(ct.astype(bf16), qT)
        den = num[dh:dh + 1, :]
        hT = num[0:dh, :] * (1.0 / jnp.maximum(jnp.abs(den), jnp.exp2(-m_t)))

        mu = jnp.mean(hT, axis=0, keepdims=True)
        cen = hT - mu
        var = jnp.mean(cen * cen, axis=0, keepdims=True)
        yn = (cen * lax.rsqrt(var + LN_EPS)).T
        z = z_ref[tok, sl].astype(f32)
        out = (yn * nw_ref[:, sl] + sk_ref[:, sl] * xc_ref[tok, sl].astype(f32)) * (z * jax.nn.sigmoid(z))
        y_ref[tok, sl] = out.astype(y_ref.dtype)

        log_w = b_last - b_r + li_r
        m_new = jnp.maximum(b_last + m_prev, jnp.max(log_w, axis=1, keepdims=True))
        vw = vT_ext * jnp.exp2(log_w - m_new).astype(bf16)
        ct_ref[h] = jnp.exp2(b_last + m_prev - m_new) * ct + _dot(vw, k)
        m_ref[h] = jnp.broadcast_to(m_new, m_ref.shape[1:])


def _mlstm(qT, k, vT, xc, pr, ga, gb, gl, mh_norm, skip, batch, seq):
    T, W = k.shape
    cps = MLSTM_CHUNKS_PER_STEP
    rows = cps * MLSTM_CHUNK
    assert seq % rows == 0
    nc = seq // rows
    z_col = 2
    tok = pl.BlockSpec((rows, W), lambda b, c: (b * nc + c, 0))
    chunkT = pl.BlockSpec((cps, W, MLSTM_CHUNK), lambda b, c: (b * nc + c, 0, 0))
    rows8 = pl.BlockSpec((GATE_ROWS, rows), lambda b, c: (0, b * nc + c))
    return pl.pallas_call(
        _mlstm_kernel,
        grid=(batch, nc),
        in_specs=[chunkT, tok, chunkT, tok,
                  pl.BlockSpec((rows, W), lambda b, c: (b * nc + c, z_col)),
                  pl.BlockSpec((rows, GATE_ROWS), lambda b, c: (b * nc + c, 0)),
                  rows8, rows8,
                  pl.BlockSpec((1, W), lambda b, c: (0, 0)),
                  pl.BlockSpec((1, W), lambda b, c: (0, 0))],
        out_specs=tok,
        out_shape=jax.ShapeDtypeStruct((T, W), bf16),
        scratch_shapes=[pltpu.VMEM((MLSTM_HEADS, MLSTM_HEAD_DIM + SUM_ROWS, MLSTM_HEAD_DIM), f32),
                        pltpu.VMEM((MLSTM_HEADS, 8, LANES), f32)],
        compiler_params=pltpu.CompilerParams(
            dimension_semantics=("arbitrary", "arbitrary"), vmem_limit_bytes=VMEM_LIMIT),
        name="mlstm",
    )(qT, k, vT, xc, pr, ga, gb, gl, mh_norm, skip)


def _moba_kernel(*refs, nb, heads):
    def query_block(sub, c):
        _moba_query_block(sub, *refs, nb=nb, heads=heads)
        return c
    lax.fori_loop(0, MOBA_QBLOCKS_PER_STEP, query_block, 0)


def _moba_query_block(sub, rb_ref, qT_ref, k_ref, vT_ref, g_ref, o_ref, kmean_ref, sel_ref, sa_ref,
                      sb_ref, ca_ref, cb_ref, acc_ref, m_ref, alpha_ref, p_ref, bias_ref, *, nb, heads):
    blk_len = MOBA_BLOCK
    dh = ATT_HEAD_DIM
    own = pl.program_id(2) * MOBA_QBLOCKS_PER_STEP + sub
    q_rows = pl.ds(pl.multiple_of(sub * blk_len, blk_len), blk_len)
    qT_ref = qT_ref.at[sub]
    g_ref = g_ref.at[q_rows]
    o_ref = o_ref.at[q_rows]

    @pl.when(own == 0)
    def _():
        def mean_body(j, c):
            kb = k_ref[pl.ds(pl.multiple_of(j * blk_len, blk_len), blk_len), :].astype(f32)
            kmean_ref[pl.ds(j, 1), :] = jnp.sum(kb, axis=0, keepdims=True) * (1.0 / blk_len)
            return c
        lax.fori_loop(0, nb, mean_body, 0)

        @pl.when(pl.program_id(1) == 0)
        def _():
            for hh in range(heads):
                _write_bias_tiles(rb_ref, pl.program_id(0) * heads + hh, bias_ref.at[hh])

    hsl = [slice(hh * dh, (hh + 1) * dh) for hh in range(heads)]

    def score_matmuls(i, buf, with_tile):
        dst_ref, cmax_ref = buf
        ic = jnp.minimum(i, own)
        rows = pl.ds(pl.multiple_of((own - ic) * blk_len, blk_len), blk_len)
        for hh in range(heads):
            s = _dot(k_ref[rows, hsl[hh]], qT_ref[hsl[hh], :])
            if with_tile:
                s = s + bias_ref[hh, jnp.minimum(ic, BIAS_TILES - 1)]
            dst_ref[hh] = s
            cmax_ref[hh] = jnp.max(s, axis=0, keepdims=True)

    def select_blocks():
        blk = lax.broadcasted_iota(jnp.int32, (nb, blk_len), 0)
        past = blk < own
        for hh in range(heads):
            qT = qT_ref[hsl[hh], :]
            km = kmean_ref[:, hsl[hh]]
            km_hi = km.astype(bf16)
            km_lo = (km - km_hi.astype(f32)).astype(bf16)
            gate = _dot(km_hi, qT) + _dot(km_lo, qT)
            g = jnp.where(past, gate, -jnp.inf)
            sel = blk == own
            for _ in range(MOBA_TOPK):
                mx = jnp.max(g, axis=0, keepdims=True)
                first = jnp.min(jnp.where(g == mx, blk, nb), axis=0, keepdims=True)
                pick = blk == first
                sel = sel | (pick & past)
                g = jnp.where(pick, -jnp.inf, g)
            sel_ref[hh, 0:nb, :] = jnp.where(sel, 0.0, NEG_BIG)
            sel_ref[hh, nb:nb + SEL_PAD, :] = jnp.full((SEL_PAD, blk_len), NEG_BIG, f32)

    ones_rows = jnp.ones((SUM_ROWS, blk_len), bf16)
    m_ref[...] = jnp.full(m_ref.shape, NEG_BIG, f32)
    acc_ref[...] = jnp.zeros(acc_ref.shape, f32)

    head0 = pl.program_id(0) * heads
    far_bias = [rb_ref[REL_BUCKETS - 1, head0 + hh] * LOG2E for hh in range(heads)]

    def softmax(i, buf):
        src_ref, cmax_ref = buf
        mask_row = jnp.where(i <= own, own - i, nb)
        stats = []
        for hh in range(heads):
            m = m_ref[hh]
            shift = jnp.where(i >= FIRST_BARE_STEP, far_bias[hh], 0.0)
            selrow = sel_ref[hh, pl.ds(mask_row, 1), :]
            m_new = jnp.maximum(m, cmax_ref[hh] + (selrow + shift))
            p = jnp.exp2(src_ref[hh] - jnp.where(selrow < 0.0, -NEG_BIG, m_new - shift)).astype(bf16)
            m_ref[hh] = m_new
            stats.append((jnp.exp2(m - m_new), p))
        return stats

    def value_update(i, stats):
        block = jnp.maximum(own - i, 0)
        for hh in range(heads):
            alpha, p = stats[hh]
            lhs = jnp.concatenate([vT_ref[block, hsl[hh], :], ones_rows], axis=0)
            acc_ref[hh] = alpha * acc_ref[hh] + _dot(lhs, p)

    def keep_pending(stats):
        for hh in range(heads):
            alpha_ref[hh], p_ref[hh] = stats[hh]

    even_buf, odd_buf = (sa_ref, ca_ref), (sb_ref, cb_ref)
    score_buf = (even_buf, odd_buf)

    def trip(i, n, with_tile):
        value_update(i, [(alpha_ref[hh], p_ref[hh]) for hh in range(heads)])
        score_matmuls(i + 2, even_buf, with_tile(i + 2))
        for t in range(1, n):
            stats = softmax(i + t, score_buf[t % 2])
            if n > 2:
                score_matmuls(i + 2 + t, score_buf[t % 2], with_tile(i + 2 + t))
                value_update(i + t, stats)
            else:
                value_update(i + t, stats)
                score_matmuls(i + 2 + t, score_buf[t % 2], with_tile(i + 2 + t))
        keep_pending(softmax(i + n, even_buf))

    select_blocks()
    score_matmuls(0, even_buf, True)
    score_matmuls(1, odd_buf, True)
    keep_pending(softmax(0, even_buf))

    n_steps = own + 1
    n_bare = jnp.maximum(n_steps - TILED_STEPS, 0)
    n_long = (n_bare + LONG_TRIP_SLACK) // LONG_TRIP
    n_left = jnp.maximum(n_bare - LONG_TRIP * n_long, 0)
    n_mid = n_left // MID_TRIP
    n_short = (n_left - MID_TRIP * n_mid + 1) // 2

    whole_tiled = n_steps >= TILED_STEPS - TILED_TRIP_SLACK

    def tiled_long_body(t, c):
        trip(0, TILED_STEPS, lambda step: step < FIRST_BARE_STEP)
        return c

    def tiled_short_body(t, c):
        trip(2 * t, 2, lambda step: True)
        return c

    def long_body(t, c):
        trip(TILED_STEPS + LONG_TRIP * t, LONG_TRIP, lambda step: False)
        return c

    def mid_body(t, c):
        trip(TILED_STEPS + LONG_TRIP * n_long + MID_TRIP * t, MID_TRIP, lambda step: False)
        return c

    def short_body(t, c):
        trip(TILED_STEPS + LONG_TRIP * n_long + MID_TRIP * n_mid + 2 * t, 2, lambda step: False)
        return c

    lax.fori_loop(0, jnp.where(whole_tiled, 1, 0), tiled_long_body, 0)
    lax.fori_loop(0, jnp.where(whole_tiled, 0, (n_steps + 1) // 2), tiled_short_body, 0)
    lax.fori_loop(0, n_long, long_body, 0)
    lax.fori_loop(0, n_mid, mid_body, 0)
    lax.fori_loop(0, n_short, short_body, 0)
    for hh in range(heads):
        acc = acc_ref[hh]
        o = (acc[0:dh, :] * (1.0 / acc[dh:dh + 1, :])).T
        gg = g_ref[:, hsl[hh]].astype(f32)
        o_ref[:, hsl[hh]] = (o * (gg * jax.nn.sigmoid(gg))).astype(o_ref.dtype)


def _moba(rel_bias, qT, vT, pr, batch, seq):
    T = pr.shape[0]
    nb = seq // MOBA_BLOCK
    G = MOBA_HEADS_PER_STEP
    gw = G * ATT_HEAD_DIM
    gate_col0 = ATT_WIDTH // gw
    qb = MOBA_QBLOCKS_PER_STEP
    assert nb % qb == 0
    steps = nb // qb
    vT4 = vT.reshape(batch, nb, ATT_WIDTH, MOBA_BLOCK)
    return pl.pallas_call(
        functools.partial(_moba_kernel, nb=nb, heads=G),
        grid=(ATT_HEADS // G, batch, steps),
        in_specs=[
            pl.BlockSpec(memory_space=pltpu.SMEM),
            pl.BlockSpec((qb, gw, MOBA_BLOCK), lambda h, b, i: (b * steps + i, h, 0)),
            pl.BlockSpec((seq, gw), lambda h, b, i: (b, h)),
            pl.BlockSpec((None, nb, gw, MOBA_BLOCK), lambda h, b, i: (b, 0, h, 0)),
            pl.BlockSpec((qb * MOBA_BLOCK, gw), lambda h, b, i: (b * steps + i, gate_col0 + h)),
        ],
        out_specs=pl.BlockSpec((qb * MOBA_BLOCK, gw), lambda h, b, i: (b * steps + i, h)),
        out_shape=jax.ShapeDtypeStruct((T, ATT_WIDTH), bf16),
        scratch_shapes=[pltpu.VMEM((nb, gw), f32),
                        pltpu.VMEM((G, nb + SEL_PAD, MOBA_BLOCK), f32),
                        pltpu.VMEM((G, MOBA_BLOCK, MOBA_BLOCK), f32),
                        pltpu.VMEM((G, MOBA_BLOCK, MOBA_BLOCK), f32),
                        pltpu.VMEM((G, 1, MOBA_BLOCK), f32),
                        pltpu.VMEM((G, 1, MOBA_BLOCK), f32),
                        pltpu.VMEM((G, ATT_HEAD_DIM + SUM_ROWS, MOBA_BLOCK), f32),
                        pltpu.VMEM((G, 1, MOBA_BLOCK), f32),
                        pltpu.VMEM((G, 1, MOBA_BLOCK), f32),
                        pltpu.VMEM((G, MOBA_BLOCK, MOBA_BLOCK), bf16),
                        pltpu.VMEM((G, BIAS_TILES, MOBA_BLOCK, MOBA_BLOCK), f32)],
        compiler_params=pltpu.CompilerParams(
            dimension_semantics=("arbitrary", "arbitrary", "arbitrary"),
            vmem_limit_bytes=VMEM_LIMIT),
        name="moba",
    )(rel_bias, qT, pr, vT4, pr)


def _outproj_kernel(ya_ref, ym_ref, wa_ref, wm_ref, g_ref, x_ref, o_ref):
    y = _dot(ya_ref[...], wa_ref[...]) + _dot(ym_ref[...], wm_ref[...])
    ms = jnp.mean(y * y, axis=-1, keepdims=True)
    o_ref[...] = x_ref[...] + y * lax.rsqrt(ms + RMS_EPS) * g_ref[...]


def _outproj(ya, ym, wa, wm, g_post, x2):
    T, D = x2.shape
    tm = OUT_TOKEN_TILE
    assert T % tm == 0
    resident = dict(pipeline_mode=pl.Buffered(1))
    return pl.pallas_call(
        _outproj_kernel,
        grid=(T // tm,),
        in_specs=[
            pl.BlockSpec((tm, ya.shape[1]), lambda i: (i, 0)),
            pl.BlockSpec((tm, ym.shape[1]), lambda i: (i, 0)),
            pl.BlockSpec(wa.shape, lambda i: (0, 0), **resident),
            pl.BlockSpec(wm.shape, lambda i: (0, 0), **resident),
            pl.BlockSpec((1, D), lambda i: (0, 0)),
            pl.BlockSpec((tm, D), lambda i: (i, 0)),
        ],
        out_specs=pl.BlockSpec((tm, D), lambda i: (i, 0)),
        out_shape=jax.ShapeDtypeStruct((T, D), f32),
        compiler_params=pltpu.CompilerParams(
            dimension_semantics=("arbitrary",), vmem_limit_bytes=VMEM_LIMIT),
        name="outproj",
    )(ya, ym, wa, wm, g_post, x2)


def _block_diag_256(w):
    width = w.shape[0] * QKV_BLOCK
    rows = w.reshape(width, QKV_BLOCK)
    col = np.arange(256)
    spread = jnp.asarray(col[None, :] % QKV_BLOCK == np.arange(QKV_BLOCK)[:, None], w.dtype)
    tiled = jnp.dot(rows, spread, precision=lax.Precision.HIGHEST)
    same_block = (np.arange(width)[:, None] % 256) // QKV_BLOCK == col[None, :] // QKV_BLOCK
    dense = jnp.where(jnp.asarray(same_block), tiled, 0.0)
    return dense.reshape(-1, 256, 256)


def _gate_lanes(a):
    nh = MLSTM_HEADS
    zeros = lambda n: jnp.zeros(a.shape[:-1] + (n,), a.dtype)
    return jnp.concatenate([a[..., 0:nh], zeros(GATE_ROWS - nh), a[..., nh:2 * nh],
                            zeros(LANES - GATE_ROWS - nh)], axis=-1)


def _layer(x, rel_bias, g_pre, g_post, w_in, conv_w, conv_b, wq_m, wk_m, wv_m,
           w_if, b_if, mh_norm, skip, w_out):
    batch, seq, d_model = x.shape
    assert seq % TOKEN_TILE == 0 and TOKEN_TILE % MLSTM_CHUNK == 0 and TOKEN_TILE % MOBA_BLOCK == 0
    aw, nh = ATT_WIDTH, MLSTM_HEADS
    x2 = x.reshape(batch * seq, d_model)

    w_bf = w_in.astype(bf16)
    wq = w_bf[:, 0:aw]
    wv = w_bf[:, 2 * aw:3 * aw]
    mw = MLSTM_WIDTH
    wr = jnp.concatenate([w_bf[:, aw:2 * aw], w_bf[:, 3 * aw:4 * aw], w_bf[:, 4 * aw + mw:]], axis=1)
    wx = w_bf[:, 4 * aw:4 * aw + mw]
    wq_bd = _block_diag_256(wq_m).astype(bf16)
    wk_bd = _block_diag_256(wk_m).astype(bf16)
    wv_bd = _block_diag_256(wv_m).astype(bf16)
    wg = _gate_lanes(w_if).astype(bf16)
    bg = _gate_lanes(b_if).reshape(1, LANES)
    w_out_bf = w_out.astype(bf16)

    qT, vT, pr, xc, qmT, km, vmT, ga, gb, gl = _inproj(
        x2, g_pre.reshape(1, -1), wq, wv, wr, wx, conv_w, conv_b.reshape(1, -1),
        wq_bd, wk_bd, wv_bd, wg, bg, seq)
    ym = _mlstm(qmT, km, vmT, xc, pr, ga, gb, gl, mh_norm.reshape(1, -1), skip.reshape(1, -1),
                batch, seq)
    ya = _moba(rel_bias, qT, vT, pr, batch, seq)
    out = _outproj(ya, ym, w_out_bf[0:aw], w_out_bf[aw:], g_post.reshape(1, -1), x2)
    return out.reshape(batch, seq, d_model)


def kernel(x, rel_bias, g_pre, g_post, w_in, conv_w, conv_b, wq_m, wk_m, wv_m, w_if, b_if,
           mh_norm, skip, w_out):
    depth = w_in.shape[0]
    for l in range(depth):
        x = _layer(x, rel_bias, g_pre[l], g_post[l], w_in[l], conv_w[l], conv_b[l], wq_m[l],
                   wk_m[l], wv_m[l], w_if[l], b_if[l], mh_norm[l], skip[l], w_out[l])
    return x
```
